```python
import jax, jax.numpy as jnp
from jax import lax
import numpy as np

D_MODEL = 1024
BATCH = 8
SEQ = 2048
DEPTH = 1

MEM_LEN = 256
EPS = 1e-6
ATTN_HEADS = 8
ATTN_KV_HEADS = 2
ATTN_HEAD_DIM = 64
ATTN_GROUP = ATTN_HEADS // ATTN_KV_HEADS
WINDOW = 128
ATTN_BLOCK = 128
ROPE_THETA = 10000.0
DN_HEADS = 4
DN_HEAD_K = 128
DN_HEAD_V = 128
DN_CONV = 5
DN_CHUNK = 64
N_DIRS = 2
CROSS_HEADS = 4
CROSS_HEAD_DIM = D_MODEL // CROSS_HEADS
D_FF = -(-8 * D_MODEL // (3 * 256)) * 256
ATTN_WIDTH = ATTN_HEADS * ATTN_HEAD_DIM
DN_WIDTH = DN_HEADS * DN_HEAD_V
MIX_WIDTH = ATTN_WIDTH + DN_WIDTH
DN_QKV = DN_HEADS * (2 * DN_HEAD_K + DN_HEAD_V)
IN_SIZES = [ATTN_WIDTH, ATTN_KV_HEADS * ATTN_HEAD_DIM, ATTN_KV_HEADS * ATTN_HEAD_DIM,
            DN_QKV, DN_WIDTH, N_DIRS * DN_HEADS, N_DIRS * DN_HEADS]
IN_COLS = int(np.sum(IN_SIZES))
IN_SPLITS = [int(v) for v in np.cumsum(IN_SIZES)[:-1]]

kernel_name = 'hybrid_swa_gdn_memory_block'


def _rms_norm(x, gain):
    xf = x.astype(jnp.float32)
    y = xf * lax.rsqrt(jnp.mean(xf * xf, axis=-1, keepdims=True) + EPS)
    return (y * gain.astype(jnp.float32)).astype(x.dtype)


def _l2norm(x):
    return x * lax.rsqrt(jnp.sum(x * x, axis=-1, keepdims=True) + EPS)


def _rope(x, positions):
    half = x.shape[-1] // 2
    inv_freq = ROPE_THETA ** (-jnp.arange(half, dtype=jnp.float32) / half)
    ang = positions.astype(jnp.float32)[..., None] * inv_freq
    cos = jnp.cos(ang)[:, :, None, :]
    sin = jnp.sin(ang)[:, :, None, :]
    xf = x.astype(jnp.float32)
    x1, x2 = xf[..., :half], xf[..., half:]
    return jnp.concatenate([x1 * cos - x2 * sin, x2 * cos + x1 * sin], axis=-1)


def _window_attention(q, k, v, sink):
    b, s = q.shape[0], q.shape[1]
    nb = s // ATTN_BLOCK
    qb = q.reshape(b, nb, ATTN_BLOCK, ATTN_KV_HEADS, ATTN_GROUP, ATTN_HEAD_DIM)

    def band(t):
        tp = t.reshape(b, nb, ATTN_BLOCK, ATTN_KV_HEADS, ATTN_HEAD_DIM)
        tp = jnp.pad(tp, ((0, 0), (1, 1), (0, 0), (0, 0), (0, 0)))
        return jnp.concatenate([tp[:, :-2], tp[:, 1:-1], tp[:, 2:]], axis=2)

    kw, vw = band(k), band(v)
    scores = jnp.einsum('bnqhgd,bnkhd->bnhgqk', qb, kw) * (ATTN_HEAD_DIM ** -0.5)
    qi = jnp.arange(ATTN_BLOCK)[:, None]
    ki = jnp.arange(3 * ATTN_BLOCK)[None, :]
    in_band = jnp.abs(ki - ATTN_BLOCK - qi) <= WINDOW
    kpos = (jnp.arange(nb)[:, None] - 1) * ATTN_BLOCK + jnp.arange(3 * ATTN_BLOCK)[None, :]
    in_seq = (kpos >= 0) & (kpos < s)
    mask = in_band[None, :, :] & in_seq[:, None, :]
    scores = jnp.where(mask[None, :, None, None], scores, -jnp.inf)
    sk = sink.astype(jnp.float32).reshape(ATTN_KV_HEADS, ATTN_GROUP)[None, None, :, :, None, None]
    m = jnp.maximum(jnp.max(scores, axis=-1, keepdims=True), sk)
    p = jnp.exp(scores - m)
    denom = jnp.sum(p, axis=-1, keepdims=True) + jnp.exp(sk - m)
    o = jnp.einsum('bnhgqk,bnkhd->bnqhgd', p / denom, vw)
    return o.reshape(b, s, ATTN_WIDTH)


def _gated_delta_chunked(q, k, v, g, beta):
    b, h, s, dk = q.shape
    dv = v.shape[-1]
    c = DN_CHUNK
    n = s // c
    q = q.reshape(b, h, n, c, dk)
    k = k.reshape(b, h, n, c, dk)
    v = v.reshape(b, h, n, c, dv)
    g = jnp.cumsum(g.reshape(b, h, n, c), axis=-1)
    beta = beta.reshape(b, h, n, c)
    incl = jnp.tril(jnp.ones((c, c), dtype=bool))
    strict = jnp.tril(jnp.ones((c, c), dtype=bool), k=-1)
    decay = jnp.exp(jnp.where(incl, g[..., :, None] - g[..., None, :], -jnp.inf))
    kb = k * beta[..., None]
    lower = jnp.where(strict, jnp.einsum('bhnid,bhnjd->bhnij', kb, k) * decay, 0.0)
    tmat = lower + jnp.eye(c, dtype=jnp.float32)
    rhs = jnp.concatenate([v * beta[..., None], kb * jnp.exp(g)[..., None]], axis=-1)
    sol = lax.linalg.triangular_solve(tmat, rhs, left_side=True, lower=True, unit_diagonal=True)
    u, w = sol[..., :dv], sol[..., dv:]
    qk = jnp.einsum('bhnid,bhnjd->bhnij', q, k) * decay
    qg = q * jnp.exp(g)[..., None]
    kg = k * jnp.exp(g[..., -1:] - g)[..., None]
    g_last = jnp.exp(g[..., -1])

    def step(state, inp):
        qg_i, kg_i, u_i, w_i, qk_i, gl_i = inp
        v_new = u_i - jnp.einsum('bhck,bhkv->bhcv', w_i, state)
        o_i = jnp.einsum('bhck,bhkv->bhcv', qg_i, state) + jnp.einsum('bhij,bhjv->bhiv', qk_i, v_new)
        state = state * gl_i[..., None, None] + jnp.einsum('bhck,bhcv->bhkv', kg_i, v_new)
        return state, o_i

    xs = tuple(jnp.moveaxis(t, 2, 0) for t in (qg, kg, u, w, qk, g_last))
    state0 = jnp.zeros((b, h, dk, dv), jnp.float32)
    _, o = lax.scan(step, state0, xs)
    return jnp.moveaxis(o, 0, 2).reshape(b, h, s, dv)


def _gated_deltanet(qkv, gate, a, bt, conv_w, a_log, dt_bias, g_out):
    b, s = qkv.shape[0], qkv.shape[1]
    ch = qkv.shape[-1]
    pad = DN_CONV // 2
    conv = lax.conv_general_dilated(qkv, conv_w.reshape(DN_CONV, 1, ch).astype(qkv.dtype),
                                    window_strides=(1,), padding=[(pad, pad)],
                                    dimension_numbers=('NWC', 'WIO', 'NWC'), feature_group_count=ch)
    conv = jax.nn.silu(conv.astype(jnp.float32))
    nq = DN_HEADS * DN_HEAD_K
    q = conv[..., :nq].reshape(b, s, DN_HEADS, DN_HEAD_K)
    k = conv[..., nq:2 * nq].reshape(b, s, DN_HEADS, DN_HEAD_K)
    v = conv[..., 2 * nq:].reshape(b, s, DN_HEADS, DN_HEAD_V)
    q = (_l2norm(q) * (DN_HEAD_K ** -0.5)).transpose(0, 2, 1, 3)
    k = _l2norm(k).transpose(0, 2, 1, 3)
    v = v.transpose(0, 2, 1, 3)
    a = a.astype(jnp.float32).reshape(b, s, N_DIRS, DN_HEADS)
    bt = bt.astype(jnp.float32).reshape(b, s, N_DIRS, DN_HEADS)
    g = -jnp.exp(a_log.astype(jnp.float32)) * jax.nn.softplus(a + dt_bias.astype(jnp.float32))
    g = g.transpose(2, 0, 3, 1)
    beta = jax.nn.sigmoid(bt).transpose(2, 0, 3, 1)
    o_fwd = _gated_delta_chunked(q, k, v, g[0], beta[0])
    flip = lambda t: jnp.flip(t, axis=2)
    o_bwd = flip(_gated_delta_chunked(flip(q), flip(k), flip(v), flip(g[1]), flip(beta[1])))
    o = (o_fwd + o_bwd).transpose(0, 2, 1, 3)
    o = o * lax.rsqrt(jnp.mean(o * o, axis=-1, keepdims=True) + EPS) * g_out.astype(jnp.float32)
    o = o * jax.nn.silu(gate.astype(jnp.float32).reshape(b, s, DN_HEADS, DN_HEAD_V))
    return o.reshape(b, s, DN_WIDTH)


def _memory_cross_attention(h, mem_n, w_q, w_kv, w_o):
    b, s = h.shape[0], h.shape[1]
    m = mem_n.shape[1]
    q = (h @ w_q).reshape(b, s, CROSS_HEADS, CROSS_HEAD_DIM).astype(jnp.float32)
    kv = mem_n @ w_kv
    k = kv[..., :D_MODEL].reshape(b, m, CROSS_HEADS, CROSS_HEAD_DIM).astype(jnp.float32)
    v = kv[..., D_MODEL:].reshape(b, m, CROSS_HEADS, CROSS_HEAD_DIM).astype(jnp.float32)
    p = jax.nn.softmax(jnp.einsum('bqhd,bkhd->bhqk', q, k) * (CROSS_HEAD_DIM ** -0.5), axis=-1)
    o = jnp.einsum('bhqk,bkhd->bqhd', p, v).reshape(b, s, D_MODEL).astype(h.dtype)
    return o @ w_o


def _swiglu(h, w_gate_up, w_down):
    gu = h @ w_gate_up
    return (jax.nn.silu(gu[..., :D_FF]) * gu[..., D_FF:]) @ w_down


def setup_inputs(seed: int = 0) -> dict:
    key = jax.random.key(seed)
    ks = jax.random.split(key, 24)
    f32 = jnp.float32

    def dense(k, fan_in, fan_out):
        return jax.random.normal(k, (DEPTH, fan_in, fan_out), f32) * fan_in ** -0.5

    def gain(k, n):
        return 1.0 + 0.02 * jax.random.normal(k, (DEPTH, n), f32)

    x = jax.random.normal(ks[0], (BATCH, SEQ, D_MODEL), f32)
    mem = jax.random.normal(ks[1], (BATCH, MEM_LEN, D_MODEL), f32)
    offsets = jax.random.randint(ks[2], (BATCH, 1), 0, 4096, dtype=jnp.int32)
    positions = jnp.arange(SEQ, dtype=jnp.int32)[None, :] + offsets
    dt = jnp.exp(jax.random.uniform(ks[3], (DEPTH, N_DIRS, DN_HEADS), f32)
                 * (np.log(0.1) - np.log(0.001)) + np.log(0.001))
    return {
        'x': x,
        'mem': mem,
        'positions': positions,
        'g_mix_pre': gain(ks[4], D_MODEL),
        'w_in': dense(ks[5], D_MODEL, IN_COLS),
        'conv_w': jax.random.normal(ks[6], (DEPTH, DN_CONV, DN_QKV), f32) * DN_CONV ** -0.5,
        'a_log': jnp.log(jax.random.uniform(ks[7], (DEPTH, N_DIRS, DN_HEADS), f32, 1.0, 16.0)),
        'dt_bias': dt + jnp.log(-jnp.expm1(-dt)),
        'g_dn_out': gain(ks[8], DN_HEAD_V),
        'attn_sink': 0.5 * jax.random.normal(ks[9], (DEPTH, ATTN_HEADS), f32),
        'w_out': dense(ks[10], MIX_WIDTH, D_MODEL),
        'g_mix_post': gain(ks[11], D_MODEL),
        'g_cross_pre': gain(ks[12], D_MODEL),
        'g_mem': gain(ks[13], D_MODEL),
        'w_cq': dense(ks[14], D_MODEL, D_MODEL),
        'w_ckv': dense(ks[15], D_MODEL, 2 * D_MODEL),
        'w_co': dense(ks[16], D_MODEL, D_MODEL),
        'g_cross_post': gain(ks[17], D_MODEL),
        'g_ffn_pre': gain(ks[18], D_MODEL),
        'w_gate_up': dense(ks[19], D_MODEL, 2 * D_FF),
        'w_down': dense(ks[20], D_FF, D_MODEL),
        'g_ffn_post': gain(ks[21], D_MODEL),
    }


def reference(x, mem, positions, g_mix_pre, w_in, conv_w, a_log, dt_bias, g_dn_out, attn_sink,
              w_out, g_mix_post, g_cross_pre, g_mem, w_cq, w_ckv, w_co, g_cross_post,
              g_ffn_pre, w_gate_up, w_down, g_ffn_post):
    b, s = x.shape[0], x.shape[1]
    for l in range(DEPTH):
        h = _rms_norm(x, g_mix_pre[l])
        proj = h @ w_in[l]
        aq, ak, av, dqkv, dgate, dalpha, dbeta = jnp.split(proj, IN_SPLITS, axis=-1)
        aq = _rope(aq.reshape(b, s, ATTN_HEADS, ATTN_HEAD_DIM), positions)
        ak = _rope(ak.reshape(b, s, ATTN_KV_HEADS, ATTN_HEAD_DIM), positions)
        av = av.reshape(b, s, ATTN_KV_HEADS, ATTN_HEAD_DIM).astype(jnp.float32)
        attn_o = _window_attention(aq, ak, av, attn_sink[l]).astype(x.dtype)
        dn_o = _gated_deltanet(dqkv, dgate, dalpha, dbeta, conv_w[l], a_log[l], dt_bias[l],
                               g_dn_out[l]).astype(x.dtype)
        mix = jnp.concatenate([attn_o, dn_o], axis=-1) @ w_out[l]
        x = x + _rms_norm(mix, g_mix_post[l])
        c = _memory_cross_attention(_rms_norm(x, g_cross_pre[l]), _rms_norm(mem, g_mem[l]),
                                    w_cq[l], w_ckv[l], w_co[l])
        x = x + _rms_norm(c, g_cross_post[l])
        f = _swiglu(_rms_norm(x, g_ffn_pre[l]), w_gate_up[l], w_down[l])
        x = x + _rms_norm(f, g_ffn_post[l])
    return x
```

```python
import functools

import jax
import jax.numpy as jnp
from jax import lax
from jax.experimental import pallas as pl
from jax.experimental.pallas import tpu as pltpu

F32 = jnp.float32
BF16 = jnp.bfloat16

EPS = 1e-6
ROPE_THETA = 10000.0
ATTN_HEADS = 8
ATTN_KV_HEADS = 2
ATTN_HEAD_DIM = 64
ATTN_BLOCK = 128
DN_HEADS = 4
DN_HEAD_DIM = 128
DN_CONV = 5
N_DIRS = 2
CROSS_HEADS = 4

LANES = 128
DN_CHUNK = 128
TRI_BASE = 16
CONV_HALO = 8
VMEM_LIMIT = 56 * 1024 * 1024


def _cparams(n_grid_dims):
    return pltpu.CompilerParams(
        dimension_semantics=("arbitrary",) * n_grid_dims,
        vmem_limit_bytes=VMEM_LIMIT,
    )


def _rms(x, gain):
    return x * lax.rsqrt(jnp.mean(x * x, axis=-1, keepdims=True) + EPS) * gain


def _silu(x):
    return x * jax.nn.sigmoid(x)


def _softplus(x):
    return jnp.maximum(x, 0.0) + jnp.log(1.0 + jnp.exp(-jnp.abs(x)))


def _const_spec(shape):
    nd = len(shape)
    return pl.BlockSpec(shape, lambda *_: (0,) * nd)


def _rope_table_kernel(pos_ref, invf_ref, cos_ref, sin_ref):
    ang = pos_ref[...].astype(F32) * invf_ref[...]
    cos_ref[...] = jnp.cos(ang)
    sin_ref[...] = jnp.sin(ang)


def _rope_tables(pos_exp, invf_row):
    rows = pos_exp.shape[0]
    blk = min(rows, 512)
    spec = pl.BlockSpec((blk, LANES), lambda i: (i, 0))
    return pl.pallas_call(
        _rope_table_kernel,
        out_shape=(jax.ShapeDtypeStruct((rows, LANES), F32),) * 2,
        grid=(rows // blk,),
        in_specs=[spec, _const_spec((1, LANES))],
        out_specs=(spec, spec),
        compiler_params=_cparams(1),
        name="rope_table",
    )(pos_exp, invf_row)


def _in_proj_kernel(x_ref, gain_ref, w_ref, wab_ref, wabt_ref, cos_ref, sin_ref,
                    q_ref, k_ref, v_ref, dqkv_ref, gate_ref, abc_ref, abr_ref):
    h = _rms(x_ref[...], gain_ref[...]).astype(BF16)
    cos = cos_ref[...]
    sin = sin_ref[...]
    lane = lax.broadcasted_iota(jnp.int32, (1, LANES), 1)
    first_half = (lane % ATTN_HEAD_DIM) < (ATTN_HEAD_DIM // 2)

    def rope(t):
        rot = jnp.where(first_half, -pltpu.roll(t, LANES - 32, 1), pltpu.roll(t, 32, 1))
        return t * cos + rot * sin

    def proj(lo, hi):
        return jnp.dot(h, w_ref[:, lo:hi], preferred_element_type=F32)

    aw = ATTN_HEADS * ATTN_HEAD_DIM
    kvw = ATTN_KV_HEADS * ATTN_HEAD_DIM
    for m in range(aw // LANES):
        q = rope(proj(m * LANES, (m + 1) * LANES)) * (ATTN_HEAD_DIM ** -0.5)
        q_ref[:, m * LANES:(m + 1) * LANES] = q.astype(BF16)
    k_ref[...] = rope(proj(aw, aw + kvw)).astype(BF16)
    v_ref[...] = proj(aw + kvw, aw + 2 * kvw).astype(BF16)
    off = aw + 2 * kvw
    dn = dqkv_ref.shape[1]
    dqkv_ref[...] = proj(off, off + dn).astype(BF16)
    gate_ref[...] = proj(off + dn, off + dn + gate_ref.shape[1]).astype(BF16)
    abc_ref[...] = jnp.dot(h, wab_ref[...], preferred_element_type=F32)
    abr = lax.dot_general(wabt_ref[...], h, (((1,), (1,)), ((), ())), preferred_element_type=F32)
    for j in range(abr_ref.shape[0]):
        abr_ref[j] = abr[:, j * DN_CHUNK:(j + 1) * DN_CHUNK]


def _in_proj(x2, gain, w_main, w_ab, w_abt, cos_t, sin_t, tm):
    t, d = x2.shape
    aw = ATTN_HEADS * ATTN_HEAD_DIM
    kvw = ATTN_KV_HEADS * ATTN_HEAD_DIM
    dnw = DN_HEADS * DN_HEAD_DIM
    n_ab = w_abt.shape[0]
    row = lambda w: pl.BlockSpec((tm, w), lambda i: (i, 0))
    out_shape = (
        jax.ShapeDtypeStruct((t, aw), BF16),
        jax.ShapeDtypeStruct((t, kvw), BF16),
        jax.ShapeDtypeStruct((t, kvw), BF16),
        jax.ShapeDtypeStruct((t, 3 * dnw), BF16),
        jax.ShapeDtypeStruct((t, dnw), BF16),
        jax.ShapeDtypeStruct((t, LANES), F32),
        jax.ShapeDtypeStruct((t // DN_CHUNK, n_ab, DN_CHUNK), F32),
    )
    out_specs = (
        row(aw), row(kvw), row(kvw), row(3 * dnw), row(dnw), row(LANES),
        pl.BlockSpec((tm // DN_CHUNK, n_ab, DN_CHUNK), lambda i: (i, 0, 0)),
    )
    return pl.pallas_call(
        _in_proj_kernel,
        out_shape=out_shape,
        grid=(t // tm,),
        in_specs=[row(d), _const_spec((1, d)), _const_spec(w_main.shape), _const_spec(w_ab.shape),
                  _const_spec(w_abt.shape), row(LANES), row(LANES)],
        out_specs=out_specs,
        compiler_params=_cparams(1),
        name="in_proj",
    )(x2, gain, w_main, w_ab, w_abt, cos_t, sin_t)


def _win_attn_kernel(sink_ref, q_ref, kp_ref, kc_ref, kn_ref, vp_ref, vc_ref, vn_ref, o_ref):
    n = pl.program_id(1)
    nb = pl.num_programs(1)
    blk = ATTN_BLOCK
    hd = ATTN_HEAD_DIM
    kcat = jnp.concatenate([kp_ref[...], kc_ref[...], kn_ref[...]], axis=0)
    vcat = jnp.concatenate([vp_ref[...], vc_ref[...], vn_ref[...]], axis=0)
    kswp = pltpu.roll(kcat.astype(F32), hd, 1).astype(BF16)
    vswp = pltpu.roll(vcat.astype(F32), hd, 1).astype(BF16)
    lane = lax.broadcasted_iota(jnp.int32, (1, LANES), 1)
    lo = lane < hd
    zero = jnp.zeros_like(kcat)

    def blockdiag(own, swapped, g):
        top = own if g == 0 else swapped
        bot = swapped if g == 0 else own
        return jnp.concatenate([jnp.where(lo, top, zero), jnp.where(lo, zero, bot)], axis=0)

    qi = lax.broadcasted_iota(jnp.int32, (blk, blk), 0)
    ki = lax.broadcasted_iota(jnp.int32, (blk, blk), 1)
    neg = jnp.full((blk, blk), -jnp.inf, F32)
    zer = jnp.zeros((blk, blk), F32)
    bias = jnp.concatenate([
        jnp.where((ki >= qi) & (n > 0), zer, neg),
        zer,
        jnp.where((ki <= qi) & (n < nb - 1), zer, neg)], axis=1)

    for m in range(ATTN_HEADS // 2):
        g = (2 * m) // (ATTN_HEADS // ATTN_KV_HEADS)
        kbd = blockdiag(kcat, kswp, g)
        vbd = blockdiag(vcat, vswp, g)
        qt = q_ref[:, m * LANES:(m + 1) * LANES]
        s = lax.dot_general(qt, kbd, (((1,), (1,)), ((), ())), preferred_element_type=F32)
        ps, ds = [], []
        for e in range(2):
            sk = sink_ref[2 * m + e]
            se = s[:, e * 3 * blk:(e + 1) * 3 * blk] + bias
            mx = jnp.maximum(jnp.max(se, axis=-1, keepdims=True), sk)
            p = jnp.exp(se - mx)
            ds.append(jnp.sum(p, axis=-1, keepdims=True) + jnp.exp(sk - mx))
            ps.append(p.astype(BF16))
        o = jnp.dot(jnp.concatenate(ps, axis=1), vbd, preferred_element_type=F32)
        o = o / jnp.where(lo, ds[0], ds[1])
        o_ref[:, m * LANES:(m + 1) * LANES] = o.astype(BF16)


def _win_attn(sink, q, k, v, batch, seq):
    nb = seq // ATTN_BLOCK
    aw = q.shape[1]
    kvw = k.shape[1]
    cur = lambda b, n: (b * nb + n, 0)
    prev = lambda b, n: (b * nb + jnp.maximum(n - 1, 0), 0)
    nxt = lambda b, n: (b * nb + jnp.minimum(n + 1, nb - 1), 0)
    kv = lambda f: pl.BlockSpec((ATTN_BLOCK, kvw), f)
    return pl.pallas_call(
        _win_attn_kernel,
        out_shape=jax.ShapeDtypeStruct(q.shape, BF16),
        grid=(batch, nb),
        in_specs=[pl.BlockSpec(memory_space=pltpu.SMEM),
                  pl.BlockSpec((ATTN_BLOCK, aw), cur),
                  kv(prev), kv(cur), kv(nxt), kv(prev), kv(cur), kv(nxt)],
        out_specs=pl.BlockSpec((ATTN_BLOCK, aw), cur),
        compiler_params=_cparams(2),
        name="win_attn",
    )(sink, q, k, k, k, v, v, v)


def _cumsum_both(x, axis):
    n = x.shape[axis]
    idx = lax.broadcasted_iota(jnp.int32, x.shape, axis)
    pre, suf = x, x
    s = 1
    while s < n:
        pre = pre + jnp.where(idx >= s, pltpu.roll(pre, s, axis), 0.0)
        suf = suf + jnp.where(idx < n - s, pltpu.roll(suf, n - s, axis), 0.0)
        s *= 2
    return pre, suf


def _unit_tri_inverse(nmat, rc_xor):
    c = nmat.shape[0]
    n0 = jnp.where(rc_xor < TRI_BASE, nmat, 0.0)
    nb16 = n0.astype(BF16)
    p = jnp.dot(nb16, nb16, preferred_element_type=F32)
    y = n0
    n_sq = (TRI_BASE - 1).bit_length() - 1
    for m in range(n_sq):
        pb = p.astype(BF16)
        if m < n_sq - 1:
            r2 = jnp.dot(pb, jnp.concatenate([y.astype(BF16), pb], axis=1), preferred_element_type=F32)
            y = y + p + r2[:, :c]
            p = r2[:, c:]
        else:
            y = y + p + jnp.dot(pb, y.astype(BF16), preferred_element_type=F32)
    blk = TRI_BASE
    while blk < c:
        l_off = jnp.where((rc_xor >= blk) & (rc_xor < 2 * blk), -nmat, 0.0)
        u = l_off + jnp.dot(l_off.astype(BF16), y.astype(BF16), preferred_element_type=F32)
        y = y - u - jnp.dot(y.astype(BF16), u.astype(BF16), preferred_element_type=F32)
        blk *= 2
    return y


def _deltanet_kernel(dq_ref, dk_ref, dv_ref, gate_ref, abc_ref, abr_ref, cw_ref,
                     alog_r_ref, dtb_r_ref, alog_c_ref, dtb_c_ref, gout_ref,
                     o_ref, xp, qn, kn, vn, kt, oacc, st):
    seq = dq_ref.shape[0]
    nc = seq // DN_CHUNK
    c = DN_CHUNK
    hd = DN_HEAD_DIM
    width = DN_HEADS * hd

    xp[0:CONV_HALO, :] = jnp.zeros((CONV_HALO, width), F32)
    xp[CONV_HALO + seq:2 * CONV_HALO + seq, :] = jnp.zeros((CONV_HALO, width), F32)

    def conv_phase(src_ref, w_off, dst, l2, scale):
        def stage(i, carry):
            r = pl.multiple_of(i * c, c)
            xp[pl.ds(CONV_HALO + r, c), :] = src_ref[pl.ds(r, c), :].astype(F32)
            return carry
        lax.fori_loop(0, nc, stage, 0)

        def body(i, carry):
            r = pl.multiple_of(i * c, c)
            for h in range(DN_HEADS):
                xa = xp[pl.ds(r, c + 2 * CONV_HALO), h * hd:(h + 1) * hd]
                y = jnp.zeros((c, hd), F32)
                for j in range(DN_CONV):
                    a = CONV_HALO - DN_CONV // 2 + j
                    wj = cw_ref[j:j + 1, w_off + h * hd:w_off + (h + 1) * hd]
                    y = y + wj * xa[a:a + c, :]
                y = _silu(y)
                if l2:
                    y = y * lax.rsqrt(jnp.sum(y * y, axis=-1, keepdims=True) + EPS)
                    if scale != 1.0:
                        y = y * scale
                dst[i, :, h * hd:(h + 1) * hd] = y.astype(BF16)
                if dst is kn:
                    kt[i, h * hd:(h + 1) * hd, :] = y.T.astype(BF16)
            return carry
        lax.fori_loop(0, nc, body, 0)

    conv_phase(dq_ref, 0, qn, True, hd ** -0.5)
    conv_phase(dk_ref, width, kn, True, 1.0)
    conv_phase(dv_ref, 2 * width, vn, False, 1.0)

    st[...] = jnp.zeros(st.shape, F32)
    row = lax.broadcasted_iota(jnp.int32, (c, c), 0)
    col = lax.broadcasted_iota(jnp.int32, (c, c), 1)
    eye = row == col
    masks = ((row >= col, row > col), (row <= col, row < col))
    nh = DN_HEADS

    def gates(ci):
        r = pl.multiple_of(ci * c, c)
        ab_c = abc_ref[pl.ds(r, c), :]
        ab_r = abr_ref[ci]
        g_c = -jnp.exp(alog_r_ref[...]) * _softplus(ab_c + dtb_r_ref[...])
        g_r = -jnp.exp(alog_c_ref[...]) * _softplus(ab_r + dtb_c_ref[...])
        return dict(
            cum_c=_cumsum_both(g_c, 0), cum_r=_cumsum_both(g_r, 1),
            beta_c=jax.nn.sigmoid(ab_c), beta_r=jax.nn.sigmoid(ab_r),
            gtot=jnp.sum(g_r, axis=1, keepdims=True))

    def chain(ci, h, d, gt):
        j = d * nh + h
        jb = N_DIRS * nh + j
        gc_c = gt["cum_c"][d][:, j:j + 1]
        gc_r = gt["cum_r"][d][j:j + 1, :]
        b_c = gt["beta_c"][:, jb:jb + 1]
        b_r = gt["beta_r"][jb:jb + 1, :]
        gtot = gt["gtot"][j:j + 1, :]
        incl, strict = masks[d]
        hs = slice(h * hd, (h + 1) * hd)
        qc = qn[ci, :, hs]
        kc = kn[ci, :, hs]
        vc = vn[ci, :, hs]
        ktc = kt[ci, hs, :]
        qk2 = jnp.concatenate([qc, kc], axis=0)
        gram = jnp.dot(qk2, ktc, preferred_element_type=F32)
        decay = jnp.exp(jnp.where(incl, gc_c - gc_r, -jnp.inf))
        nmat = jnp.where(strict, -(gram[c:] * b_c) * decay, 0.0)
        tinv = _unit_tri_inverse(nmat, row ^ col) + jnp.where(eye, 1.0, 0.0)
        eg_r = jnp.exp(gc_r)
        m1 = jnp.concatenate([tinv * b_r, -(tinv * (b_r * eg_r))], axis=1).astype(BF16)
        m2 = jnp.concatenate([jnp.where(eye, eg_r, 0.0), gram[:c] * decay], axis=1).astype(BF16)
        kte = (ktc.astype(F32) * jnp.exp(gtot - gc_r)).astype(BF16)
        s_old = st[j]
        pm = jnp.dot(qk2, s_old.astype(BF16), preferred_element_type=F32)
        v_new = jnp.dot(m1, jnp.concatenate([vc, pm[c:].astype(BF16)], axis=0), preferred_element_type=F32)
        vb = v_new.astype(BF16)
        o = jnp.dot(m2, jnp.concatenate([pm[:c].astype(BF16), vb], axis=0), preferred_element_type=F32)
        st[j] = s_old * jnp.exp(gtot) + jnp.dot(kte, vb, preferred_element_type=F32)
        return o

    def emit(ci, h, o, first_visit):
        hs = slice(h * hd, (h + 1) * hd)
        if first_visit:
            oacc[ci, :, hs] = o
        else:
            r = pl.multiple_of(ci * c, c)
            tot = oacc[ci, :, hs] + o
            gate = gate_ref[pl.ds(r, c), hs].astype(F32)
            o_ref[pl.ds(r, c), hs] = (_rms(tot, gout_ref[...]) * _silu(gate)).astype(BF16)

    def step(s, first_visit):
        cf = s
        cb = nc - 1 - s
        gf = gates(cf)
        gb = gates(cb)
        for h in range(nh):
            emit(cf, h, chain(cf, h, 0, gf), first_visit)
            emit(cb, h, chain(cb, h, 1, gb), first_visit)

    def first_half(s, carry):
        step(s, True)
        return carry

    def second_half(s, carry):
        step(s, False)
        return carry

    lax.fori_loop(0, nc // 2, first_half, 0)
    lax.fori_loop(nc // 2, nc, second_half, 0)


def _deltanet(dqkv, gate, abc, abr, conv_w, alog_r, dtb_r, alog_c, dtb_c, gout, batch, seq):
    width = DN_HEADS * DN_HEAD_DIM
    nc = seq // DN_CHUNK
    n_ab = abr.shape[1]
    col = lambda j: pl.BlockSpec((seq, width), lambda b, j=j: (b, j))
    return pl.pallas_call(
        _deltanet_kernel,
        out_shape=jax.ShapeDtypeStruct((batch * seq, width), BF16),
        grid=(batch,),
        in_specs=[col(0), col(1), col(2), col(0),
                  pl.BlockSpec((seq, LANES), lambda b: (b, 0)),
                  pl.BlockSpec((nc, n_ab, DN_CHUNK), lambda b: (b, 0, 0)),
                  _const_spec(conv_w.shape),
                  _const_spec(alog_r.shape), _const_spec(dtb_r.shape),
                  _const_spec(alog_c.shape), _const_spec(dtb_c.shape),
                  _const_spec(gout.shape)],
        out_specs=col(0),
        scratch_shapes=[
            pltpu.VMEM((seq + 2 * CONV_HALO, width), F32),
            pltpu.VMEM((nc, DN_CHUNK, width), BF16),
            pltpu.VMEM((nc, DN_CHUNK, width), BF16),
            pltpu.VMEM((nc, DN_CHUNK, width), BF16),
            pltpu.VMEM((nc, width, DN_CHUNK), BF16),
            pltpu.VMEM((nc, DN_CHUNK, width), F32),
            pltpu.VMEM((N_DIRS * DN_HEADS, DN_HEAD_DIM, DN_HEAD_DIM), F32),
        ],
        compiler_params=_cparams(1),
        name="deltanet",
    )(dqkv, dqkv, dqkv, gate, abc, abr, conv_w, alog_r, dtb_r, alog_c, dtb_c, gout)


def _mem_kv_kernel(mem_ref, gain_ref, w_ref, k_ref, v_ref):
    d = k_ref.shape[1]
    mn = _rms(mem_ref[...], gain_ref[...]).astype(BF16)
    k_ref[...] = jnp.dot(mn, w_ref[:, :d], preferred_element_type=F32).astype(BF16)
    v_ref[...] = jnp.dot(mn, w_ref[:, d:], preferred_element_type=F32).astype(BF16)


def _mem_kv(mem2, gain, w_ckv, mem_len):
    rows, d = mem2.shape
    spec = pl.BlockSpec((mem_len, d), lambda b: (b, 0))
    return pl.pallas_call(
        _mem_kv_kernel,
        out_shape=(jax.ShapeDtypeStruct((rows, d), BF16),) * 2,
        grid=(rows // mem_len,),
        in_specs=[spec, _const_spec((1, d)), _const_spec(w_ckv.shape)],
        out_specs=(spec, spec),
        compiler_params=_cparams(1),
        name="mem_kv",
    )(mem2, gain, w_ckv)


def _out_cross_kernel(x_ref, ao_ref, do_ref, wout_ref, gpost_ref, gpre_ref, wq_ref, km_ref, vm_ref,
                      wo_ref, gcpost_ref, y_ref):
    aw = ao_ref.shape[1]
    mix = (jnp.dot(ao_ref[...], wout_ref[:aw, :], preferred_element_type=F32)
           + jnp.dot(do_ref[...], wout_ref[aw:, :], preferred_element_type=F32))
    x1 = x_ref[...] + _rms(mix, gpost_ref[...])
    hc = _rms(x1, gpre_ref[...]).astype(BF16)
    d = x1.shape[1]
    chd = d // CROSS_HEADS
    q = (jnp.dot(hc, wq_ref[...], preferred_element_type=F32) * (chd ** -0.5)).astype(BF16)
    heads = []
    for h in range(CROSS_HEADS):
        hs = slice(h * chd, (h + 1) * chd)
        s = lax.dot_general(q[:, hs], km_ref[:, hs], (((1,), (1,)), ((), ())), preferred_element_type=F32)
        p = jnp.exp(s - jnp.max(s, axis=-1, keepdims=True))
        o = jnp.dot(p.astype(BF16), vm_ref[:, hs], preferred_element_type=F32)
        heads.append((o / jnp.sum(p, axis=-1, keepdims=True)).astype(BF16))
    cproj = jnp.dot(jnp.concatenate(heads, axis=1), wo_ref[...], preferred_element_type=F32)
    y_ref[...] = x1 + _rms(cproj, gcpost_ref[...])


def _out_cross(x2, attn_o, dn_o, w_out, g_post, g_pre, w_cq, kmem, vmem, w_co, g_cpost, seq, mem_len, tm):
    t, d = x2.shape
    per_batch = seq // tm
    row = lambda w: pl.BlockSpec((tm, w), lambda i: (i, 0))
    memspec = pl.BlockSpec((mem_len, d), lambda i: (i // per_batch, 0))
    vec = _const_spec((1, d))
    return pl.pallas_call(
        _out_cross_kernel,
        out_shape=jax.ShapeDtypeStruct((t, d), F32),
        grid=(t // tm,),
        in_specs=[row(d), row(attn_o.shape[1]), row(dn_o.shape[1]), _const_spec(w_out.shape), vec, vec,
                  _const_spec(w_cq.shape), memspec, memspec, _const_spec(w_co.shape), vec],
        out_specs=row(d),
        compiler_params=_cparams(1),
        name="out_cross",
    )(x2, attn_o, dn_o, w_out, g_post, g_pre, w_cq, kmem, vmem, w_co, g_cpost)


def _ffn_kernel(x_ref, gpre_ref, wgu_ref, wdown_ref, gpost_ref, y_ref):
    x = x_ref[...]
    dff = wdown_ref.shape[0]
    hf = _rms(x, gpre_ref[...]).astype(BF16)
    gu = jnp.dot(hf, wgu_ref[...], preferred_element_type=F32)
    act = (_silu(gu[:, :dff]) * gu[:, dff:]).astype(BF16)
    f = jnp.dot(act, wdown_ref[...], preferred_element_type=F32)
    y_ref[...] = x + _rms(f, gpost_ref[...])


def _ffn(x2, g_pre, w_gu, w_down, g_post, tm):
    t, d = x2.shape
    row = pl.BlockSpec((tm, d), lambda i: (i, 0))
    vec = _const_spec((1, d))
    return pl.pallas_call(
        _ffn_kernel,
        out_shape=jax.ShapeDtypeStruct((t, d), F32),
        grid=(t // tm,),
        in_specs=[row, vec, _const_spec(w_gu.shape), _const_spec(w_down.shape), vec],
        out_specs=row,
        compiler_params=_cparams(1),
        name="ffn",
    )(x2, g_pre, w_gu, w_down, g_post)


def _layer(x, mem, positions, g_mix_pre, w_in, conv_w, a_log, dt_bias, g_dn_out, attn_sink, w_out,
           g_mix_post, g_cross_pre, g_mem, w_cq, w_ckv, w_co, g_cross_post, g_ffn_pre, w_gate_up,
           w_down, g_ffn_post):
    batch, seq, d = x.shape
    mem_len = mem.shape[1]
    t = batch * seq
    x2 = x.reshape(t, d)
    vec = lambda g: g.reshape(1, -1).astype(F32)

    half = ATTN_HEAD_DIM // 2
    inv_freq = ROPE_THETA ** (-jnp.arange(half, dtype=F32) / half)
    per_row = LANES // half
    pos_exp = jnp.repeat(positions.reshape(t, 1), half, axis=1).reshape(t // per_row, LANES)
    cos_d, sin_d = _rope_tables(pos_exp, jnp.tile(inv_freq, per_row).reshape(1, LANES))
    cos_t = jnp.tile(cos_d.reshape(t, half), (1, per_row))
    sin_t = jnp.tile(sin_d.reshape(t, half), (1, per_row))

    n_main = w_in.shape[1] - 2 * N_DIRS * DN_HEADS
    n_ab = 2 * N_DIRS * DN_HEADS
    w_main = w_in[:, :n_main].astype(BF16)
    w_ab = jnp.pad(w_in[:, n_main:], ((0, 0), (0, LANES - n_ab))).astype(BF16)
    w_abt = w_in[:, n_main:].T.astype(BF16)
    q, k, v, dqkv, dgate, abc, abr = _in_proj(x2, vec(g_mix_pre), w_main, w_ab, w_abt, cos_t, sin_t, tm=512)

    attn_o = _win_attn(attn_sink.astype(F32), q, k, v, batch, seq)

    pad_r = lambda p: jnp.pad(p.reshape(1, -1).astype(F32), ((0, 0), (0, LANES - N_DIRS * DN_HEADS)))
    pad_c = lambda p: jnp.pad(p.reshape(-1, 1).astype(F32), ((0, n_ab - N_DIRS * DN_HEADS), (0, 0)))
    dn_o = _deltanet(dqkv, dgate, abc, abr, conv_w.astype(F32), pad_r(a_log), pad_r(dt_bias),
                     pad_c(a_log), pad_c(dt_bias), vec(g_dn_out), batch, seq)

    kmem, vmem = _mem_kv(mem.reshape(batch * mem_len, d), vec(g_mem), w_ckv.astype(BF16), mem_len)
    x3 = _out_cross(x2, attn_o, dn_o, w_out.astype(BF16), vec(g_mix_post), vec(g_cross_pre),
                    w_cq.astype(BF16), kmem, vmem, w_co.astype(BF16), vec(g_cross_post), seq, mem_len, tm=512)
    y = _ffn(x3, vec(g_ffn_pre), w_gate_up.astype(BF16), w_down.astype(BF16), vec(g_ffn_post), tm=256)
    return y.reshape(batch, seq, d)


def kernel(x, mem, positions, g_mix_pre, w_in, conv_w, a_log, dt_bias, g_dn_out, attn_sink, w_out,
           g_mix_post, g_cross_pre, g_mem, w_cq, w_ckv, w_co, g_cross_post, g_ffn_pre, w_gate_up,
           w_down, g_ffn_post):
    depth = w_in.shape[0]
    for l in range(depth):
        x = _layer(x, mem, positions, g_mix_pre[l], w_in[l], conv_w[l], a_log[l], dt_bias[l], g_dn_out[l],
                   attn_sink[l], w_out[l], g_mix_post[l], g_cross_pre[l], g_mem[l], w_cq[l], w_ckv[l],
                   w_co[l], g_cross_post[l], g_ffn_pre[l], w_gate_up[l], w_down[l], g_ffn_post[l])
    return x
```

```python
import functools

import jax
import jax.numpy as jnp
from jax import lax
from jax.experimental import pallas as pl
from jax.experimental.pallas import tpu as pltpu

F32 = jnp.float32
BF16 = jnp.bfloat16

EPS = 1e-6
ROPE_THETA = 10000.0
ATTN_HEADS = 8
ATTN_KV_HEADS = 2
ATTN_HEAD_DIM = 64
ATTN_BLOCK = 128
DN_HEADS = 4
DN_HEAD_DIM = 128
DN_CONV = 5
N_DIRS = 2
CROSS_HEADS = 4

LANES = 128
DN_CHUNK = 128
TRI_BASE = 16
CONV_HALO = 8
VMEM_LIMIT = 56 * 1024 * 1024


def _cparams(n_grid_dims):
    return pltpu.CompilerParams(
        dimension_semantics=("arbitrary",) * n_grid_dims,
        vmem_limit_bytes=VMEM_LIMIT,
    )


def _rms(x, gain):
    return x * lax.rsqrt(jnp.mean(x * x, axis=-1, keepdims=True) + EPS) * gain


def _silu(x):
    return x * jax.nn.sigmoid(x)


def _softplus(x):
    return jnp.maximum(x, 0.0) + jnp.log(1.0 + jnp.exp(-jnp.abs(x)))


def _const_spec(shape):
    nd = len(shape)
    return pl.BlockSpec(shape, lambda *_: (0,) * nd)


def _rope_table_kernel(pos_ref, invf_ref, cos_ref, sin_ref):
    ang = pos_ref[...].astype(F32) * invf_ref[...]
    cos_ref[...] = jnp.cos(ang)
    sin_ref[...] = jnp.sin(ang)


def _rope_tables(pos_exp, invf_row):
    rows = pos_exp.shape[0]
    blk = min(rows, 512)
    spec = pl.BlockSpec((blk, LANES), lambda i: (i, 0))
    return pl.pallas_call(
        _rope_table_kernel,
        out_shape=(jax.ShapeDtypeStruct((rows, LANES), F32),) * 2,
        grid=(rows // blk,),
        in_specs=[spec, _const_spec((1, LANES))],
        out_specs=(spec, spec),
        compiler_params=_cparams(1),
        name="rope_table",
    )(pos_exp, invf_row)


def _in_proj_kernel(x_ref, gain_ref, w_ref, wab_ref, wabt_ref, cos_ref, sin_ref,
                    q_ref, k_ref, v_ref, dqkv_ref, gate_ref, abc_ref, abr_ref):
    h = _rms(x_ref[...], gain_ref[...]).astype(BF16)
    cos = cos_ref[...]
    sin = sin_ref[...]
    lane = lax.broadcasted_iota(jnp.int32, (1, LANES), 1)
    first_half = (lane % ATTN_HEAD_DIM) < (ATTN_HEAD_DIM // 2)

    def rope(t):
        rot = jnp.where(first_half, -pltpu.roll(t, LANES - 32, 1), pltpu.roll(t, 32, 1))
        return t * cos + rot * sin

    def proj(lo, hi):
        return jnp.dot(h, w_ref[:, lo:hi], preferred_element_type=F32)

    aw = ATTN_HEADS * ATTN_HEAD_DIM
    kvw = ATTN_KV_HEADS * ATTN_HEAD_DIM
    for m in range(aw // LANES):
        q = rope(proj(m * LANES, (m + 1) * LANES)) * (ATTN_HEAD_DIM ** -0.5)
        q_ref[:, m * LANES:(m + 1) * LANES] = q.astype(BF16)
    k_ref[...] = rope(proj(aw, aw + kvw)).astype(BF16)
    v_ref[...] = proj(aw + kvw, aw + 2 * kvw).astype(BF16)
    off = aw + 2 * kvw
    dn = dqkv_ref.shape[1]
    dqkv_ref[...] = proj(off, off + dn).astype(BF16)
    gate_ref[...] = proj(off + dn, off + dn + gate_ref.shape[1]).astype(BF16)
    abc_ref[...] = jnp.dot(h, wab_ref[...], preferred_element_type=F32)
    abr = lax.dot_general(wabt_ref[...], h, (((1,), (1,)), ((), ())), preferred_element_type=F32)
    for j in range(abr_ref.shape[0]):
        abr_ref[j] = abr[:, j * DN_CHUNK:(j + 1) * DN_CHUNK]


def _in_proj(x2, gain, w_main, w_ab, w_abt, cos_t, sin_t, tm):
    t, d = x2.shape
    aw = ATTN_HEADS * ATTN_HEAD_DIM
    kvw = ATTN_KV_HEADS * ATTN_HEAD_DIM
    dnw = DN_HEADS * DN_HEAD_DIM
    n_ab = w_abt.shape[0]
    row = lambda w: pl.BlockSpec((tm, w), lambda i: (i, 0))
    out_shape = (
        jax.ShapeDtypeStruct((t, aw), BF16),
        jax.ShapeDtypeStruct((t, kvw), BF16),
        jax.ShapeDtypeStruct((t, kvw), BF16),
        jax.ShapeDtypeStruct((t, 3 * dnw), BF16),
        jax.ShapeDtypeStruct((t, dnw), BF16),
        jax.ShapeDtypeStruct((t, LANES), F32),
        jax.ShapeDtypeStruct((t // DN_CHUNK, n_ab, DN_CHUNK), F32),
    )
    out_specs = (
        row(aw), row(kvw), row(kvw), row(3 * dnw), row(dnw), row(LANES),
        pl.BlockSpec((tm // DN_CHUNK, n_ab, DN_CHUNK), lambda i: (i, 0, 0)),
    )
    return pl.pallas_call(
        _in_proj_kernel,
        out_shape=out_shape,
        grid=(t // tm,),
        in_specs=[row(d), _const_spec((1, d)), _const_spec(w_main.shape), _const_spec(w_ab.shape),
                  _const_spec(w_abt.shape), row(LANES), row(LANES)],
        out_specs=out_specs,
        compiler_params=_cparams(1),
        name="in_proj",
    )(x2, gain, w_main, w_ab, w_abt, cos_t, sin_t)


def _win_attn_kernel(sink_ref, q_ref, kp_ref, kc_ref, kn_ref, vp_ref, vc_ref, vn_ref, o_ref):
    n = pl.program_id(1)
    nb = pl.num_programs(1)
    blk = ATTN_BLOCK
    hd = ATTN_HEAD_DIM
    kcat = jnp.concatenate([kp_ref[...], kc_ref[...], kn_ref[...]], axis=0)
    vcat = jnp.concatenate([vp_ref[...], vc_ref[...], vn_ref[...]], axis=0)
    kswp = pltpu.roll(kcat.astype(F32), hd, 1).astype(BF16)
    vswp = pltpu.roll(vcat.astype(F32), hd, 1).astype(BF16)
    lane = lax.broadcasted_iota(jnp.int32, (1, LANES), 1)
    lo = lane < hd
    zero = jnp.zeros_like(kcat)

    def blockdiag(own, swapped, g):
        top = own if g == 0 else swapped
        bot = swapped if g == 0 else own
        return jnp.concatenate([jnp.where(lo, top, zero), jnp.where(lo, zero, bot)], axis=0)

    qi = lax.broadcasted_iota(jnp.int32, (blk, blk), 0)
    ki = lax.broadcasted_iota(jnp.int32, (blk, blk), 1)
    neg = jnp.full((blk, blk), -jnp.inf, F32)
    zer = jnp.zeros((blk, blk), F32)
    bias = jnp.concatenate([
        jnp.where((ki >= qi) & (n > 0), zer, neg),
        zer,
        jnp.where((ki <= qi) & (n < nb - 1), zer, neg)], axis=1)

    for m in range(ATTN_HEADS // 2):
        g = (2 * m) // (ATTN_HEADS // ATTN_KV_HEADS)
        kbd = blockdiag(kcat, kswp, g)
        vbd = blockdiag(vcat, vswp, g)
        qt = q_ref[:, m * LANES:(m + 1) * LANES]
        s = lax.dot_general(qt, kbd, (((1,), (1,)), ((), ())), preferred_element_type=F32)
        ps, ds = [], []
        for e in range(2):
            sk = sink_ref[2 * m + e]
            se = s[:, e * 3 * blk:(e + 1) * 3 * blk] + bias
            mx = jnp.maximum(jnp.max(se, axis=-1, keepdims=True), sk)
            p = jnp.exp(se - mx)
            ds.append(jnp.sum(p, axis=-1, keepdims=True) + jnp.exp(sk - mx))
            ps.append(p.astype(BF16))
        o = jnp.dot(jnp.concatenate(ps, axis=1), vbd, preferred_element_type=F32)
        o = o / jnp.where(lo, ds[0], ds[1])
        o_ref[:, m * LANES:(m + 1) * LANES] = o.astype(BF16)


def _win_attn(sink, q, k, v, batch, seq):
    nb = seq // ATTN_BLOCK
    aw = q.shape[1]
    kvw = k.shape[1]
    cur = lambda b, n: (b * nb + n, 0)
    prev = lambda b, n: (b * nb + jnp.maximum(n - 1, 0), 0)
    nxt = lambda b, n: (b * nb + jnp.minimum(n + 1, nb - 1), 0)
    kv = lambda f: pl.BlockSpec((ATTN_BLOCK, kvw), f)
    return pl.pallas_call(
        _win_attn_kernel,
        out_shape=jax.ShapeDtypeStruct(q.shape, BF16),
        grid=(batch, nb),
        in_specs=[pl.BlockSpec(memory_space=pltpu.SMEM),
                  pl.BlockSpec((ATTN_BLOCK, aw), cur),
                  kv(prev), kv(cur), kv(nxt), kv(prev), kv(cur), kv(nxt)],
        out_specs=pl.BlockSpec((ATTN_BLOCK, aw), cur),
        compiler_params=_cparams(2),
        name="win_attn",
    )(sink, q, k, k, k, v, v, v)


def _cumsum_both(x, axis):
    n = x.shape[axis]
    idx = lax.broadcasted_iota(jnp.int32, x.shape, axis)
    pre, suf = x, x
    s = 1
    while s < n:
        pre = pre + jnp.where(idx >= s, pltpu.roll(pre, s, axis), 0.0)
        suf = suf + jnp.where(idx < n - s, pltpu.roll(suf, n - s, axis), 0.0)
        s *= 2
    return pre, suf


def _lockstep(gens):
    results = [None] * len(gens)
    active = list(enumerate(gens))
    while active:
        still = []
        for i, g in active:
            try:
                next(g)
                still.append((i, g))
            except StopIteration as stop:
                results[i] = stop.value
        active = still
    return results


def _unit_tri_inverse(nmat, rc_xor):
    c = nmat.shape[0]
    n0 = jnp.where(rc_xor < TRI_BASE, nmat, 0.0)
    nb16 = n0.astype(BF16)
    p = jnp.dot(nb16, nb16, preferred_element_type=F32)
    yield
    y = n0
    n_sq = (TRI_BASE - 1).bit_length() - 1
    for m in range(n_sq):
        pb = p.astype(BF16)
        if m < n_sq - 1:
            r2 = jnp.dot(pb, jnp.concatenate([y.astype(BF16), pb], axis=1), preferred_element_type=F32)
            yield
            y = y + p + r2[:, :c]
            p = r2[:, c:]
        else:
            py = jnp.dot(pb, y.astype(BF16), preferred_element_type=F32)
            yield
            y = y + p + py
    blk = TRI_BASE
    while blk < c:
        l_off = jnp.where((rc_xor >= blk) & (rc_xor < 2 * blk), -nmat, 0.0)
        ly = jnp.dot(l_off.astype(BF16), y.astype(BF16), preferred_element_type=F32)
        yield
        u = l_off + ly
        yu = jnp.dot(y.astype(BF16), u.astype(BF16), preferred_element_type=F32)
        yield
        y = y - u - yu
        blk *= 2
    return y


def _deltanet_kernel(dq_ref, dk_ref, dv_ref, gate_ref, abc_ref, abr_ref, cw_ref,
                     alog_r_ref, dtb_r_ref, alog_c_ref, dtb_c_ref, gout_ref,
                     o_ref, xp, qn, kn, vn, kt, oacc, st):
    seq = dq_ref.shape[0]
    nc = seq // DN_CHUNK
    c = DN_CHUNK
    hd = DN_HEAD_DIM
    width = DN_HEADS * hd

    xp[0:CONV_HALO, :] = jnp.zeros((CONV_HALO, width), F32)
    xp[CONV_HALO + seq:2 * CONV_HALO + seq, :] = jnp.zeros((CONV_HALO, width), F32)

    def conv_phase(src_ref, w_off, dst, l2, scale):
        def stage(i, carry):
            r = pl.multiple_of(i * c, c)
            xp[pl.ds(CONV_HALO + r, c), :] = src_ref[pl.ds(r, c), :].astype(F32)
            return carry
        lax.fori_loop(0, nc, stage, 0)

        def body(i, carry):
            r = pl.multiple_of(i * c, c)
            for h in range(DN_HEADS):
                xa = xp[pl.ds(r, c + 2 * CONV_HALO), h * hd:(h + 1) * hd]
                y = jnp.zeros((c, hd), F32)
                for j in range(DN_CONV):
                    a = CONV_HALO - DN_CONV // 2 + j
                    wj = cw_ref[j:j + 1, w_off + h * hd:w_off + (h + 1) * hd]
                    y = y + wj * xa[a:a + c, :]
                y = _silu(y)
                if l2:
                    y = y * lax.rsqrt(jnp.sum(y * y, axis=-1, keepdims=True) + EPS)
                    if scale != 1.0:
                        y = y * scale
                dst[i, :, h * hd:(h + 1) * hd] = y.astype(BF16)
                if dst is kn:
                    kt[i, h * hd:(h + 1) * hd, :] = y.T.astype(BF16)
            return carry
        lax.fori_loop(0, nc, body, 0)

    conv_phase(dq_ref, 0, qn, True, hd ** -0.5)
    conv_phase(dk_ref, width, kn, True, 1.0)
    conv_phase(dv_ref, 2 * width, vn, False, 1.0)

    st[...] = jnp.zeros(st.shape, F32)
    row = lax.broadcasted_iota(jnp.int32, (c, c), 0)
    col = lax.broadcasted_iota(jnp.int32, (c, c), 1)
    eye = row == col
    masks = ((row >= col, row > col), (row <= col, row < col))
    nh = DN_HEADS

    def gates(ci):
        r = pl.multiple_of(ci * c, c)
        ab_c = abc_ref[pl.ds(r, c), :]
        ab_r = abr_ref[ci]
        g_c = -jnp.exp(alog_r_ref[...]) * _softplus(ab_c + dtb_r_ref[...])
        g_r = -jnp.exp(alog_c_ref[...]) * _softplus(ab_r + dtb_c_ref[...])
        return dict(
            cum_c=_cumsum_both(g_c, 0), cum_r=_cumsum_both(g_r, 1),
            beta_c=jax.nn.sigmoid(ab_c), beta_r=jax.nn.sigmoid(ab_r),
            gtot=jnp.sum(g_r, axis=1, keepdims=True))

    def chain(ci, h, d, gt, first_visit):
        j = d * nh + h
        jb = N_DIRS * nh + j
        gc_c = gt["cum_c"][d][:, j:j + 1]
        gc_r = gt["cum_r"][d][j:j + 1, :]
        b_c = gt["beta_c"][:, jb:jb + 1]
        b_r = gt["beta_r"][jb:jb + 1, :]
        gtot = gt["gtot"][j:j + 1, :]
        incl, strict = masks[d]
        hs = slice(h * hd, (h + 1) * hd)
        ktc = kt[ci, hs, :]
        qk2 = jnp.concatenate([qn[ci, :, hs], kn[ci, :, hs]], axis=0)
        g1 = jnp.dot(qk2, jnp.concatenate([ktc, st[j].astype(BF16)], axis=1), preferred_element_type=F32)
        yield
        decay = jnp.exp(jnp.where(incl, gc_c - gc_r, -jnp.inf))
        nmat = jnp.where(strict, -(g1[c:, :c] * b_c) * decay, 0.0)
        y = yield from _unit_tri_inverse(nmat, row ^ col)
        tinv = y + jnp.where(eye, 1.0, 0.0)
        eg_r = jnp.exp(gc_r)
        m1 = jnp.concatenate([tinv * b_r, -(tinv * (b_r * eg_r))], axis=1).astype(BF16)
        v_new = jnp.dot(m1, jnp.concatenate([vn[ci, :, hs], g1[c:, c:].astype(BF16)], axis=0),
                        preferred_element_type=F32)
        yield
        vb = v_new.astype(BF16)
        m2 = jnp.concatenate([jnp.where(eye, eg_r, 0.0), g1[:c, :c] * decay], axis=1).astype(BF16)
        o = jnp.dot(m2, jnp.concatenate([g1[:c, c:].astype(BF16), vb], axis=0), preferred_element_type=F32)
        kte = (ktc.astype(F32) * jnp.exp(gtot - gc_r)).astype(BF16)
        ds = jnp.dot(kte, vb, preferred_element_type=F32)
        yield
        st[j] = st[j] * jnp.exp(gtot) + ds
        if first_visit:
            oacc[ci, :, hs] = o
        else:
            r = pl.multiple_of(ci * c, c)
            tot = oacc[ci, :, hs] + o
            gate = gate_ref[pl.ds(r, c), hs].astype(F32)
            o_ref[pl.ds(r, c), hs] = (_rms(tot, gout_ref[...]) * _silu(gate)).astype(BF16)

    def step(s, first_visit):
        cf = s
        cb = nc - 1 - s
        gf = gates(cf)
        gb = gates(cb)
        _lockstep([chain(cf, h, 0, gf, first_visit) for h in range(nh)]
                  + [chain(cb, h, 1, gb, first_visit) for h in range(nh)])

    def first_half(s, carry):
        step(s, True)
        return carry

    def second_half(s, carry):
        step(s, False)
        return carry

    lax.fori_loop(0, nc // 2, first_half, 0)
    lax.fori_loop(nc // 2, nc, second_half, 0)


def _deltanet(dqkv, gate, abc, abr, conv_w, alog_r, dtb_r, alog_c, dtb_c, gout, batch, seq):
    width = DN_HEADS * DN_HEAD_DIM
    nc = seq // DN_CHUNK
    n_ab = abr.shape[1]
    col = lambda j: pl.BlockSpec((seq, width), lambda b, j=j: (b, j))
    return pl.pallas_call(
        _deltanet_kernel,
        out_shape=jax.ShapeDtypeStruct((batch * seq, width), BF16),
        grid=(batch,),
        in_specs=[col(0), col(1), col(2), col(0),
                  pl.BlockSpec((seq, LANES), lambda b: (b, 0)),
                  pl.BlockSpec((nc, n_ab, DN_CHUNK), lambda b: (b, 0, 0)),
                  _const_spec(conv_w.shape),
                  _const_spec(alog_r.shape), _const_spec(dtb_r.shape),
                  _const_spec(alog_c.shape), _const_spec(dtb_c.shape),
                  _const_spec(gout.shape)],
        out_specs=col(0),
        scratch_shapes=[
            pltpu.VMEM((seq + 2 * CONV_HALO, width), F32),
            pltpu.VMEM((nc, DN_CHUNK, width), BF16),
            pltpu.VMEM((nc, DN_CHUNK, width), BF16),
            pltpu.VMEM((nc, DN_CHUNK, width), BF16),
            pltpu.VMEM((nc, width, DN_CHUNK), BF16),
            pltpu.VMEM((nc, DN_CHUNK, width), F32),
            pltpu.VMEM((N_DIRS * DN_HEADS, DN_HEAD_DIM, DN_HEAD_DIM), F32),
        ],
        compiler_params=_cparams(1),
        name="deltanet",
    )(dqkv, dqkv, dqkv, gate, abc, abr, conv_w, alog_r, dtb_r, alog_c, dtb_c, gout)


def _mem_kv_kernel(mem_ref, gain_ref, w_ref, k_ref, v_ref):
    d = k_ref.shape[1]
    mn = _rms(mem_ref[...], gain_ref[...]).astype(BF16)
    k_ref[...] = jnp.dot(mn, w_ref[:, :d], preferred_element_type=F32).astype(BF16)
    v_ref[...] = jnp.dot(mn, w_ref[:, d:], preferred_element_type=F32).astype(BF16)


def _mem_kv(mem2, gain, w_ckv, mem_len):
    rows, d = mem2.shape
    spec = pl.BlockSpec((mem_len, d), lambda b: (b, 0))
    return pl.pallas_call(
        _mem_kv_kernel,
        out_shape=(jax.ShapeDtypeStruct((rows, d), BF16),) * 2,
        grid=(rows // mem_len,),
        in_specs=[spec, _const_spec((1, d)), _const_spec(w_ckv.shape)],
        out_specs=(spec, spec),
        compiler_params=_cparams(1),
        name="mem_kv",
    )(mem2, gain, w_ckv)


def _out_cross_kernel(x_ref, ao_ref, do_ref, wout_ref, gpost_ref, gpre_ref, wq_ref, km_ref, vm_ref,
                      wo_ref, gcpost_ref, y_ref):
    aw = ao_ref.shape[1]
    mix = (jnp.dot(ao_ref[...], wout_ref[:aw, :], preferred_element_type=F32)
           + jnp.dot(do_ref[...], wout_ref[aw:, :], preferred_element_type=F32))
    x1 = x_ref[...] + _rms(mix, gpost_ref[...])
    hc = _rms(x1, gpre_ref[...]).astype(BF16)
    d = x1.shape[1]
    chd = d // CROSS_HEADS
    q = (jnp.dot(hc, wq_ref[...], preferred_element_type=F32) * (chd ** -0.5)).astype(BF16)
    heads = []
    for h in range(CROSS_HEADS):
        hs = slice(h * chd, (h + 1) * chd)
        s = lax.dot_general(q[:, hs], km_ref[:, hs], (((1,), (1,)), ((), ())), preferred_element_type=F32)
        p = jnp.exp(s - jnp.max(s, axis=-1, keepdims=True))
        o = jnp.dot(p.astype(BF16), vm_ref[:, hs], preferred_element_type=F32)
        heads.append((o / jnp.sum(p, axis=-1, keepdims=True)).astype(BF16))
    cproj = jnp.dot(jnp.concatenate(heads, axis=1), wo_ref[...], preferred_element_type=F32)
    y_ref[...] = x1 + _rms(cproj, gcpost_ref[...])


def _out_cross(x2, attn_o, dn_o, w_out, g_post, g_pre, w_cq, kmem, vmem, w_co, g_cpost, seq, mem_len, tm):
    t, d = x2.shape
    per_batch = seq // tm
    row = lambda w: pl.BlockSpec((tm, w), lambda i: (i, 0))
    memspec = pl.BlockSpec((mem_len, d), lambda i: (i // per_batch, 0))
    vec = _const_spec((1, d))
    return pl.pallas_call(
        _out_cross_kernel,
        out_shape=jax.ShapeDtypeStruct((t, d), F32),
        grid=(t // tm,),
        in_specs=[row(d), row(attn_o.shape[1]), row(dn_o.shape[1]), _const_spec(w_out.shape), vec, vec,
                  _const_spec(w_cq.shape), memspec, memspec, _const_spec(w_co.shape), vec],
        out_specs=row(d),
        compiler_params=_cparams(1),
        name="out_cross",
    )(x2, attn_o, dn_o, w_out, g_post, g_pre, w_cq, kmem, vmem, w_co, g_cpost)


def _ffn_kernel(x_ref, gpre_ref, wgu_ref, wdown_ref, gpost_ref, y_ref):
    x = x_ref[...]
    dff = wdown_ref.shape[0]
    hf = _rms(x, gpre_ref[...]).astype(BF16)
    gu = jnp.dot(hf, wgu_ref[...], preferred_element_type=F32)
    act = (_silu(gu[:, :dff]) * gu[:, dff:]).astype(BF16)
    f = jnp.dot(act, wdown_ref[...], preferred_element_type=F32)
    y_ref[...] = x + _rms(f, gpost_ref[...])


def _ffn(x2, g_pre, w_gu, w_down, g_post, tm):
    t, d = x2.shape
    row = pl.BlockSpec((tm, d), lambda i: (i, 0))
    vec = _const_spec((1, d))
    return pl.pallas_call(
        _ffn_kernel,
        out_shape=jax.ShapeDtypeStruct((t, d), F32),
        grid=(t // tm,),
        in_specs=[row, vec, _const_spec(w_gu.shape), _const_spec(w_down.shape), vec],
        out_specs=row,
        compiler_params=_cparams(1),
        name="ffn",
    )(x2, g_pre, w_gu, w_down, g_post)


def _layer(x, mem, positions, g_mix_pre, w_in, conv_w, a_log, dt_bias, g_dn_out, attn_sink, w_out,
           g_mix_post, g_cross_pre, g_mem, w_cq, w_ckv, w_co, g_cross_post, g_ffn_pre, w_gate_up,
           w_down, g_ffn_post):
    batch, seq, d = x.shape
    mem_len = mem.shape[1]
    t = batch * seq
    x2 = x.reshape(t, d)
    vec = lambda g: g.reshape(1, -1).astype(F32)

    half = ATTN_HEAD_DIM // 2
    inv_freq = ROPE_THETA ** (-jnp.arange(half, dtype=F32) / half)
    per_row = LANES // half
    pos_exp = jnp.repeat(positions.reshape(t, 1), half, axis=1).reshape(t // per_row, LANES)
    cos_d, sin_d = _rope_tables(pos_exp, jnp.tile(inv_freq, per_row).reshape(1, LANES))
    cos_t = jnp.tile(cos_d.reshape(t, half), (1, per_row))
    sin_t = jnp.tile(sin_d.reshape(t, half), (1, per_row))

    n_main = w_in.shape[1] - 2 * N_DIRS * DN_HEADS
    n_ab = 2 * N_DIRS * DN_HEADS
    w_main = w_in[:, :n_main].astype(BF16)
    w_ab = jnp.pad(w_in[:, n_main:], ((0, 0), (0, LANES - n_ab))).astype(BF16)
    w_abt = w_in[:, n_main:].T.astype(BF16)
    q, k, v, dqkv, dgate, abc, abr = _in_proj(x2, vec(g_mix_pre), w_main, w_ab, w_abt, cos_t, sin_t, tm=512)

    attn_o = _win_attn(attn_sink.astype(F32), q, k, v, batch, seq)

    pad_r = lambda p: jnp.pad(p.reshape(1, -1).astype(F32), ((0, 0), (0, LANES - N_DIRS * DN_HEADS)))
    pad_c = lambda p: jnp.pad(p.reshape(-1, 1).astype(F32), ((0, n_ab - N_DIRS * DN_HEADS), (0, 0)))
    dn_o = _deltanet(dqkv, dgate, abc, abr, conv_w.astype(F32), pad_r(a_log), pad_r(dt_bias),
                     pad_c(a_log), pad_c(dt_bias), vec(g_dn_out), batch, seq)

    kmem, vmem = _mem_kv(mem.reshape(batch * mem_len, d), vec(g_mem), w_ckv.astype(BF16), mem_len)
    x3 = _out_cross(x2, attn_o, dn_o, w_out.astype(BF16), vec(g_mix_post), vec(g_cross_pre),
                    w_cq.astype(BF16), kmem, vmem, w_co.astype(BF16), vec(g_cross_post), seq, mem_len, tm=512)
    y = _ffn(x3, vec(g_ffn_pre), w_gate_up.astype(BF16), w_down.astype(BF16), vec(g_ffn_post), tm=256)
    return y.reshape(batch, seq, d)


def kernel(x, mem, positions, g_mix_pre, w_in, conv_w, a_log, dt_bias, g_dn_out, attn_sink, w_out,
           g_mix_post, g_cross_pre, g_mem, w_cq, w_ckv, w_co, g_cross_post, g_ffn_pre, w_gate_up,
           w_down, g_ffn_post):
    depth = w_in.shape[0]
    for l in range(depth):
        x = _layer(x, mem, positions, g_mix_pre[l], w_in[l], conv_w[l], a_log[l], dt_bias[l], g_dn_out[l],
                   attn_sink[l], w_out[l], g_mix_post[l], g_cross_pre[l], g_mem[l], w_cq[l], w_ckv[l],
                   w_co[l], g_cross_post[l], g_ffn_pre[l], w_gate_up[l], w_down[l], g_ffn_post[l])
    return x
```

```python
import functools

import jax
import jax.numpy as jnp
from jax import lax
from jax.experimental import pallas as pl
from jax.experimental.pallas import tpu as pltpu

F32 = jnp.float32
BF16 = jnp.bfloat16

EPS = 1e-6
ROPE_THETA = 10000.0
ATTN_HEADS = 8
ATTN_KV_HEADS = 2
ATTN_HEAD_DIM = 64
ATTN_BLOCK = 128
DN_HEADS = 4
DN_HEAD_DIM = 128
DN_CONV = 5
N_DIRS = 2
CROSS_HEADS = 4

LANES = 128
DN_CHUNK = 128
TRI_BASE = 16
CONV_HALO = 8
VMEM_LIMIT = 56 * 1024 * 1024


def _cparams(n_grid_dims):
    return pltpu.CompilerParams(
        dimension_semantics=("arbitrary",) * n_grid_dims,
        vmem_limit_bytes=VMEM_LIMIT,
    )


def _rms(x, gain):
    return x * lax.rsqrt(jnp.mean(x * x, axis=-1, keepdims=True) + EPS) * gain


def _silu(x):
    return x * jax.nn.sigmoid(x)


def _softplus(x):
    return jnp.maximum(x, 0.0) + jnp.log(1.0 + jnp.exp(-jnp.abs(x)))


def _const_spec(shape):
    nd = len(shape)
    return pl.BlockSpec(shape, lambda *_: (0,) * nd)


def _in_proj_kernel(x_ref, pos_ref, invf_ref, gain_ref, w_ref, wabt_ref,
                    q_ref, k_ref, v_ref, dqkv_ref, gate_ref, abc_ref, abr_ref):
    h = _rms(x_ref[...], gain_ref[...]).astype(BF16)
    half = ATTN_HEAD_DIM // 2
    ang = pos_ref[...].astype(F32) * invf_ref[...]
    cos = jnp.cos(ang)
    sin = jnp.sin(ang)
    lane = lax.broadcasted_iota(jnp.int32, (1, LANES), 1)
    first_half = (lane % ATTN_HEAD_DIM) < half

    def rope(t):
        rot = jnp.where(first_half, -pltpu.roll(t, LANES - half, 1), pltpu.roll(t, half, 1))
        return t * cos + rot * sin

    def proj(lo, hi):
        return jnp.dot(h, w_ref[:, lo:hi], preferred_element_type=F32)

    aw = ATTN_HEADS * ATTN_HEAD_DIM
    kvw = ATTN_KV_HEADS * ATTN_HEAD_DIM
    q = proj(0, aw)
    for m in range(aw // LANES):
        cols = slice(m * LANES, (m + 1) * LANES)
        q_ref[:, cols] = (rope(q[:, cols]) * (ATTN_HEAD_DIM ** -0.5)).astype(BF16)
    kv = proj(aw, aw + 2 * kvw)
    k_ref[...] = rope(kv[:, :kvw]).astype(BF16)
    v_ref[...] = kv[:, kvw:].astype(BF16)
    off = aw + 2 * kvw
    dn = dqkv_ref.shape[1]
    dqkv_ref[...] = proj(off, off + dn).astype(BF16)
    gw = gate_ref.shape[1]
    gab = proj(off + dn, off + dn + gw + LANES)
    gate_ref[...] = gab[:, :gw].astype(BF16)
    abc_ref[...] = gab[:, gw:]
    abr = lax.dot_general(wabt_ref[...], h, (((1,), (1,)), ((), ())), preferred_element_type=F32)
    for j in range(abr_ref.shape[0]):
        abr_ref[j] = abr[:, j * DN_CHUNK:(j + 1) * DN_CHUNK]


def _in_proj(x2, pos_col, invf_row, gain, w_main, w_abt, tm):
    t, d = x2.shape
    aw = ATTN_HEADS * ATTN_HEAD_DIM
    kvw = ATTN_KV_HEADS * ATTN_HEAD_DIM
    dnw = DN_HEADS * DN_HEAD_DIM
    n_ab = w_abt.shape[0]
    row = lambda w: pl.BlockSpec((tm, w), lambda i: (i, 0))
    out_shape = (
        jax.ShapeDtypeStruct((t, aw), BF16),
        jax.ShapeDtypeStruct((t, kvw), BF16),
        jax.ShapeDtypeStruct((t, kvw), BF16),
        jax.ShapeDtypeStruct((t, 3 * dnw), BF16),
        jax.ShapeDtypeStruct((t, dnw), BF16),
        jax.ShapeDtypeStruct((t, LANES), F32),
        jax.ShapeDtypeStruct((t // DN_CHUNK, n_ab, DN_CHUNK), F32),
    )
    out_specs = (
        row(aw), row(kvw), row(kvw), row(3 * dnw), row(dnw), row(LANES),
        pl.BlockSpec((tm // DN_CHUNK, n_ab, DN_CHUNK), lambda i: (i, 0, 0)),
    )
    return pl.pallas_call(
        _in_proj_kernel,
        out_shape=out_shape,
        grid=(t // tm,),
        in_specs=[row(d), row(1), _const_spec((1, LANES)), _const_spec((1, d)), _const_spec(w_main.shape),
                  _const_spec(w_abt.shape)],
        out_specs=out_specs,
        compiler_params=_cparams(1),
        name="in_proj",
    )(x2, pos_col, invf_row, gain, w_main, w_abt)


def _lockstep(gens):
    results = [None] * len(gens)
    active = list(enumerate(gens))
    while active:
        still = []
        for i, g in active:
            try:
                next(g)
                still.append((i, g))
            except StopIteration as stop:
                results[i] = stop.value
        active = still
    return results


def _win_attn_kernel(sink_ref, q_ref, kp_ref, kc_ref, kn_ref, vp_ref, vc_ref, vn_ref, o_ref):
    n = pl.program_id(1)
    ns = pl.num_programs(1)
    blk = ATTN_BLOCK
    hd = ATTN_HEAD_DIM
    nq = q_ref.shape[0] // blk
    group = ATTN_HEADS // ATTN_KV_HEADS
    lane = lax.broadcasted_iota(jnp.int32, (1, LANES), 1)
    lo = lane < hd

    def halves(x):
        swapped = pltpu.roll(x.astype(F32), hd, 1).astype(BF16)
        z = jnp.zeros_like(x)
        return ((jnp.where(lo, x, z), jnp.where(lo, z, swapped)),
                (jnp.where(lo, swapped, z), jnp.where(lo, z, x)))

    kh = [halves(kp_ref[...])] + [halves(kc_ref[i * blk:(i + 1) * blk, :]) for i in range(nq)] + [halves(kn_ref[...])]
    vh = [halves(vp_ref[...])] + [halves(vc_ref[i * blk:(i + 1) * blk, :]) for i in range(nq)] + [halves(vn_ref[...])]

    def blockdiag(parts, i, g):
        return jnp.concatenate([parts[i + w][g][e] for e in range(2) for w in range(3)], axis=0)

    qi = lax.broadcasted_iota(jnp.int32, (blk, blk), 0)
    ki = lax.broadcasted_iota(jnp.int32, (blk, blk), 1)
    ninf = jnp.float32(-jnp.inf)
    zer = jnp.zeros((blk, blk), F32)

    def band_bias(i):
        has_prev = jnp.where(n > 0, 0.0, ninf) if i == 0 else jnp.float32(0.0)
        has_next = jnp.where(n < ns - 1, 0.0, ninf) if i == nq - 1 else jnp.float32(0.0)
        return jnp.concatenate([jnp.where(ki >= qi, has_prev, ninf), zer,
                                jnp.where(ki <= qi, has_next, ninf)], axis=1)

    def chain(i, m, kbd, vbd, bias):
        rows = slice(i * blk, (i + 1) * blk)
        cols = slice(m * LANES, (m + 1) * LANES)
        s = lax.dot_general(q_ref[rows, cols], kbd, (((1,), (1,)), ((), ())), preferred_element_type=F32)
        yield
        ps, ds = [], []
        for e in range(2):
            sk = sink_ref[2 * m + e]
            se = s[:, e * 3 * blk:(e + 1) * 3 * blk] + bias
            mx = jnp.maximum(jnp.max(se, axis=-1, keepdims=True), sk)
            p = jnp.exp(se - mx)
            ds.append(jnp.sum(p, axis=-1, keepdims=True) + jnp.exp(sk - mx))
            ps.append(p.astype(BF16))
        o = jnp.dot(jnp.concatenate(ps, axis=1), vbd, preferred_element_type=F32)
        yield
        o_ref[rows, cols] = (o / jnp.where(lo, ds[0], ds[1])).astype(BF16)

    chains = []
    for i in range(nq):
        bias = band_bias(i)
        for g in range(ATTN_KV_HEADS):
            kbd = blockdiag(kh, i, g)
            vbd = blockdiag(vh, i, g)
            for m in range(g * group // 2, (g + 1) * group // 2):
                chains.append(chain(i, m, kbd, vbd, bias))
    _lockstep(chains)


def _win_attn(sink, q, k, v, batch, seq, nq):
    nb = seq // ATTN_BLOCK
    ns = nb // nq
    aw = q.shape[1]
    kvw = k.shape[1]
    cur = lambda b, n: (b * ns + n, 0)
    prev = lambda b, n: (b * nb + jnp.maximum(n * nq - 1, 0), 0)
    nxt = lambda b, n: (b * nb + jnp.minimum(n * nq + nq, nb - 1), 0)
    edge = lambda f: pl.BlockSpec((ATTN_BLOCK, kvw), f)
    own = pl.BlockSpec((nq * ATTN_BLOCK, kvw), cur)
    return pl.pallas_call(
        _win_attn_kernel,
        out_shape=jax.ShapeDtypeStruct(q.shape, BF16),
        grid=(batch, ns),
        in_specs=[pl.BlockSpec(memory_space=pltpu.SMEM),
                  pl.BlockSpec((nq * ATTN_BLOCK, aw), cur),
                  edge(prev), own, edge(nxt), edge(prev), own, edge(nxt)],
        out_specs=pl.BlockSpec((nq * ATTN_BLOCK, aw), cur),
        compiler_params=_cparams(2),
        name="win_attn",
    )(sink, q, k, k, k, v, v, v)


def _cumsum_both(x, axis):
    n = x.shape[axis]
    idx = lax.broadcasted_iota(jnp.int32, x.shape, axis)
    pre, suf = x, x
    s = 1
    while s < n:
        pre = pre + jnp.where(idx >= s, pltpu.roll(pre, s, axis), 0.0)
        suf = suf + jnp.where(idx < n - s, pltpu.roll(suf, n - s, axis), 0.0)
        s *= 2
    return pre, suf


def _unit_tri_inverse(nmat, rc_xor):
    c = nmat.shape[0]
    n0 = jnp.where(rc_xor < TRI_BASE, nmat, 0.0)
    nb16 = n0.astype(BF16)
    p = jnp.dot(nb16, nb16, preferred_element_type=F32)
    yield
    y = n0
    n_sq = (TRI_BASE - 1).bit_length() - 1
    for m in range(n_sq):
        pb = p.astype(BF16)
        if m < n_sq - 1:
            r2 = jnp.dot(pb, jnp.concatenate([y.astype(BF16), pb], axis=1), preferred_element_type=F32)
            yield
            y = y + p + r2[:, :c]
            p = r2[:, c:]
        else:
            py = jnp.dot(pb, y.astype(BF16), preferred_element_type=F32)
            yield
            y = y + p + py
    blk = TRI_BASE
    while blk < c:
        l_off = jnp.where((rc_xor >= blk) & (rc_xor < 2 * blk), -nmat, 0.0)
        ly = jnp.dot(l_off.astype(BF16), y.astype(BF16), preferred_element_type=F32)
        yield
        u = l_off + ly
        yu = jnp.dot(y.astype(BF16), u.astype(BF16), preferred_element_type=F32)
        yield
        y = y - u - yu
        blk *= 2
    return y


def _deltanet_kernel(dq_ref, dk_ref, dv_ref, gate_ref, abc_ref, abr_ref, cw_ref,
                     alog_r_ref, dtb_r_ref, alog_c_ref, dtb_c_ref, gout_ref,
                     o_ref, xp, qn, kn, vn, kt, oacc, st):
    seq = dq_ref.shape[0]
    nc = seq // DN_CHUNK
    c = DN_CHUNK
    hd = DN_HEAD_DIM
    width = DN_HEADS * hd

    xp[0:CONV_HALO, :] = jnp.zeros((CONV_HALO, width), F32)
    xp[CONV_HALO + seq:2 * CONV_HALO + seq, :] = jnp.zeros((CONV_HALO, width), F32)

    def conv_phase(src_ref, w_off, dst, l2, scale):
        def stage(i, carry):
            r = pl.multiple_of(i * c, c)
            xp[pl.ds(CONV_HALO + r, c), :] = src_ref[pl.ds(r, c), :].astype(F32)
            return carry
        lax.fori_loop(0, nc, stage, 0)

        def body(i, carry):
            r = pl.multiple_of(i * c, c)
            for h in range(DN_HEADS):
                xa = xp[pl.ds(r, c + 2 * CONV_HALO), h * hd:(h + 1) * hd]
                y = jnp.zeros((c, hd), F32)
                for j in range(DN_CONV):
                    a = CONV_HALO - DN_CONV // 2 + j
                    wj = cw_ref[j:j + 1, w_off + h * hd:w_off + (h + 1) * hd]
                    y = y + wj * xa[a:a + c, :]
                y = _silu(y)
                if l2:
                    y = y * lax.rsqrt(jnp.sum(y * y, axis=-1, keepdims=True) + EPS)
                    if scale != 1.0:
                        y = y * scale
                dst[i, :, h * hd:(h + 1) * hd] = y.astype(BF16)
                if dst is kn:
                    kt[i, h * hd:(h + 1) * hd, :] = y.T.astype(BF16)
            return carry
        lax.fori_loop(0, nc, body, 0)

    conv_phase(dq_ref, 0, qn, True, hd ** -0.5)
    conv_phase(dk_ref, width, kn, True, 1.0)
    conv_phase(dv_ref, 2 * width, vn, False, 1.0)

    st[...] = jnp.zeros(st.shape, F32)
    row = lax.broadcasted_iota(jnp.int32, (c, c), 0)
    col = lax.broadcasted_iota(jnp.int32, (c, c), 1)
    eye = row == col
    masks = ((row >= col, row > col), (row <= col, row < col))
    nh = DN_HEADS

    def gates(ci):
        r = pl.multiple_of(ci * c, c)
        ab_c = abc_ref[pl.ds(r, c), :]
        ab_r = abr_ref[ci]
        g_c = -jnp.exp(alog_r_ref[...]) * _softplus(ab_c + dtb_r_ref[...])
        g_r = -jnp.exp(alog_c_ref[...]) * _softplus(ab_r + dtb_c_ref[...])
        return dict(
            cum_c=_cumsum_both(g_c, 0), cum_r=_cumsum_both(g_r, 1),
            beta_c=jax.nn.sigmoid(ab_c), beta_r=jax.nn.sigmoid(ab_r),
            gtot=jnp.sum(g_r, axis=1, keepdims=True))

    def chain(ci, h, d, gt, first_visit):
        j = d * nh + h
        jb = N_DIRS * nh + j
        gc_c = gt["cum_c"][d][:, j:j + 1]
        gc_r = gt["cum_r"][d][j:j + 1, :]
        b_c = gt["beta_c"][:, jb:jb + 1]
        b_r = gt["beta_r"][jb:jb + 1, :]
        gtot = gt["gtot"][j:j + 1, :]
        incl, strict = masks[d]
        hs = slice(h * hd, (h + 1) * hd)
        ktc = kt[ci, hs, :]
        qk2 = jnp.concatenate([qn[ci, :, hs], kn[ci, :, hs]], axis=0)
        g1 = jnp.dot(qk2, jnp.concatenate([ktc, st[j].astype(BF16)], axis=1), preferred_element_type=F32)
        yield
        decay = jnp.exp(jnp.where(incl, gc_c - gc_r, -jnp.inf))
        nmat = jnp.where(strict, -(g1[c:, :c] * b_c) * decay, 0.0)
        y = yield from _unit_tri_inverse(nmat, row ^ col)
        tinv = y + jnp.where(eye, 1.0, 0.0)
        eg_r = jnp.exp(gc_r)
        m1 = jnp.concatenate([tinv * b_r, -(tinv * (b_r * eg_r))], axis=1).astype(BF16)
        v_new = jnp.dot(m1, jnp.concatenate([vn[ci, :, hs], g1[c:, c:].astype(BF16)], axis=0),
                        preferred_element_type=F32)
        yield
        vb = v_new.astype(BF16)
        m2 = jnp.concatenate([jnp.where(eye, eg_r, 0.0), g1[:c, :c] * decay], axis=1).astype(BF16)
        o = jnp.dot(m2, jnp.concatenate([g1[:c, c:].astype(BF16), vb], axis=0), preferred_element_type=F32)
        kte = (ktc.astype(F32) * jnp.exp(gtot - gc_r)).astype(BF16)
        ds = jnp.dot(kte, vb, preferred_element_type=F32)
        yield
        st[j] = st[j] * jnp.exp(gtot) + ds
        if first_visit:
            oacc[ci, :, hs] = o
        else:
            r = pl.multiple_of(ci * c, c)
            tot = oacc[ci, :, hs] + o
            gate = gate_ref[pl.ds(r, c), hs].astype(F32)
            o_ref[pl.ds(r, c), hs] = (_rms(tot, gout_ref[...]) * _silu(gate)).astype(BF16)

    def step(s, first_visit):
        cf = s
        cb = nc - 1 - s
        gf = gates(cf)
        gb = gates(cb)
        _lockstep([chain(cf, h, 0, gf, first_visit) for h in range(nh)]
                  + [chain(cb, h, 1, gb, first_visit) for h in range(nh)])

    def first_half(s, carry):
        step(s, True)
        return carry

    def second_half(s, carry):
        step(s, False)
        return carry

    lax.fori_loop(0, nc // 2, first_half, 0)
    lax.fori_loop(nc // 2, nc, second_half, 0)


def _deltanet(dqkv, gate, abc, abr, conv_w, alog_r, dtb_r, alog_c, dtb_c, gout, batch, seq):
    width = DN_HEADS * DN_HEAD_DIM
    nc = seq // DN_CHUNK
    n_ab = abr.shape[1]
    col = lambda j: pl.BlockSpec((seq, width), lambda b, j=j: (b, j))
    return pl.pallas_call(
        _deltanet_kernel,
        out_shape=jax.ShapeDtypeStruct((batch * seq, width), BF16),
        grid=(batch,),
        in_specs=[col(0), col(1), col(2), col(0),
                  pl.BlockSpec((seq, LANES), lambda b: (b, 0)),
                  pl.BlockSpec((nc, n_ab, DN_CHUNK), lambda b: (b, 0, 0)),
                  _const_spec(conv_w.shape),
                  _const_spec(alog_r.shape), _const_spec(dtb_r.shape),
                  _const_spec(alog_c.shape), _const_spec(dtb_c.shape),
                  _const_spec(gout.shape)],
        out_specs=col(0),
        scratch_shapes=[
            pltpu.VMEM((seq + 2 * CONV_HALO, width), F32),
            pltpu.VMEM((nc, DN_CHUNK, width), BF16),
            pltpu.VMEM((nc, DN_CHUNK, width), BF16),
            pltpu.VMEM((nc, DN_CHUNK, width), BF16),
            pltpu.VMEM((nc, width, DN_CHUNK), BF16),
            pltpu.VMEM((nc, DN_CHUNK, width), F32),
            pltpu.VMEM((N_DIRS * DN_HEADS, DN_HEAD_DIM, DN_HEAD_DIM), F32),
        ],
        compiler_params=_cparams(1),
        name="deltanet",
    )(dqkv, dqkv, dqkv, gate, abc, abr, conv_w, alog_r, dtb_r, alog_c, dtb_c, gout)


def _mem_kv_kernel(mem_ref, gain_ref, w_ref, k_ref, v_ref):
    d = k_ref.shape[1]
    mn = _rms(mem_ref[...], gain_ref[...]).astype(BF16)
    k_ref[...] = jnp.dot(mn, w_ref[:, :d], preferred_element_type=F32).astype(BF16)
    v_ref[...] = jnp.dot(mn, w_ref[:, d:], preferred_element_type=F32).astype(BF16)


def _mem_kv(mem2, gain, w_ckv, mem_len):
    rows, d = mem2.shape
    spec = pl.BlockSpec((mem_len, d), lambda b: (b, 0))
    return pl.pallas_call(
        _mem_kv_kernel,
        out_shape=(jax.ShapeDtypeStruct((rows, d), BF16),) * 2,
        grid=(rows // mem_len,),
        in_specs=[spec, _const_spec((1, d)), _const_spec(w_ckv.shape)],
        out_specs=(spec, spec),
        compiler_params=_cparams(1),
        name="mem_kv",
    )(mem2, gain, w_ckv)


def _out_cross_kernel(x_ref, ao_ref, do_ref, wout_ref, gpost_ref, gpre_ref, wq_ref, km_ref, vm_ref,
                      wo_ref, gcpost_ref, y_ref):
    aw = ao_ref.shape[1]
    mix = (jnp.dot(ao_ref[...], wout_ref[:aw, :], preferred_element_type=F32)
           + jnp.dot(do_ref[...], wout_ref[aw:, :], preferred_element_type=F32))
    x1 = x_ref[...] + _rms(mix, gpost_ref[...])
    hc = _rms(x1, gpre_ref[...]).astype(BF16)
    d = x1.shape[1]
    chd = d // CROSS_HEADS
    q = (jnp.dot(hc, wq_ref[...], preferred_element_type=F32) * (chd ** -0.5)).astype(BF16)
    heads = []
    for h in range(CROSS_HEADS):
        hs = slice(h * chd, (h + 1) * chd)
        s = lax.dot_general(q[:, hs], km_ref[:, hs], (((1,), (1,)), ((), ())), preferred_element_type=F32)
        p = jnp.exp(s - jnp.max(s, axis=-1, keepdims=True))
        o = jnp.dot(p.astype(BF16), vm_ref[:, hs], preferred_element_type=F32)
        heads.append((o / jnp.sum(p, axis=-1, keepdims=True)).astype(BF16))
    cproj = jnp.dot(jnp.concatenate(heads, axis=1), wo_ref[...], preferred_element_type=F32)
    y_ref[...] = x1 + _rms(cproj, gcpost_ref[...])


def _out_cross(x2, attn_o, dn_o, w_out, g_post, g_pre, w_cq, kmem, vmem, w_co, g_cpost, seq, mem_len, tm):
    t, d = x2.shape
    per_batch = seq // tm
    row = lambda w: pl.BlockSpec((tm, w), lambda i: (i, 0))
    memspec = pl.BlockSpec((mem_len, d), lambda i: (i // per_batch, 0))
    vec = _const_spec((1, d))
    return pl.pallas_call(
        _out_cross_kernel,
        out_shape=jax.ShapeDtypeStruct((t, d), F32),
        grid=(t // tm,),
        in_specs=[row(d), row(attn_o.shape[1]), row(dn_o.shape[1]), _const_spec(w_out.shape), vec, vec,
                  _const_spec(w_cq.shape), memspec, memspec, _const_spec(w_co.shape), vec],
        out_specs=row(d),
        compiler_params=_cparams(1),
        name="out_cross",
    )(x2, attn_o, dn_o, w_out, g_post, g_pre, w_cq, kmem, vmem, w_co, g_cpost)


def _ffn_kernel(x_ref, gpre_ref, wgu_ref, wdown_ref, gpost_ref, y_ref):
    x = x_ref[...]
    dff = wdown_ref.shape[0]
    hf = _rms(x, gpre_ref[...]).astype(BF16)
    gu = jnp.dot(hf, wgu_ref[...], preferred_element_type=F32)
    act = (_silu(gu[:, :dff]) * gu[:, dff:]).astype(BF16)
    f = jnp.dot(act, wdown_ref[...], preferred_element_type=F32)
    y_ref[...] = x + _rms(f, gpost_ref[...])


def _ffn(x2, g_pre, w_gu, w_down, g_post, tm):
    t, d = x2.shape
    row = pl.BlockSpec((tm, d), lambda i: (i, 0))
    vec = _const_spec((1, d))
    return pl.pallas_call(
        _ffn_kernel,
        out_shape=jax.ShapeDtypeStruct((t, d), F32),
        grid=(t // tm,),
        in_specs=[row, vec, _const_spec(w_gu.shape), _const_spec(w_down.shape), vec],
        out_specs=row,
        compiler_params=_cparams(1),
        name="ffn",
    )(x2, g_pre, w_gu, w_down, g_post)


def _layer(x, mem, positions, g_mix_pre, w_in, conv_w, a_log, dt_bias, g_dn_out, attn_sink, w_out,
           g_mix_post, g_cross_pre, g_mem, w_cq, w_ckv, w_co, g_cross_post, g_ffn_pre, w_gate_up,
           w_down, g_ffn_post):
    batch, seq, d = x.shape
    mem_len = mem.shape[1]
    t = batch * seq
    x2 = x.reshape(t, d)
    vec = lambda g: g.reshape(1, -1).astype(F32)

    half = ATTN_HEAD_DIM // 2
    inv_freq = ROPE_THETA ** (-jnp.arange(half, dtype=F32) / half)
    invf_row = jnp.tile(inv_freq, LANES // half).reshape(1, LANES)

    n_ab = 2 * N_DIRS * DN_HEADS
    n_main = w_in.shape[1] - n_ab
    w_main = jnp.pad(w_in, ((0, 0), (0, LANES - n_ab))).astype(BF16)
    w_abt = w_in[:, n_main:].T.astype(BF16)
    q, k, v, dqkv, dgate, abc, abr = _in_proj(x2, positions.reshape(t, 1), invf_row, vec(g_mix_pre),
                                              w_main, w_abt, tm=512)

    attn_o = _win_attn(attn_sink.astype(F32), q, k, v, batch, seq, nq=min(4, seq // ATTN_BLOCK))

    pad_r = lambda p: jnp.pad(p.reshape(1, -1).astype(F32), ((0, 0), (0, LANES - N_DIRS * DN_HEADS)))
    pad_c = lambda p: jnp.pad(p.reshape(-1, 1).astype(F32), ((0, n_ab - N_DIRS * DN_HEADS), (0, 0)))
    dn_o = _deltanet(dqkv, dgate, abc, abr, conv_w.astype(F32), pad_r(a_log), pad_r(dt_bias),
                     pad_c(a_log), pad_c(dt_bias), vec(g_dn_out), batch, seq)

    kmem, vmem = _mem_kv(mem.reshape(batch * mem_len, d), vec(g_mem), w_ckv.astype(BF16), mem_len)
    x3 = _out_cross(x2, attn_o, dn_o, w_out.astype(BF16), vec(g_mix_post), vec(g_cross_pre),
                    w_cq.astype(BF16), kmem, vmem, w_co.astype(BF16), vec(g_cross_post), seq, mem_len, tm=512)
    y = _ffn(x3, vec(g_ffn_pre), w_gate_up.astype(BF16), w_down.astype(BF16), vec(g_ffn_post), tm=256)
    return y.reshape(batch, seq, d)


def kernel(x, mem, positions, g_mix_pre, w_in, conv_w, a_log, dt_bias, g_dn_out, attn_sink, w_out,
           g_mix_post, g_cross_pre, g_mem, w_cq, w_ckv, w_co, g_cross_post, g_ffn_pre, w_gate_up,
           w_down, g_ffn_post):
    depth = w_in.shape[0]
    for l in range(depth):
        x = _layer(x, mem, positions, g_mix_pre[l], w_in[l], conv_w[l], a_log[l], dt_bias[l], g_dn_out[l],
                   attn_sink[l], w_out[l], g_mix_post[l], g_cross_pre[l], g_mem[l], w_cq[l], w_ckv[l],
                   w_co[l], g_cross_post[l], g_ffn_pre[l], w_gate_up[l], w_down[l], g_ffn_post[l])
    return x
```

```python
import functools

import jax
import jax.numpy as jnp
from jax import lax
from jax.experimental import pallas as pl
from jax.experimental.pallas import tpu as pltpu

F32 = jnp.float32
BF16 = jnp.bfloat16

EPS = 1e-6
ROPE_THETA = 10000.0
ATTN_HEADS = 8
ATTN_KV_HEADS = 2
ATTN_HEAD_DIM = 64
ATTN_BLOCK = 128
DN_HEADS = 4
DN_HEAD_DIM = 128
DN_CONV = 5
N_DIRS = 2
CROSS_HEADS = 4

LANES = 128
DN_CHUNK = 128
TRI_BASE = 16
CONV_HALO = 8
VMEM_LIMIT = 56 * 1024 * 1024


def _cparams(n_grid_dims):
    return pltpu.CompilerParams(
        dimension_semantics=("arbitrary",) * n_grid_dims,
        vmem_limit_bytes=VMEM_LIMIT,
    )


def _rms(x, gain):
    return x * lax.rsqrt(jnp.mean(x * x, axis=-1, keepdims=True) + EPS) * gain


def _silu(x):
    return x * jax.nn.sigmoid(x)


def _softplus(x):
    return jnp.maximum(x, 0.0) + jnp.log(1.0 + jnp.exp(-jnp.abs(x)))


def _const_spec(shape):
    nd = len(shape)
    return pl.BlockSpec(shape, lambda *_: (0,) * nd)


def _in_proj_kernel(x_ref, pos_ref, invf_ref, gain_ref, w_ref, wabt_ref,
                    q_ref, k_ref, v_ref, dqkv_ref, gate_ref, abr_ref):
    h = _rms(x_ref[...], gain_ref[...]).astype(BF16)
    half = ATTN_HEAD_DIM // 2
    ang = pos_ref[...].astype(F32) * invf_ref[...]
    cos = jnp.cos(ang)
    sin = jnp.sin(ang)
    lane = lax.broadcasted_iota(jnp.int32, (1, LANES), 1)
    first_half = (lane % ATTN_HEAD_DIM) < half

    def rope(t):
        rot = jnp.where(first_half, -pltpu.roll(t, LANES - half, 1), pltpu.roll(t, half, 1))
        return t * cos + rot * sin

    def proj(lo, hi):
        return jnp.dot(h, w_ref[:, lo:hi], preferred_element_type=F32)

    aw = ATTN_HEADS * ATTN_HEAD_DIM
    kvw = ATTN_KV_HEADS * ATTN_HEAD_DIM
    q = proj(0, aw)
    for m in range(aw // LANES):
        cols = slice(m * LANES, (m + 1) * LANES)
        q_ref[:, cols] = (rope(q[:, cols]) * (ATTN_HEAD_DIM ** -0.5)).astype(BF16)
    kv = proj(aw, aw + 2 * kvw)
    k_ref[...] = rope(kv[:, :kvw]).astype(BF16)
    v_ref[...] = kv[:, kvw:].astype(BF16)
    off = aw + 2 * kvw
    dn = dqkv_ref.shape[1]
    dqkv_ref[...] = proj(off, off + dn).astype(BF16)
    gate_ref[...] = proj(off + dn, off + dn + gate_ref.shape[1]).astype(BF16)
    abr = lax.dot_general(wabt_ref[...], h, (((1,), (1,)), ((), ())), preferred_element_type=F32)
    for j in range(abr_ref.shape[0]):
        abr_ref[j] = abr[:, j * DN_CHUNK:(j + 1) * DN_CHUNK]


def _in_proj(x2, pos_col, invf_row, gain, w_main, w_abt, tm):
    t, d = x2.shape
    aw = ATTN_HEADS * ATTN_HEAD_DIM
    kvw = ATTN_KV_HEADS * ATTN_HEAD_DIM
    dnw = DN_HEADS * DN_HEAD_DIM
    n_ab = w_abt.shape[0]
    row = lambda w: pl.BlockSpec((tm, w), lambda i: (i, 0))
    out_shape = (
        jax.ShapeDtypeStruct((t, aw), BF16),
        jax.ShapeDtypeStruct((t, kvw), BF16),
        jax.ShapeDtypeStruct((t, kvw), BF16),
        jax.ShapeDtypeStruct((t, 3 * dnw), BF16),
        jax.ShapeDtypeStruct((t, dnw), BF16),
        jax.ShapeDtypeStruct((t // DN_CHUNK, n_ab, DN_CHUNK), F32),
    )
    out_specs = (
        row(aw), row(kvw), row(kvw), row(3 * dnw), row(dnw),
        pl.BlockSpec((tm // DN_CHUNK, n_ab, DN_CHUNK), lambda i: (i, 0, 0)),
    )
    return pl.pallas_call(
        _in_proj_kernel,
        out_shape=out_shape,
        grid=(t // tm,),
        in_specs=[row(d), row(1), _const_spec((1, LANES)), _const_spec((1, d)), _const_spec(w_main.shape),
                  _const_spec(w_abt.shape)],
        out_specs=out_specs,
        compiler_params=_cparams(1),
        name="in_proj",
    )(x2, pos_col, invf_row, gain, w_main, w_abt)


def _paired_dots(reqs):
    out = [None] * len(reqs)
    todo = [i for i, r in enumerate(reqs) if r is not None]
    while todo:
        i = todo.pop(0)
        l1, r1 = reqs[i]
        j = next((t for t in todo if reqs[t][0].shape == l1.shape and reqs[t][1].shape == r1.shape), None)
        if j is None or r1.shape[1] != LANES:
            out[i] = jnp.dot(l1, r1, preferred_element_type=F32)
            continue
        todo.remove(j)
        l2, r2 = reqs[j]
        z = jnp.zeros_like(r1)
        rhs = jnp.concatenate([jnp.concatenate([r1, z], axis=1), jnp.concatenate([z, r2], axis=1)], axis=0)
        both = jnp.dot(jnp.concatenate([l1, l2], axis=1), rhs, preferred_element_type=F32)
        out[i] = both[:, :LANES]
        out[j] = both[:, LANES:]
    return out


def _lockstep(gens):
    live = list(gens)
    sends = [None] * len(live)
    while live:
        reqs, still = [], []
        for g, val in zip(live, sends):
            try:
                reqs.append(g.send(val))
                still.append(g)
            except StopIteration:
                pass
        live = still
        sends = _paired_dots(reqs)


def _win_attn_kernel(sink_ref, q_ref, kp_ref, kc_ref, kn_ref, vp_ref, vc_ref, vn_ref, o_ref):
    n = pl.program_id(1)
    ns = pl.num_programs(1)
    blk = ATTN_BLOCK
    hd = ATTN_HEAD_DIM
    nq = q_ref.shape[0] // blk
    group = ATTN_HEADS // ATTN_KV_HEADS
    lane = lax.broadcasted_iota(jnp.int32, (1, LANES), 1)
    lo = lane < hd

    def halves(x):
        swapped = pltpu.roll(x.astype(F32), hd, 1).astype(BF16)
        z = jnp.zeros_like(x)
        return ((jnp.where(lo, x, z), jnp.where(lo, z, swapped)),
                (jnp.where(lo, swapped, z), jnp.where(lo, z, x)))

    kh = [halves(kp_ref[...])] + [halves(kc_ref[i * blk:(i + 1) * blk, :]) for i in range(nq)] + [halves(kn_ref[...])]
    vh = [halves(vp_ref[...])] + [halves(vc_ref[i * blk:(i + 1) * blk, :]) for i in range(nq)] + [halves(vn_ref[...])]

    def blockdiag(parts, i, g):
        return jnp.concatenate([parts[i + w][g][e] for e in range(2) for w in range(3)], axis=0)

    qi = lax.broadcasted_iota(jnp.int32, (blk, blk), 0)
    ki = lax.broadcasted_iota(jnp.int32, (blk, blk), 1)
    ninf = jnp.float32(-jnp.inf)
    zer = jnp.zeros((blk, blk), F32)

    def band_bias(i):
        has_prev = jnp.where(n > 0, 0.0, ninf) if i == 0 else jnp.float32(0.0)
        has_next = jnp.where(n < ns - 1, 0.0, ninf) if i == nq - 1 else jnp.float32(0.0)
        return jnp.concatenate([jnp.where(ki >= qi, has_prev, ninf), zer,
                                jnp.where(ki <= qi, has_next, ninf)], axis=1)

    def chain(i, m, kbd, vbd, bias):
        rows = slice(i * blk, (i + 1) * blk)
        cols = slice(m * LANES, (m + 1) * LANES)
        s = lax.dot_general(q_ref[rows, cols], kbd, (((1,), (1,)), ((), ())), preferred_element_type=F32)
        yield
        ps, ds = [], []
        for e in range(2):
            sk = sink_ref[2 * m + e]
            se = s[:, e * 3 * blk:(e + 1) * 3 * blk] + bias
            mx = jnp.maximum(jnp.max(se, axis=-1, keepdims=True), sk)
            p = jnp.exp(se - mx)
            ds.append(jnp.sum(p, axis=-1, keepdims=True) + jnp.exp(sk - mx))
            ps.append(p.astype(BF16))
        o = jnp.dot(jnp.concatenate(ps, axis=1), vbd, preferred_element_type=F32)
        yield
        o_ref[rows, cols] = (o / jnp.where(lo, ds[0], ds[1])).astype(BF16)

    chains = []
    for i in range(nq):
        bias = band_bias(i)
        for g in range(ATTN_KV_HEADS):
            kbd = blockdiag(kh, i, g)
            vbd = blockdiag(vh, i, g)
            for m in range(g * group // 2, (g + 1) * group // 2):
                chains.append(chain(i, m, kbd, vbd, bias))
    _lockstep(chains)


def _win_attn(sink, q, k, v, batch, seq, nq):
    nb = seq // ATTN_BLOCK
    ns = nb // nq
    aw = q.shape[1]
    kvw = k.shape[1]
    cur = lambda b, n: (b * ns + n, 0)
    prev = lambda b, n: (b * nb + jnp.maximum(n * nq - 1, 0), 0)
    nxt = lambda b, n: (b * nb + jnp.minimum(n * nq + nq, nb - 1), 0)
    edge = lambda f: pl.BlockSpec((ATTN_BLOCK, kvw), f)
    own = pl.BlockSpec((nq * ATTN_BLOCK, kvw), cur)
    return pl.pallas_call(
        _win_attn_kernel,
        out_shape=jax.ShapeDtypeStruct(q.shape, BF16),
        grid=(batch, ns),
        in_specs=[pl.BlockSpec(memory_space=pltpu.SMEM),
                  pl.BlockSpec((nq * ATTN_BLOCK, aw), cur),
                  edge(prev), own, edge(nxt), edge(prev), own, edge(nxt)],
        out_specs=pl.BlockSpec((nq * ATTN_BLOCK, aw), cur),
        compiler_params=_cparams(2),
        name="win_attn",
    )(sink, q, k, k, k, v, v, v)


def _cumsum_both(x, axis):
    n = x.shape[axis]
    idx = lax.broadcasted_iota(jnp.int32, x.shape, axis)
    pre, suf = x, x
    s = 1
    while s < n:
        pre = pre + jnp.where(idx >= s, pltpu.roll(pre, s, axis), 0.0)
        suf = suf + jnp.where(idx < n - s, pltpu.roll(suf, n - s, axis), 0.0)
        s *= 2
    return pre, suf


def _unit_tri_inverse(nmat, rc_xor, lower):
    c = nmat.shape[0]
    n0 = jnp.where(rc_xor < TRI_BASE, nmat, 0.0)
    nb16 = n0.astype(BF16)
    p = yield (nb16, nb16)
    y = n0
    n_sq = (TRI_BASE - 1).bit_length() - 1
    for m in range(n_sq):
        pb = p.astype(BF16)
        if m < n_sq - 1:
            r2 = jnp.dot(pb, jnp.concatenate([y.astype(BF16), pb], axis=1), preferred_element_type=F32)
            yield
            y = y + p + r2[:, :c]
            p = r2[:, c:]
        else:
            py = yield (pb, y.astype(BF16))
            y = y + p + py
    neg_n = -nmat
    blk = TRI_BASE
    while blk < c:
        first = blk if lower else 0
        held = [slice(s, s + blk) for s in range(first, c, 2 * blk)]
        kept = [slice(s, s + blk) for s in range(blk - first, c, 2 * blk)]
        take = lambda a: jnp.concatenate([a[s] for s in held], axis=0)
        rx = take(rc_xor)
        l_rows = jnp.where((rx >= blk) & (rx < 2 * blk), take(neg_n), 0.0)
        y_rows = take(y)
        ly = yield (l_rows.astype(BF16), y.astype(BF16))
        u_rows = l_rows + ly
        ub = u_rows.astype(BF16)
        zero = jnp.zeros((blk, c), BF16)
        pieces = [ub[i * blk:(i + 1) * blk] for i in range(len(held))]
        order = [zero, None] if lower else [None, zero]
        u_full = jnp.concatenate([(pieces[i] if o is None else o) for i in range(len(held)) for o in order], axis=0)
        yu = yield (y_rows.astype(BF16), u_full)
        new_rows = y_rows - u_rows - yu
        parts = {}
        for i, s in enumerate(held):
            parts[s.start] = new_rows[i * blk:(i + 1) * blk]
        for s in kept:
            parts[s.start] = y[s]
        y = jnp.concatenate([parts[k] for k in sorted(parts)], axis=0)
        blk *= 2
    return y


def _deltanet_kernel(dq_ref, dk_ref, dv_ref, gate_ref, abr_ref, cw_ref, alog_ref, dtb_ref, gout_ref,
                     o_ref, xp, qn, kn, vn, kt, oacc, st, pm1, pm2, pkte, pgl, gcum, gbeta, gcols, gsum):
    seq = dq_ref.shape[0]
    nc = seq // DN_CHUNK
    c = DN_CHUNK
    hd = DN_HEAD_DIM
    width = DN_HEADS * hd

    xp[0:CONV_HALO, :] = jnp.zeros((CONV_HALO, width), F32)
    xp[CONV_HALO + seq:2 * CONV_HALO + seq, :] = jnp.zeros((CONV_HALO, width), F32)

    def conv_phase(src_ref, w_off, dst, l2, scale):
        def stage(i, carry):
            r = pl.multiple_of(i * c, c)
            xp[pl.ds(CONV_HALO + r, c), :] = src_ref[pl.ds(r, c), :].astype(F32)
            return carry
        lax.fori_loop(0, nc, stage, 0)

        def body(i, carry):
            r = pl.multiple_of(i * c, c)
            for h in range(DN_HEADS):
                xa = xp[pl.ds(r, c + 2 * CONV_HALO), h * hd:(h + 1) * hd]
                y = jnp.zeros((c, hd), F32)
                for j in range(DN_CONV):
                    a = CONV_HALO - DN_CONV // 2 + j
                    wj = cw_ref[j:j + 1, w_off + h * hd:w_off + (h + 1) * hd]
                    y = y + wj * xa[a:a + c, :]
                y = _silu(y)
                if l2:
                    y = y * lax.rsqrt(jnp.sum(y * y, axis=-1, keepdims=True) + EPS)
                    if scale != 1.0:
                        y = y * scale
                dst[i, :, h * hd:(h + 1) * hd] = y.astype(BF16)
                if dst is kn:
                    kt[i, h * hd:(h + 1) * hd, :] = y.T.astype(BF16)
            return carry
        lax.fori_loop(0, nc, body, 0)

    conv_phase(dq_ref, 0, qn, True, hd ** -0.5)
    conv_phase(dk_ref, width, kn, True, 1.0)
    conv_phase(dv_ref, 2 * width, vn, False, 1.0)

    st[...] = jnp.zeros(st.shape, F32)
    row = lax.broadcasted_iota(jnp.int32, (c, c), 0)
    col = lax.broadcasted_iota(jnp.int32, (c, c), 1)
    eye = row == col
    masks = ((row >= col, row > col), (row <= col, row < col))
    nh = DN_HEADS

    n_g = N_DIRS * nh
    n_r = abr_ref.shape[1]
    ab = jnp.concatenate([abr_ref[ci] for ci in range(nc)], axis=0)
    g = -jnp.exp(jnp.tile(alog_ref[...], (nc, 1))) * _softplus(ab + jnp.tile(dtb_ref[...], (nc, 1)))
    beta = jax.nn.sigmoid(ab)
    pre, suf = _cumsum_both(g, 1)
    cum = jnp.where(lax.broadcasted_iota(jnp.int32, g.shape, 0) % n_r < nh, pre, suf)
    gtot_all = jnp.broadcast_to(jnp.sum(g, axis=1, keepdims=True), g.shape)
    for ci in range(nc):
        rows = slice(ci * n_r, (ci + 1) * n_r)
        gcum[ci] = cum[rows]
        gbeta[ci] = beta[rows]
        gsum[ci] = gtot_all[rows]
        packed = jnp.concatenate([cum[ci * n_r:ci * n_r + n_g], -beta[ci * n_r + n_g:(ci + 1) * n_r],
                                  jnp.zeros((c - 2 * n_g, c), F32)], axis=0)
        gcols[ci] = packed.T

    def prep(ci, h, d, slot):
        j = d * nh + h
        jb = N_DIRS * nh + j
        gc_c = gcols[ci, :, j:j + 1]
        gc_r = gcum[ci, j:j + 1, :]
        neg_b_c = gcols[ci, :, jb:jb + 1]
        b_r = gbeta[ci, jb:jb + 1, :]
        gtot = gsum[ci, j:j + 1, 0:1]
        incl, strict = masks[d]
        hs = slice(h * hd, (h + 1) * hd)
        ktc = kt[ci, hs, :]
        qk2 = jnp.concatenate([qn[ci, :, hs], kn[ci, :, hs]], axis=0)
        gram = yield (qk2, ktc)
        decay = jnp.exp(jnp.where(incl, gc_c - gc_r, -jnp.inf))
        nmat = jnp.where(strict, gram[c:] * neg_b_c * decay, 0.0)
        y = yield from _unit_tri_inverse(nmat, row ^ col, lower=(d == 0))
        tinv = y + jnp.where(eye, 1.0, 0.0)
        eg_r = jnp.exp(gc_r)
        pm1[slot] = jnp.concatenate([tinv * b_r, tinv * -(b_r * eg_r)], axis=1).astype(BF16)
        pm2[slot] = jnp.concatenate([jnp.where(eye, eg_r, 0.0), gram[:c] * decay], axis=1).astype(BF16)
        pkte[slot] = (ktc.astype(F32) * jnp.exp(gtot - gc_r)).astype(BF16)
        pgl[slot] = jnp.broadcast_to(jnp.exp(gtot), pgl.shape[1:])

    def scan(ci, h, d, slot):
        j = d * nh + h
        hs = slice(h * hd, (h + 1) * hd)
        qk2 = jnp.concatenate([qn[ci, :, hs], kn[ci, :, hs]], axis=0)
        pm = yield (qk2, st[j].astype(BF16))
        v_new = jnp.dot(pm1[slot], jnp.concatenate([vn[ci, :, hs], pm[c:].astype(BF16)], axis=0),
                        preferred_element_type=F32)
        yield
        vb = v_new.astype(BF16)
        o = jnp.dot(pm2[slot], jnp.concatenate([pm[:c].astype(BF16), vb], axis=0), preferred_element_type=F32)
        ds = yield (pkte[slot], vb)
        st[j] = st[j] * pgl[slot][0:1, :] + ds
        oacc[ci, :, hs] = oacc[ci, :, hs] + o

    def chunk_of(step_idx, d):
        return step_idx if d == 0 else nc - 1 - step_idx

    def preps(step_idx, parity):
        return [prep(chunk_of(step_idx, d), h, d, parity * N_DIRS * nh + d * nh + h)
                for d in range(N_DIRS) for h in range(nh)]

    def scans(step_a, step_b):
        def both(h, d):
            yield from scan(chunk_of(step_a, d), h, d, d * nh + h)
            yield from scan(chunk_of(step_b, d), h, d, N_DIRS * nh + d * nh + h)
        return [both(h, d) for d in range(N_DIRS) for h in range(nh)]

    oacc[...] = jnp.zeros(oacc.shape, F32)
    _lockstep(preps(0, 0) + preps(1, 1))

    def pair_of_steps(i, carry):
        s0 = 2 * i
        nxt0 = jnp.minimum(s0 + 2, nc - 2)
        _lockstep(scans(s0, s0 + 1))
        _lockstep(preps(nxt0, 0) + preps(nxt0 + 1, 1))
        return carry

    lax.fori_loop(0, nc // 2, pair_of_steps, 0)

    def finish(ci, carry):
        r = pl.multiple_of(ci * c, c)
        for h in range(nh):
            hs = slice(h * hd, (h + 1) * hd)
            gate = gate_ref[pl.ds(r, c), hs].astype(F32)
            o_ref[pl.ds(r, c), hs] = (_rms(oacc[ci, :, hs], gout_ref[...]) * _silu(gate)).astype(BF16)
        return carry

    lax.fori_loop(0, nc, finish, 0)


def _deltanet(dqkv, gate, abr, conv_w, alog_c, dtb_c, gout, batch, seq):
    width = DN_HEADS * DN_HEAD_DIM
    nc = seq // DN_CHUNK
    n_ab = abr.shape[1]
    col = lambda j: pl.BlockSpec((seq, width), lambda b, j=j: (b, j))
    return pl.pallas_call(
        _deltanet_kernel,
        out_shape=jax.ShapeDtypeStruct((batch * seq, width), BF16),
        grid=(batch,),
        in_specs=[col(0), col(1), col(2), col(0),
                  pl.BlockSpec((nc, n_ab, DN_CHUNK), lambda b: (b, 0, 0)),
                  _const_spec(conv_w.shape),
                  _const_spec(alog_c.shape), _const_spec(dtb_c.shape),
                  _const_spec(gout.shape)],
        out_specs=col(0),
        scratch_shapes=[
            pltpu.VMEM((seq + 2 * CONV_HALO, width), F32),
            pltpu.VMEM((nc, DN_CHUNK, width), BF16),
            pltpu.VMEM((nc, DN_CHUNK, width), BF16),
            pltpu.VMEM((nc, DN_CHUNK, width), BF16),
            pltpu.VMEM((nc, width, DN_CHUNK), BF16),
            pltpu.VMEM((nc, DN_CHUNK, width), F32),
            pltpu.VMEM((N_DIRS * DN_HEADS, DN_HEAD_DIM, DN_HEAD_DIM), F32),
            pltpu.VMEM((2 * N_DIRS * DN_HEADS, DN_CHUNK, 2 * DN_CHUNK), BF16),
            pltpu.VMEM((2 * N_DIRS * DN_HEADS, DN_CHUNK, 2 * DN_CHUNK), BF16),
            pltpu.VMEM((2 * N_DIRS * DN_HEADS, DN_HEAD_DIM, DN_CHUNK), BF16),
            pltpu.VMEM((2 * N_DIRS * DN_HEADS, 8, LANES), F32),
            pltpu.VMEM((nc, n_ab, DN_CHUNK), F32),
            pltpu.VMEM((nc, n_ab, DN_CHUNK), F32),
            pltpu.VMEM((nc, DN_CHUNK, LANES), F32),
            pltpu.VMEM((nc, n_ab, DN_CHUNK), F32),
        ],
        compiler_params=_cparams(1),
        name="deltanet",
    )(dqkv, dqkv, dqkv, gate, abr, conv_w, alog_c, dtb_c, gout)


def _mem_kv_kernel(mem_ref, gain_ref, w_ref, k_ref, v_ref):
    d = k_ref.shape[1]
    mn = _rms(mem_ref[...], gain_ref[...]).astype(BF16)
    k_ref[...] = jnp.dot(mn, w_ref[:, :d], preferred_element_type=F32).astype(BF16)
    v_ref[...] = jnp.dot(mn, w_ref[:, d:], preferred_element_type=F32).astype(BF16)


def _mem_kv(mem2, gain, w_ckv, mem_len):
    rows, d = mem2.shape
    spec = pl.BlockSpec((mem_len, d), lambda b: (b, 0))
    return pl.pallas_call(
        _mem_kv_kernel,
        out_shape=(jax.ShapeDtypeStruct((rows, d), BF16),) * 2,
        grid=(rows // mem_len,),
        in_specs=[spec, _const_spec((1, d)), _const_spec(w_ckv.shape)],
        out_specs=(spec, spec),
        compiler_params=_cparams(1),
        name="mem_kv",
    )(mem2, gain, w_ckv)


def _out_cross_kernel(x_ref, ao_ref, do_ref, wout_ref, gpost_ref, gpre_ref, wq_ref, km_ref, vm_ref,
                      wo_ref, gcpost_ref, y_ref):
    aw = ao_ref.shape[1]
    mix = (jnp.dot(ao_ref[...], wout_ref[:aw, :], preferred_element_type=F32)
           + jnp.dot(do_ref[...], wout_ref[aw:, :], preferred_element_type=F32))
    x1 = x_ref[...] + _rms(mix, gpost_ref[...])
    hc = _rms(x1, gpre_ref[...]).astype(BF16)
    d = x1.shape[1]
    chd = d // CROSS_HEADS
    q = (jnp.dot(hc, wq_ref[...], preferred_element_type=F32) * (chd ** -0.5)).astype(BF16)
    heads = []
    for h in range(CROSS_HEADS):
        hs = slice(h * chd, (h + 1) * chd)
        s = lax.dot_general(q[:, hs], km_ref[:, hs], (((1,), (1,)), ((), ())), preferred_element_type=F32)
        p = jnp.exp(s - jnp.max(s, axis=-1, keepdims=True))
        o = jnp.dot(p.astype(BF16), vm_ref[:, hs], preferred_element_type=F32)
        heads.append((o / jnp.sum(p, axis=-1, keepdims=True)).astype(BF16))
    cproj = jnp.dot(jnp.concatenate(heads, axis=1), wo_ref[...], preferred_element_type=F32)
    y_ref[...] = x1 + _rms(cproj, gcpost_ref[...])


def _out_cross(x2, attn_o, dn_o, w_out, g_post, g_pre, w_cq, kmem, vmem, w_co, g_cpost, seq, mem_len, tm):
    t, d = x2.shape
    per_batch = seq // tm
    row = lambda w: pl.BlockSpec((tm, w), lambda i: (i, 0))
    memspec = pl.BlockSpec((mem_len, d), lambda i: (i // per_batch, 0))
    vec = _const_spec((1, d))
    return pl.pallas_call(
        _out_cross_kernel,
        out_shape=jax.ShapeDtypeStruct((t, d), F32),
        grid=(t // tm,),
        in_specs=[row(d), row(attn_o.shape[1]), row(dn_o.shape[1]), _const_spec(w_out.shape), vec, vec,
                  _const_spec(w_cq.shape), memspec, memspec, _const_spec(w_co.shape), vec],
        out_specs=row(d),
        compiler_params=_cparams(1),
        name="out_cross",
    )(x2, attn_o, dn_o, w_out, g_post, g_pre, w_cq, kmem, vmem, w_co, g_cpost)


def _ffn_kernel(x_ref, gpre_ref, wgu_ref, wdown_ref, gpost_ref, y_ref):
    x = x_ref[...]
    dff = wdown_ref.shape[0]
    hf = _rms(x, gpre_ref[...]).astype(BF16)
    gu = jnp.dot(hf, wgu_ref[...], preferred_element_type=F32)
    act = (_silu(gu[:, :dff]) * gu[:, dff:]).astype(BF16)
    f = jnp.dot(act, wdown_ref[...], preferred_element_type=F32)
    y_ref[...] = x + _rms(f, gpost_ref[...])


def _ffn(x2, g_pre, w_gu, w_down, g_post, tm):
    t, d = x2.shape
    row = pl.BlockSpec((tm, d), lambda i: (i, 0))
    vec = _const_spec((1, d))
    return pl.pallas_call(
        _ffn_kernel,
        out_shape=jax.ShapeDtypeStruct((t, d), F32),
        grid=(t // tm,),
        in_specs=[row, vec, _const_spec(w_gu.shape), _const_spec(w_down.shape), vec],
        out_specs=row,
        compiler_params=_cparams(1),
        name="ffn",
    )(x2, g_pre, w_gu, w_down, g_post)


def _layer(x, mem, positions, g_mix_pre, w_in, conv_w, a_log, dt_bias, g_dn_out, attn_sink, w_out,
           g_mix_post, g_cross_pre, g_mem, w_cq, w_ckv, w_co, g_cross_post, g_ffn_pre, w_gate_up,
           w_down, g_ffn_post):
    batch, seq, d = x.shape
    mem_len = mem.shape[1]
    t = batch * seq
    x2 = x.reshape(t, d)
    vec = lambda g: g.reshape(1, -1).astype(F32)

    half = ATTN_HEAD_DIM // 2
    inv_freq = ROPE_THETA ** (-jnp.arange(half, dtype=F32) / half)
    invf_row = jnp.tile(inv_freq, LANES // half).reshape(1, LANES)

    n_ab = 2 * N_DIRS * DN_HEADS
    n_main = w_in.shape[1] - n_ab
    w_main = w_in[:, :n_main].astype(BF16)
    w_abt = w_in[:, n_main:].T.astype(BF16)
    q, k, v, dqkv, dgate, abr = _in_proj(x2, positions.reshape(t, 1), invf_row, vec(g_mix_pre),
                                              w_main, w_abt, tm=512)

    attn_o = _win_attn(attn_sink.astype(F32), q, k, v, batch, seq, nq=min(4, seq // ATTN_BLOCK))

    pad_c = lambda p: jnp.pad(p.reshape(-1, 1).astype(F32), ((0, n_ab - N_DIRS * DN_HEADS), (0, 0)))
    dn_o = _deltanet(dqkv, dgate, abr, conv_w.astype(F32), pad_c(a_log), pad_c(dt_bias), vec(g_dn_out), batch, seq)

    kmem, vmem = _mem_kv(mem.reshape(batch * mem_len, d), vec(g_mem), w_ckv.astype(BF16), mem_len)
    x3 = _out_cross(x2, attn_o, dn_o, w_out.astype(BF16), vec(g_mix_post), vec(g_cross_pre),
                    w_cq.astype(BF16), kmem, vmem, w_co.astype(BF16), vec(g_cross_post), seq, mem_len, tm=512)
    y = _ffn(x3, vec(g_ffn_pre), w_gate_up.astype(BF16), w_down.astype(BF16), vec(g_ffn_post), tm=256)
    return y.reshape(batch, seq, d)


def kernel(x, mem, positions, g_mix_pre, w_in, conv_w, a_log, dt_bias, g_dn_out, attn_sink, w_out,
           g_mix_post, g_cross_pre, g_mem, w_cq, w_ckv, w_co, g_cross_post, g_ffn_pre, w_gate_up,
           w_down, g_ffn_post):
    depth = w_in.shape[0]
    for l in range(depth):
        x = _layer(x, mem, positions, g_mix_pre[l], w_in[l], conv_w[l], a_log[l], dt_bias[l], g_dn_out[l],
                   attn_sink[l], w_out[l], g_mix_post[l], g_cross_pre[l], g_mem[l], w_cq[l], w_ckv[l],
                   w_co[l], g_cross_post[l], g_ffn_pre[l], w_gate_up[l], w_down[l], g_ffn_post[l])
    return x
```

```python
import functools

import jax
import jax.numpy as jnp
from jax import lax
from jax.experimental import pallas as pl
from jax.experimental.pallas import tpu as pltpu

F32 = jnp.float32
BF16 = jnp.bfloat16

EPS = 1e-6
ROPE_THETA = 10000.0
ATTN_HEADS = 8
ATTN_KV_HEADS = 2
ATTN_HEAD_DIM = 64
ATTN_BLOCK = 128
DN_HEADS = 4
DN_HEAD_DIM = 128
DN_CONV = 5
N_DIRS = 2
CROSS_HEADS = 4

LANES = 128
DN_CHUNK = 128
ROW_SUB = 256
TRI_BASE = 16
CONV_HALO = 8
VMEM_LIMIT = 56 * 1024 * 1024


def _cparams(n_grid_dims):
    return pltpu.CompilerParams(
        dimension_semantics=("arbitrary",) * n_grid_dims,
        vmem_limit_bytes=VMEM_LIMIT,
    )


def _rms(x, gain):
    return x * lax.rsqrt(jnp.mean(x * x, axis=-1, keepdims=True) + EPS) * gain


def _silu(x):
    return x * jax.nn.sigmoid(x)


def _softplus(x):
    return jnp.maximum(x, 0.0) + jnp.log(1.0 + jnp.exp(-jnp.abs(x)))


def _const_spec(shape):
    nd = len(shape)
    return pl.BlockSpec(shape, lambda *_: (0,) * nd, pipeline_mode=pl.Buffered(1))


def _in_proj_kernel(x_ref, pos_ref, invf_ref, gain_ref, w_ref, wabt_ref,
                    q_ref, k_ref, v_ref, dqkv_ref, gate_ref, abr_ref):
    half = ATTN_HEAD_DIM // 2
    lane = lax.broadcasted_iota(jnp.int32, (1, LANES), 1)
    first_half = (lane % ATTN_HEAD_DIM) < half
    aw = ATTN_HEADS * ATTN_HEAD_DIM
    kvw = ATTN_KV_HEADS * ATTN_HEAD_DIM
    off = aw + 2 * kvw
    dn = dqkv_ref.shape[1]
    per_sub = ROW_SUB // DN_CHUNK

    def rows_chain(r0):
        rs = slice(r0, r0 + ROW_SUB)
        h = _rms(x_ref[rs, :], gain_ref[...]).astype(BF16)
        proj = lambda lo, hi: jnp.dot(h, w_ref[:, lo:hi], preferred_element_type=F32)
        q = proj(0, aw)
        kv = proj(aw, off)
        dqkv = proj(off, off + dn)
        gate = proj(off + dn, off + dn + gate_ref.shape[1])
        abr = lax.dot_general(wabt_ref[...], h, (((1,), (1,)), ((), ())), preferred_element_type=F32)
        yield
        ang = pos_ref[rs, :].astype(F32) * invf_ref[...]
        cos = jnp.cos(ang)
        sin = jnp.sin(ang)

        def rope(t):
            rot = jnp.where(first_half, -pltpu.roll(t, LANES - half, 1), pltpu.roll(t, half, 1))
            return t * cos + rot * sin

        for m in range(aw // LANES):
            cols = slice(m * LANES, (m + 1) * LANES)
            q_ref[rs, cols] = (rope(q[:, cols]) * (ATTN_HEAD_DIM ** -0.5)).astype(BF16)
        k_ref[rs, :] = rope(kv[:, :kvw]).astype(BF16)
        v_ref[rs, :] = kv[:, kvw:].astype(BF16)
        dqkv_ref[rs, :] = dqkv.astype(BF16)
        gate_ref[rs, :] = gate.astype(BF16)
        for j in range(per_sub):
            abr_ref[r0 // DN_CHUNK + j] = abr[:, j * DN_CHUNK:(j + 1) * DN_CHUNK]

    _lockstep([rows_chain(r0) for r0 in range(0, x_ref.shape[0], ROW_SUB)])


def _in_proj(x2, pos_col, invf_row, gain, w_main, w_abt, tm):
    t, d = x2.shape
    aw = ATTN_HEADS * ATTN_HEAD_DIM
    kvw = ATTN_KV_HEADS * ATTN_HEAD_DIM
    dnw = DN_HEADS * DN_HEAD_DIM
    n_ab = w_abt.shape[0]
    row = lambda w: pl.BlockSpec((tm, w), lambda i: (i, 0))
    out_shape = (
        jax.ShapeDtypeStruct((t, aw), BF16),
        jax.ShapeDtypeStruct((t, kvw), BF16),
        jax.ShapeDtypeStruct((t, kvw), BF16),
        jax.ShapeDtypeStruct((t, 3 * dnw), BF16),
        jax.ShapeDtypeStruct((t, dnw), BF16),
        jax.ShapeDtypeStruct((t // DN_CHUNK, n_ab, DN_CHUNK), F32),
    )
    out_specs = (
        row(aw), row(kvw), row(kvw), row(3 * dnw), row(dnw),
        pl.BlockSpec((tm // DN_CHUNK, n_ab, DN_CHUNK), lambda i: (i, 0, 0)),
    )
    return pl.pallas_call(
        _in_proj_kernel,
        out_shape=out_shape,
        grid=(t // tm,),
        in_specs=[row(d), row(1), _const_spec((1, LANES)), _const_spec((1, d)), _const_spec(w_main.shape),
                  _const_spec(w_abt.shape)],
        out_specs=out_specs,
        compiler_params=_cparams(1),
        name="in_proj",
    )(x2, pos_col, invf_row, gain, w_main, w_abt)


def _paired_dots(reqs):
    out = [None] * len(reqs)
    todo = [i for i, r in enumerate(reqs) if r is not None]
    while todo:
        i = todo.pop(0)
        l1, r1 = reqs[i]
        j = next((t for t in todo if reqs[t][0].shape == l1.shape and reqs[t][1].shape == r1.shape), None)
        if j is None or r1.shape[1] != LANES:
            out[i] = jnp.dot(l1, r1, preferred_element_type=F32)
            continue
        todo.remove(j)
        l2, r2 = reqs[j]
        z = jnp.zeros_like(r1)
        rhs = jnp.concatenate([jnp.concatenate([r1, z], axis=1), jnp.concatenate([z, r2], axis=1)], axis=0)
        both = jnp.dot(jnp.concatenate([l1, l2], axis=1), rhs, preferred_element_type=F32)
        out[i] = both[:, :LANES]
        out[j] = both[:, LANES:]
    return out


def _lockstep(gens):
    live = list(gens)
    sends = [None] * len(live)
    while live:
        reqs, still = [], []
        for g, val in zip(live, sends):
            try:
                reqs.append(g.send(val))
                still.append(g)
            except StopIteration:
                pass
        live = still
        sends = _paired_dots(reqs)


def _win_attn_kernel(sink_ref, q_ref, kp_ref, kc_ref, kn_ref, vp_ref, vc_ref, vn_ref, o_ref):
    n = pl.program_id(1)
    ns = pl.num_programs(1)
    blk = ATTN_BLOCK
    hd = ATTN_HEAD_DIM
    nq = q_ref.shape[0] // blk
    group = ATTN_HEADS // ATTN_KV_HEADS
    lane = lax.broadcasted_iota(jnp.int32, (1, LANES), 1)
    lo = lane < hd

    def halves(x):
        swapped = pltpu.roll(x.astype(F32), hd, 1).astype(BF16)
        z = jnp.zeros_like(x)
        return ((jnp.where(lo, x, z), jnp.where(lo, z, swapped)),
                (jnp.where(lo, swapped, z), jnp.where(lo, z, x)))

    kh = [halves(kp_ref[...])] + [halves(kc_ref[i * blk:(i + 1) * blk, :]) for i in range(nq)] + [halves(kn_ref[...])]
    vh = [halves(vp_ref[...])] + [halves(vc_ref[i * blk:(i + 1) * blk, :]) for i in range(nq)] + [halves(vn_ref[...])]

    def blockdiag(parts, i, g):
        return jnp.concatenate([parts[i + w][g][e] for e in range(2) for w in range(3)], axis=0)

    qi = lax.broadcasted_iota(jnp.int32, (blk, blk), 0)
    ki = lax.broadcasted_iota(jnp.int32, (blk, blk), 1)
    ninf = jnp.float32(-jnp.inf)
    zer = jnp.zeros((blk, blk), F32)

    def band_bias(i):
        has_prev = jnp.where(n > 0, 0.0, ninf) if i == 0 else jnp.float32(0.0)
        has_next = jnp.where(n < ns - 1, 0.0, ninf) if i == nq - 1 else jnp.float32(0.0)
        return jnp.concatenate([jnp.where(ki >= qi, has_prev, ninf), zer,
                                jnp.where(ki <= qi, has_next, ninf)], axis=1)

    def chain(i, m, kbd, vbd, bias):
        rows = slice(i * blk, (i + 1) * blk)
        cols = slice(m * LANES, (m + 1) * LANES)
        s = lax.dot_general(q_ref[rows, cols], kbd, (((1,), (1,)), ((), ())), preferred_element_type=F32)
        yield
        ps, ds = [], []
        for e in range(2):
            sk = sink_ref[2 * m + e]
            se = s[:, e * 3 * blk:(e + 1) * 3 * blk] + bias
            mx = jnp.maximum(jnp.max(se, axis=-1, keepdims=True), sk)
            p = jnp.exp(se - mx)
            ds.append(jnp.sum(p, axis=-1, keepdims=True) + jnp.exp(sk - mx))
            ps.append(p.astype(BF16))
        o = jnp.dot(jnp.concatenate(ps, axis=1), vbd, preferred_element_type=F32)
        yield
        o_ref[rows, cols] = (o / jnp.where(lo, ds[0], ds[1])).astype(BF16)

    chains = []
    for i in range(nq):
        bias = band_bias(i)
        for g in range(ATTN_KV_HEADS):
            kbd = blockdiag(kh, i, g)
            vbd = blockdiag(vh, i, g)
            for m in range(g * group // 2, (g + 1) * group // 2):
                chains.append(chain(i, m, kbd, vbd, bias))
    _lockstep(chains)


def _win_attn(sink, q, k, v, batch, seq, nq):
    nb = seq // ATTN_BLOCK
    ns = nb // nq
    aw = q.shape[1]
    kvw = k.shape[1]
    cur = lambda b, n: (b * ns + n, 0)
    prev = lambda b, n: (b * nb + jnp.maximum(n * nq - 1, 0), 0)
    nxt = lambda b, n: (b * nb + jnp.minimum(n * nq + nq, nb - 1), 0)
    edge = lambda f: pl.BlockSpec((ATTN_BLOCK, kvw), f)
    own = pl.BlockSpec((nq * ATTN_BLOCK, kvw), cur)
    return pl.pallas_call(
        _win_attn_kernel,
        out_shape=jax.ShapeDtypeStruct(q.shape, BF16),
        grid=(batch, ns),
        in_specs=[pl.BlockSpec(memory_space=pltpu.SMEM),
                  pl.BlockSpec((nq * ATTN_BLOCK, aw), cur),
                  edge(prev), own, edge(nxt), edge(prev), own, edge(nxt)],
        out_specs=pl.BlockSpec((nq * ATTN_BLOCK, aw), cur),
        compiler_params=_cparams(2),
        name="win_attn",
    )(sink, q, k, k, k, v, v, v)


def _cumsum_both(x, axis):
    n = x.shape[axis]
    idx = lax.broadcasted_iota(jnp.int32, x.shape, axis)
    pre, suf = x, x
    s = 1
    while s < n:
        pre = pre + jnp.where(idx >= s, pltpu.roll(pre, s, axis), 0.0)
        suf = suf + jnp.where(idx < n - s, pltpu.roll(suf, n - s, axis), 0.0)
        s *= 2
    return pre, suf


def _unit_tri_inverse(nmat, rc_xor, lower):
    c = nmat.shape[0]
    n0 = jnp.where(rc_xor < TRI_BASE, nmat, 0.0)
    nb16 = n0.astype(BF16)
    p = yield (nb16, nb16)
    y = n0
    n_sq = (TRI_BASE - 1).bit_length() - 1
    for m in range(n_sq):
        pb = p.astype(BF16)
        if m < n_sq - 1:
            r2 = jnp.dot(pb, jnp.concatenate([y.astype(BF16), pb], axis=1), preferred_element_type=F32)
            yield
            y = y + p + r2[:, :c]
            p = r2[:, c:]
        else:
            py = yield (pb, y.astype(BF16))
            y = y + p + py
    neg_n = -nmat
    blk = TRI_BASE
    while blk < c:
        first = blk if lower else 0
        held = [slice(s, s + blk) for s in range(first, c, 2 * blk)]
        kept = [slice(s, s + blk) for s in range(blk - first, c, 2 * blk)]
        take = lambda a: jnp.concatenate([a[s] for s in held], axis=0)
        rx = take(rc_xor)
        l_rows = jnp.where((rx >= blk) & (rx < 2 * blk), take(neg_n), 0.0)
        y_rows = take(y)
        ly = yield (l_rows.astype(BF16), y.astype(BF16))
        u_rows = l_rows + ly
        ub = u_rows.astype(BF16)
        zero = jnp.zeros((blk, c), BF16)
        pieces = [ub[i * blk:(i + 1) * blk] for i in range(len(held))]
        order = [zero, None] if lower else [None, zero]
        u_full = jnp.concatenate([(pieces[i] if o is None else o) for i in range(len(held)) for o in order], axis=0)
        yu = yield (y_rows.astype(BF16), u_full)
        new_rows = y_rows - u_rows - yu
        parts = {}
        for i, s in enumerate(held):
            parts[s.start] = new_rows[i * blk:(i + 1) * blk]
        for s in kept:
            parts[s.start] = y[s]
        y = jnp.concatenate([parts[k] for k in sorted(parts)], axis=0)
        blk *= 2
    return y


def _deltanet_kernel(dq_ref, dk_ref, dv_ref, gate_ref, abr_ref, cw_ref, alog_ref, dtb_ref, gout_ref,
                     o_ref, xp, qn, kn, vn, kt, oacc, st, pm1, pm2, pkte, pgl, gcum, gbeta, gcols, gsum):
    seq = dq_ref.shape[0]
    nc = seq // DN_CHUNK
    c = DN_CHUNK
    hd = DN_HEAD_DIM
    width = DN_HEADS * hd

    xp[0:CONV_HALO, :] = jnp.zeros((CONV_HALO, width), F32)
    xp[CONV_HALO + seq:2 * CONV_HALO + seq, :] = jnp.zeros((CONV_HALO, width), F32)

    def conv_phase(src_ref, w_off, dst, l2, scale):
        def stage(i, carry):
            r = pl.multiple_of(i * c, c)
            xp[pl.ds(CONV_HALO + r, c), :] = src_ref[pl.ds(r, c), :].astype(F32)
            return carry
        lax.fori_loop(0, nc, stage, 0)

        def body(i, carry):
            r = pl.multiple_of(i * c, c)
            for h in range(DN_HEADS):
                xa = xp[pl.ds(r, c + 2 * CONV_HALO), h * hd:(h + 1) * hd]
                y = jnp.zeros((c, hd), F32)
                for j in range(DN_CONV):
                    a = CONV_HALO - DN_CONV // 2 + j
                    wj = cw_ref[j:j + 1, w_off + h * hd:w_off + (h + 1) * hd]
                    y = y + wj * xa[a:a + c, :]
                y = _silu(y)
                if l2:
                    y = y * lax.rsqrt(jnp.sum(y * y, axis=-1, keepdims=True) + EPS)
                    if scale != 1.0:
                        y = y * scale
                dst[i, :, h * hd:(h + 1) * hd] = y.astype(BF16)
                if dst is kn:
                    kt[i, h * hd:(h + 1) * hd, :] = y.T.astype(BF16)
            return carry
        lax.fori_loop(0, nc, body, 0)

    conv_phase(dq_ref, 0, qn, True, hd ** -0.5)
    conv_phase(dk_ref, width, kn, True, 1.0)
    conv_phase(dv_ref, 2 * width, vn, False, 1.0)

    st[...] = jnp.zeros(st.shape, F32)
    row = lax.broadcasted_iota(jnp.int32, (c, c), 0)
    col = lax.broadcasted_iota(jnp.int32, (c, c), 1)
    eye = row == col
    masks = ((row >= col, row > col), (row <= col, row < col))
    nh = DN_HEADS

    n_g = N_DIRS * nh
    n_r = abr_ref.shape[1]
    ab = jnp.concatenate([abr_ref[ci] for ci in range(nc)], axis=0)
    g = -jnp.exp(jnp.tile(alog_ref[...], (nc, 1))) * _softplus(ab + jnp.tile(dtb_ref[...], (nc, 1)))
    beta = jax.nn.sigmoid(ab)
    pre, suf = _cumsum_both(g, 1)
    cum = jnp.where(lax.broadcasted_iota(jnp.int32, g.shape, 0) % n_r < nh, pre, suf)
    gtot_all = jnp.broadcast_to(jnp.sum(g, axis=1, keepdims=True), g.shape)
    for ci in range(nc):
        rows = slice(ci * n_r, (ci + 1) * n_r)
        gcum[ci] = cum[rows]
        gbeta[ci] = beta[rows]
        gsum[ci] = gtot_all[rows]
        packed = jnp.concatenate([cum[ci * n_r:ci * n_r + n_g], -beta[ci * n_r + n_g:(ci + 1) * n_r],
                                  jnp.zeros((c - 2 * n_g, c), F32)], axis=0)
        gcols[ci] = packed.T

    def prep(ci, h, d, slot):
        j = d * nh + h
        jb = N_DIRS * nh + j
        gc_c = gcols[ci, :, j:j + 1]
        gc_r = gcum[ci, j:j + 1, :]
        neg_b_c = gcols[ci, :, jb:jb + 1]
        b_r = gbeta[ci, jb:jb + 1, :]
        gtot = gsum[ci, j:j + 1, 0:1]
        incl, strict = masks[d]
        hs = slice(h * hd, (h + 1) * hd)
        ktc = kt[ci, hs, :]
        qk2 = jnp.concatenate([qn[ci, :, hs], kn[ci, :, hs]], axis=0)
        gram = yield (qk2, ktc)
        decay = jnp.exp(jnp.where(incl, gc_c - gc_r, -jnp.inf))
        nmat = jnp.where(strict, gram[c:] * neg_b_c * decay, 0.0)
        y = yield from _unit_tri_inverse(nmat, row ^ col, lower=(d == 0))
        tinv = y + jnp.where(eye, 1.0, 0.0)
        eg_r = jnp.exp(gc_r)
        pm1[slot] = jnp.concatenate([tinv * b_r, tinv * -(b_r * eg_r)], axis=1).astype(BF16)
        pm2[slot] = jnp.concatenate([jnp.where(eye, eg_r, 0.0), gram[:c] * decay], axis=1).astype(BF16)
        pkte[slot] = (ktc.astype(F32) * jnp.exp(gtot - gc_r)).astype(BF16)
        pgl[slot] = jnp.broadcast_to(jnp.exp(gtot), pgl.shape[1:])

    def scan(ci, h, d, slot):
        j = d * nh + h
        hs = slice(h * hd, (h + 1) * hd)
        qk2 = jnp.concatenate([qn[ci, :, hs], kn[ci, :, hs]], axis=0)
        pm = yield (qk2, st[j].astype(BF16))
        v_new = jnp.dot(pm1[slot], jnp.concatenate([vn[ci, :, hs], pm[c:].astype(BF16)], axis=0),
                        preferred_element_type=F32)
        yield
        vb = v_new.astype(BF16)
        o = jnp.dot(pm2[slot], jnp.concatenate([pm[:c].astype(BF16), vb], axis=0), preferred_element_type=F32)
        ds = yield (pkte[slot], vb)
        st[j] = st[j] * pgl[slot][0:1, :] + ds
        oacc[ci, :, hs] = oacc[ci, :, hs] + o

    def chunk_of(step_idx, d):
        return step_idx if d == 0 else nc - 1 - step_idx

    def preps(step_idx, parity):
        return [prep(chunk_of(step_idx, d), h, d, parity * N_DIRS * nh + d * nh + h)
                for d in range(N_DIRS) for h in range(nh)]

    def scans(step_a, step_b):
        def both(h, d):
            yield from scan(chunk_of(step_a, d), h, d, d * nh + h)
            yield from scan(chunk_of(step_b, d), h, d, N_DIRS * nh + d * nh + h)
        return [both(h, d) for d in range(N_DIRS) for h in range(nh)]

    oacc[...] = jnp.zeros(oacc.shape, F32)
    _lockstep(preps(0, 0) + preps(1, 1))

    def pair_of_steps(i, carry):
        s0 = 2 * i
        nxt0 = jnp.minimum(s0 + 2, nc - 2)
        _lockstep(scans(s0, s0 + 1))
        _lockstep(preps(nxt0, 0) + preps(nxt0 + 1, 1))
        return carry

    lax.fori_loop(0, nc // 2, pair_of_steps, 0)

    def finish(ci, carry):
        r = pl.multiple_of(ci * c, c)
        for h in range(nh):
            hs = slice(h * hd, (h + 1) * hd)
            gate = gate_ref[pl.ds(r, c), hs].astype(F32)
            o_ref[pl.ds(r, c), hs] = (_rms(oacc[ci, :, hs], gout_ref[...]) * _silu(gate)).astype(BF16)
        return carry

    lax.fori_loop(0, nc, finish, 0)


def _deltanet(dqkv, gate, abr, conv_w, alog_c, dtb_c, gout, batch, seq):
    width = DN_HEADS * DN_HEAD_DIM
    nc = seq // DN_CHUNK
    n_ab = abr.shape[1]
    col = lambda j: pl.BlockSpec((seq, width), lambda b, j=j: (b, j))
    return pl.pallas_call(
        _deltanet_kernel,
        out_shape=jax.ShapeDtypeStruct((batch * seq, width), BF16),
        grid=(batch,),
        in_specs=[col(0), col(1), col(2), col(0),
                  pl.BlockSpec((nc, n_ab, DN_CHUNK), lambda b: (b, 0, 0)),
                  _const_spec(conv_w.shape),
                  _const_spec(alog_c.shape), _const_spec(dtb_c.shape),
                  _const_spec(gout.shape)],
        out_specs=col(0),
        scratch_shapes=[
            pltpu.VMEM((seq + 2 * CONV_HALO, width), F32),
            pltpu.VMEM((nc, DN_CHUNK, width), BF16),
            pltpu.VMEM((nc, DN_CHUNK, width), BF16),
            pltpu.VMEM((nc, DN_CHUNK, width), BF16),
            pltpu.VMEM((nc, width, DN_CHUNK), BF16),
            pltpu.VMEM((nc, DN_CHUNK, width), F32),
            pltpu.VMEM((N_DIRS * DN_HEADS, DN_HEAD_DIM, DN_HEAD_DIM), F32),
            pltpu.VMEM((2 * N_DIRS * DN_HEADS, DN_CHUNK, 2 * DN_CHUNK), BF16),
            pltpu.VMEM((2 * N_DIRS * DN_HEADS, DN_CHUNK, 2 * DN_CHUNK), BF16),
            pltpu.VMEM((2 * N_DIRS * DN_HEADS, DN_HEAD_DIM, DN_CHUNK), BF16),
            pltpu.VMEM((2 * N_DIRS * DN_HEADS, 8, LANES), F32),
            pltpu.VMEM((nc, n_ab, DN_CHUNK), F32),
            pltpu.VMEM((nc, n_ab, DN_CHUNK), F32),
            pltpu.VMEM((nc, DN_CHUNK, LANES), F32),
            pltpu.VMEM((nc, n_ab, DN_CHUNK), F32),
        ],
        compiler_params=_cparams(1),
        name="deltanet",
    )(dqkv, dqkv, dqkv, gate, abr, conv_w, alog_c, dtb_c, gout)


def _mem_kv_kernel(mem_ref, gain_ref, w_ref, k_ref, v_ref):
    d = k_ref.shape[1]
    mn = _rms(mem_ref[...], gain_ref[...]).astype(BF16)
    k_ref[...] = jnp.dot(mn, w_ref[:, :d], preferred_element_type=F32).astype(BF16)
    v_ref[...] = jnp.dot(mn, w_ref[:, d:], preferred_element_type=F32).astype(BF16)


def _mem_kv(mem2, gain, w_ckv, mem_len):
    rows, d = mem2.shape
    spec = pl.BlockSpec((mem_len, d), lambda b: (b, 0))
    return pl.pallas_call(
        _mem_kv_kernel,
        out_shape=(jax.ShapeDtypeStruct((rows, d), BF16),) * 2,
        grid=(rows // mem_len,),
        in_specs=[spec, _const_spec((1, d)), _const_spec(w_ckv.shape)],
        out_specs=(spec, spec),
        compiler_params=_cparams(1),
        name="mem_kv",
    )(mem2, gain, w_ckv)


def _out_cross_kernel(x_ref, ao_ref, do_ref, wout_ref, gpost_ref, gpre_ref, wq_ref, km_ref, vm_ref,
                      wo_ref, gcpost_ref, y_ref):
    aw = ao_ref.shape[1]
    d = x_ref.shape[1]
    chd = d // CROSS_HEADS
    nt = (((1,), (1,)), ((), ()))

    def rows_chain(r0):
        rs = slice(r0, r0 + ROW_SUB)
        mix = (jnp.dot(ao_ref[rs, :], wout_ref[:aw, :], preferred_element_type=F32)
               + jnp.dot(do_ref[rs, :], wout_ref[aw:, :], preferred_element_type=F32))
        yield
        x1 = x_ref[rs, :] + _rms(mix, gpost_ref[...])
        q = jnp.dot(_rms(x1, gpre_ref[...]).astype(BF16), wq_ref[...], preferred_element_type=F32)
        yield
        q = (q * (chd ** -0.5)).astype(BF16)
        hcols = [slice(h * chd, (h + 1) * chd) for h in range(CROSS_HEADS)]
        scores = [lax.dot_general(q[:, hs], km_ref[:, hs], nt, preferred_element_type=F32) for hs in hcols]
        yield
        probs = [jnp.exp(s - jnp.max(s, axis=-1, keepdims=True)) for s in scores]
        outs = [jnp.dot(p.astype(BF16), vm_ref[:, hs], preferred_element_type=F32) for p, hs in zip(probs, hcols)]
        yield
        heads = [(o / jnp.sum(p, axis=-1, keepdims=True)).astype(BF16) for o, p in zip(outs, probs)]
        cproj = jnp.dot(jnp.concatenate(heads, axis=1), wo_ref[...], preferred_element_type=F32)
        yield
        y_ref[rs, :] = x1 + _rms(cproj, gcpost_ref[...])

    _lockstep([rows_chain(r0) for r0 in range(0, x_ref.shape[0], ROW_SUB)])


def _out_cross(x2, attn_o, dn_o, w_out, g_post, g_pre, w_cq, kmem, vmem, w_co, g_cpost, seq, mem_len, tm):
    t, d = x2.shape
    per_batch = seq // tm
    row = lambda w: pl.BlockSpec((tm, w), lambda i: (i, 0))
    memspec = pl.BlockSpec((mem_len, d), lambda i: (i // per_batch, 0))
    vec = _const_spec((1, d))
    return pl.pallas_call(
        _out_cross_kernel,
        out_shape=jax.ShapeDtypeStruct((t, d), F32),
        grid=(t // tm,),
        in_specs=[row(d), row(attn_o.shape[1]), row(dn_o.shape[1]), _const_spec(w_out.shape), vec, vec,
                  _const_spec(w_cq.shape), memspec, memspec, _const_spec(w_co.shape), vec],
        out_specs=row(d),
        compiler_params=_cparams(1),
        name="out_cross",
    )(x2, attn_o, dn_o, w_out, g_post, g_pre, w_cq, kmem, vmem, w_co, g_cpost)


def _ffn_kernel(x_ref, gpre_ref, wgu_ref, wdown_ref, gpost_ref, y_ref):
    dff = wdown_ref.shape[0]

    def rows_chain(r0):
        rs = slice(r0, r0 + ROW_SUB)
        hf = _rms(x_ref[rs, :], gpre_ref[...]).astype(BF16)
        gu = jnp.dot(hf, wgu_ref[...], preferred_element_type=F32)
        yield
        act = (_silu(gu[:, :dff]) * gu[:, dff:]).astype(BF16)
        f = jnp.dot(act, wdown_ref[...], preferred_element_type=F32)
        yield
        y_ref[rs, :] = x_ref[rs, :] + _rms(f, gpost_ref[...])

    _lockstep([rows_chain(r0) for r0 in range(0, x_ref.shape[0], ROW_SUB)])


def _ffn(x2, g_pre, w_gu, w_down, g_post, tm):
    t, d = x2.shape
    row = pl.BlockSpec((tm, d), lambda i: (i, 0))
    vec = _const_spec((1, d))
    return pl.pallas_call(
        _ffn_kernel,
        out_shape=jax.ShapeDtypeStruct((t, d), F32),
        grid=(t // tm,),
        in_specs=[row, vec, _const_spec(w_gu.shape), _const_spec(w_down.shape), vec],
        out_specs=row,
        compiler_params=_cparams(1),
        name="ffn",
    )(x2, g_pre, w_gu, w_down, g_post)


def _layer(x, mem, positions, g_mix_pre, w_in, conv_w, a_log, dt_bias, g_dn_out, attn_sink, w_out,
           g_mix_post, g_cross_pre, g_mem, w_cq, w_ckv, w_co, g_cross_post, g_ffn_pre, w_gate_up,
           w_down, g_ffn_post):
    batch, seq, d = x.shape
    mem_len = mem.shape[1]
    t = batch * seq
    x2 = x.reshape(t, d)
    vec = lambda g: g.reshape(1, -1).astype(F32)

    half = ATTN_HEAD_DIM // 2
    inv_freq = ROPE_THETA ** (-jnp.arange(half, dtype=F32) / half)
    invf_row = jnp.tile(inv_freq, LANES // half).reshape(1, LANES)

    n_ab = 2 * N_DIRS * DN_HEADS
    n_main = w_in.shape[1] - n_ab
    w_main = w_in[:, :n_main].astype(BF16)
    w_abt = w_in[:, n_main:].T.astype(BF16)
    q, k, v, dqkv, dgate, abr = _in_proj(x2, positions.reshape(t, 1), invf_row, vec(g_mix_pre),
                                              w_main, w_abt, tm=min(1024, t))

    attn_o = _win_attn(attn_sink.astype(F32), q, k, v, batch, seq, nq=min(4, seq // ATTN_BLOCK))

    pad_c = lambda p: jnp.pad(p.reshape(-1, 1).astype(F32), ((0, n_ab - N_DIRS * DN_HEADS), (0, 0)))
    dn_o = _deltanet(dqkv, dgate, abr, conv_w.astype(F32), pad_c(a_log), pad_c(dt_bias), vec(g_dn_out), batch, seq)

    kmem, vmem = _mem_kv(mem.reshape(batch * mem_len, d), vec(g_mem), w_ckv.astype(BF16), mem_len)
    x3 = _out_cross(x2, attn_o, dn_o, w_out.astype(BF16), vec(g_mix_post), vec(g_cross_pre),
                    w_cq.astype(BF16), kmem, vmem, w_co.astype(BF16), vec(g_cross_post), seq, mem_len, tm=min(1024, seq))
    y = _ffn(x3, vec(g_ffn_pre), w_gate_up.astype(BF16), w_down.astype(BF16), vec(g_ffn_post), tm=512)
    return y.reshape(batch, seq, d)


def kernel(x, mem, positions, g_mix_pre, w_in, conv_w, a_log, dt_bias, g_dn_out, attn_sink, w_out,
           g_mix_post, g_cross_pre, g_mem, w_cq, w_ckv, w_co, g_cross_post, g_ffn_pre, w_gate_up,
           w_down, g_ffn_post):
    depth = w_in.shape[0]
    for l in range(depth):
        x = _layer(x, mem, positions, g_mix_pre[l], w_in[l], conv_w[l], a_log[l], dt_bias[l], g_dn_out[l],
                   attn_sink[l], w_out[l], g_mix_post[l], g_cross_pre[l], g_mem[l], w_cq[l], w_ckv[l],
                   w_co[l], g_cross_post[l], g_ffn_pre[l], w_gate_up[l], w_down[l], g_ffn_post[l])
    return x
```

```python
import functools

import jax
import jax.numpy as jnp
from jax import lax
from jax.experimental import pallas as pl
from jax.experimental.pallas import tpu as pltpu

F32 = jnp.float32
BF16 = jnp.bfloat16

EPS = 1e-6
ROPE_THETA = 10000.0
ATTN_HEADS = 8
ATTN_KV_HEADS = 2
ATTN_HEAD_DIM = 64
ATTN_BLOCK = 128
DN_HEADS = 4
DN_HEAD_DIM = 128
DN_CONV = 5
N_DIRS = 2
CROSS_HEADS = 4

LANES = 128
DN_CHUNK = 128
ROW_SUB = 256
DN_GROUP = 2
TRI_BASE = 16
VMEM_LIMIT = 56 * 1024 * 1024


def _cparams(n_grid_dims):
    return pltpu.CompilerParams(
        dimension_semantics=("arbitrary",) * n_grid_dims,
        vmem_limit_bytes=VMEM_LIMIT,
    )


def _rms(x, gain):
    return x * lax.rsqrt(jnp.mean(x * x, axis=-1, keepdims=True) + EPS) * gain


def _silu(x):
    return x * jax.nn.sigmoid(x)


def _softplus(x):
    return jnp.maximum(x, 0.0) + jnp.log(1.0 + jnp.exp(-jnp.abs(x)))


def _const_spec(shape):
    nd = len(shape)
    return pl.BlockSpec(shape, lambda *_: (0,) * nd, pipeline_mode=pl.Buffered(1))


def _in_proj_kernel(x_ref, pos_ref, invf_ref, gain_ref, w_ref, wabt_ref,
                    q_ref, k_ref, v_ref, dqkv_ref, gate_ref, abr_ref):
    half = ATTN_HEAD_DIM // 2
    lane = lax.broadcasted_iota(jnp.int32, (1, LANES), 1)
    first_half = (lane % ATTN_HEAD_DIM) < half
    aw = ATTN_HEADS * ATTN_HEAD_DIM
    kvw = ATTN_KV_HEADS * ATTN_HEAD_DIM
    off = aw + 2 * kvw
    dn = dqkv_ref.shape[1]
    per_sub = ROW_SUB // DN_CHUNK

    def rows_chain(r0):
        rs = slice(r0, r0 + ROW_SUB)
        h = _rms(x_ref[rs, :], gain_ref[...]).astype(BF16)
        proj = lambda lo, hi: jnp.dot(h, w_ref[:, lo:hi], preferred_element_type=F32)
        q = proj(0, aw)
        kv = proj(aw, off)
        dqkv = proj(off, off + dn)
        gate = proj(off + dn, off + dn + gate_ref.shape[1])
        abr = lax.dot_general(wabt_ref[...], h, (((1,), (1,)), ((), ())), preferred_element_type=F32)
        yield
        ang = pos_ref[rs, :].astype(F32) * invf_ref[...]
        cos = jnp.cos(ang)
        sin = jnp.sin(ang)

        def rope(t):
            rot = jnp.where(first_half, -pltpu.roll(t, LANES - half, 1), pltpu.roll(t, half, 1))
            return t * cos + rot * sin

        for m in range(aw // LANES):
            cols = slice(m * LANES, (m + 1) * LANES)
            q_ref[rs, cols] = (rope(q[:, cols]) * (ATTN_HEAD_DIM ** -0.5)).astype(BF16)
        k_ref[rs, :] = rope(kv[:, :kvw]).astype(BF16)
        v_ref[rs, :] = kv[:, kvw:].astype(BF16)
        dqkv_ref[rs, :] = dqkv.astype(BF16)
        gate_ref[rs, :] = gate.astype(BF16)
        for j in range(per_sub):
            abr_ref[r0 // DN_CHUNK + j] = abr[:, j * DN_CHUNK:(j + 1) * DN_CHUNK]

    _lockstep([rows_chain(r0) for r0 in range(0, x_ref.shape[0], ROW_SUB)])


def _in_proj(x2, pos_col, invf_row, gain, w_main, w_abt, tm):
    t, d = x2.shape
    aw = ATTN_HEADS * ATTN_HEAD_DIM
    kvw = ATTN_KV_HEADS * ATTN_HEAD_DIM
    dnw = DN_HEADS * DN_HEAD_DIM
    n_ab = w_abt.shape[0]
    row = lambda w: pl.BlockSpec((tm, w), lambda i: (i, 0))
    out_shape = (
        jax.ShapeDtypeStruct((t, aw), BF16),
        jax.ShapeDtypeStruct((t, kvw), BF16),
        jax.ShapeDtypeStruct((t, kvw), BF16),
        jax.ShapeDtypeStruct((t, 3 * dnw), BF16),
        jax.ShapeDtypeStruct((t, dnw), BF16),
        jax.ShapeDtypeStruct((t // DN_CHUNK, n_ab, DN_CHUNK), F32),
    )
    out_specs = (
        row(aw), row(kvw), row(kvw), row(3 * dnw), row(dnw),
        pl.BlockSpec((tm // DN_CHUNK, n_ab, DN_CHUNK), lambda i: (i, 0, 0)),
    )
    return pl.pallas_call(
        _in_proj_kernel,
        out_shape=out_shape,
        grid=(t // tm,),
        in_specs=[row(d), row(1), _const_spec((1, LANES)), _const_spec((1, d)), _const_spec(w_main.shape),
                  _const_spec(w_abt.shape)],
        out_specs=out_specs,
        compiler_params=_cparams(1),
        name="in_proj",
    )(x2, pos_col, invf_row, gain, w_main, w_abt)


def _paired_dots(reqs):
    out = [None] * len(reqs)
    todo = [i for i, r in enumerate(reqs) if r is not None]
    while todo:
        i = todo.pop(0)
        l1, r1 = reqs[i]
        j = next((t for t in todo if reqs[t][0].shape == l1.shape and reqs[t][1].shape == r1.shape), None)
        if j is None or r1.shape[1] != LANES:
            out[i] = jnp.dot(l1, r1, preferred_element_type=F32)
            continue
        todo.remove(j)
        l2, r2 = reqs[j]
        z = jnp.zeros_like(r1)
        rhs = jnp.concatenate([jnp.concatenate([r1, z], axis=1), jnp.concatenate([z, r2], axis=1)], axis=0)
        both = jnp.dot(jnp.concatenate([l1, l2], axis=1), rhs, preferred_element_type=F32)
        out[i] = both[:, :LANES]
        out[j] = both[:, LANES:]
    return out


def _lockstep(gens):
    live = list(gens)
    sends = [None] * len(live)
    while live:
        reqs, still = [], []
        for g, val in zip(live, sends):
            try:
                reqs.append(g.send(val))
                still.append(g)
            except StopIteration:
                pass
        live = still
        sends = _paired_dots(reqs)


def _win_attn_kernel(sink_ref, q_ref, kp_ref, kc_ref, kn_ref, vp_ref, vc_ref, vn_ref, o_ref):
    n = pl.program_id(1)
    ns = pl.num_programs(1)
    blk = ATTN_BLOCK
    hd = ATTN_HEAD_DIM
    nq = q_ref.shape[0] // blk
    group = ATTN_HEADS // ATTN_KV_HEADS
    lane = lax.broadcasted_iota(jnp.int32, (1, LANES), 1)
    lo = lane < hd

    def halves(x):
        swapped = pltpu.roll(x.astype(F32), hd, 1).astype(BF16)
        z = jnp.zeros_like(x)
        return ((jnp.where(lo, x, z), jnp.where(lo, z, swapped)),
                (jnp.where(lo, swapped, z), jnp.where(lo, z, x)))

    kh = [halves(kp_ref[...])] + [halves(kc_ref[i * blk:(i + 1) * blk, :]) for i in range(nq)] + [halves(kn_ref[...])]
    vh = [halves(vp_ref[...])] + [halves(vc_ref[i * blk:(i + 1) * blk, :]) for i in range(nq)] + [halves(vn_ref[...])]

    def blockdiag(parts, i, g):
        return jnp.concatenate([parts[i + w][g][e] for e in range(2) for w in range(3)], axis=0)

    qi = lax.broadcasted_iota(jnp.int32, (blk, blk), 0)
    ki = lax.broadcasted_iota(jnp.int32, (blk, blk), 1)
    ninf = jnp.float32(-jnp.inf)
    zer = jnp.zeros((blk, blk), F32)

    def band_bias(i):
        has_prev = jnp.where(n > 0, 0.0, ninf) if i == 0 else jnp.float32(0.0)
        has_next = jnp.where(n < ns - 1, 0.0, ninf) if i == nq - 1 else jnp.float32(0.0)
        return jnp.concatenate([jnp.where(ki >= qi, has_prev, ninf), zer,
                                jnp.where(ki <= qi, has_next, ninf)], axis=1)

    def chain(i, m, kbd, vbd, bias):
        rows = slice(i * blk, (i + 1) * blk)
        cols = slice(m * LANES, (m + 1) * LANES)
        s = lax.dot_general(q_ref[rows, cols], kbd, (((1,), (1,)), ((), ())), preferred_element_type=F32)
        yield
        ps, ds = [], []
        for e in range(2):
            sk = sink_ref[2 * m + e]
            se = s[:, e * 3 * blk:(e + 1) * 3 * blk] + bias
            mx = jnp.maximum(jnp.max(se, axis=-1, keepdims=True), sk)
            p = jnp.exp(se - mx)
            ds.append(jnp.sum(p, axis=-1, keepdims=True) + jnp.exp(sk - mx))
            ps.append(p.astype(BF16))
        o = jnp.dot(jnp.concatenate(ps, axis=1), vbd, preferred_element_type=F32)
        yield
        o_ref[rows, cols] = (o / jnp.where(lo, ds[0], ds[1])).astype(BF16)

    chains = []
    for i in range(nq):
        bias = band_bias(i)
        for g in range(ATTN_KV_HEADS):
            kbd = blockdiag(kh, i, g)
            vbd = blockdiag(vh, i, g)
            for m in range(g * group // 2, (g + 1) * group // 2):
                chains.append(chain(i, m, kbd, vbd, bias))
    _lockstep(chains)


def _win_attn(sink, q, k, v, batch, seq, nq):
    nb = seq // ATTN_BLOCK
    ns = nb // nq
    aw = q.shape[1]
    kvw = k.shape[1]
    cur = lambda b, n: (b * ns + n, 0)
    prev = lambda b, n: (b * nb + jnp.maximum(n * nq - 1, 0), 0)
    nxt = lambda b, n: (b * nb + jnp.minimum(n * nq + nq, nb - 1), 0)
    edge = lambda f: pl.BlockSpec((ATTN_BLOCK, kvw), f)
    own = pl.BlockSpec((nq * ATTN_BLOCK, kvw), cur)
    return pl.pallas_call(
        _win_attn_kernel,
        out_shape=jax.ShapeDtypeStruct(q.shape, BF16),
        grid=(batch, ns),
        in_specs=[pl.BlockSpec(memory_space=pltpu.SMEM),
                  pl.BlockSpec((nq * ATTN_BLOCK, aw), cur),
                  edge(prev), own, edge(nxt), edge(prev), own, edge(nxt)],
        out_specs=pl.BlockSpec((nq * ATTN_BLOCK, aw), cur),
        compiler_params=_cparams(2),
        name="win_attn",
    )(sink, q, k, k, k, v, v, v)


def _cumsum_both(x, axis):
    n = x.shape[axis]
    idx = lax.broadcasted_iota(jnp.int32, x.shape, axis)
    pre, suf = x, x
    s = 1
    while s < n:
        pre = pre + jnp.where(idx >= s, pltpu.roll(pre, s, axis), 0.0)
        suf = suf + jnp.where(idx < n - s, pltpu.roll(suf, n - s, axis), 0.0)
        s *= 2
    return pre, suf


def _unit_tri_inverse(nmat, rc_xor, lower):
    c = nmat.shape[0]
    n0 = jnp.where(rc_xor < TRI_BASE, nmat, 0.0)
    nb16 = n0.astype(BF16)
    p = yield (nb16, nb16)
    y = n0
    n_sq = (TRI_BASE - 1).bit_length() - 1
    for m in range(n_sq):
        pb = p.astype(BF16)
        if m < n_sq - 1:
            r2 = jnp.dot(pb, jnp.concatenate([y.astype(BF16), pb], axis=1), preferred_element_type=F32)
            yield
            y = y + p + r2[:, :c]
            p = r2[:, c:]
        else:
            py = yield (pb, y.astype(BF16))
            y = y + p + py
    neg_n = -nmat
    blk = TRI_BASE
    while blk < c:
        first = blk if lower else 0
        held = [slice(s, s + blk) for s in range(first, c, 2 * blk)]
        kept = [slice(s, s + blk) for s in range(blk - first, c, 2 * blk)]
        take = lambda a: jnp.concatenate([a[s] for s in held], axis=0)
        rx = take(rc_xor)
        l_rows = jnp.where((rx >= blk) & (rx < 2 * blk), take(neg_n), 0.0)
        y_rows = take(y)
        ly = yield (l_rows.astype(BF16), y.astype(BF16))
        u_rows = l_rows + ly
        ub = u_rows.astype(BF16)
        zero = jnp.zeros((blk, c), BF16)
        pieces = [ub[i * blk:(i + 1) * blk] for i in range(len(held))]
        order = [zero, None] if lower else [None, zero]
        u_full = jnp.concatenate([(pieces[i] if o is None else o) for i in range(len(held)) for o in order], axis=0)
        yu = yield (y_rows.astype(BF16), u_full)
        new_rows = y_rows - u_rows - yu
        parts = {}
        for i, s in enumerate(held):
            parts[s.start] = new_rows[i * blk:(i + 1) * blk]
        for s in kept:
            parts[s.start] = y[s]
        y = jnp.concatenate([parts[k] for k in sorted(parts)], axis=0)
        blk *= 2
    return y


def _deltanet_kernel(dq_ref, dk_ref, dv_ref, gate_ref, abr_ref, cw_ref, alog_ref, dtb_ref, gout_ref,
                     o_ref, shifts, qn, kn, vn, kt, oacc, st, pm1, pm2, pkte, pgl, gcum, gbeta, gcols, gsum):
    seq = dq_ref.shape[0]
    nc = seq // DN_CHUNK
    c = DN_CHUNK
    hd = DN_HEAD_DIM
    width = DN_HEADS * hd

    win = shifts.shape[2]
    pad = DN_CONV // 2
    lead = (win - c) // 2
    side_taps = [j for j in range(DN_CONV) if j != pad]

    @pl.when(pl.program_id(0) == 0)
    def _():
        tok = lax.broadcasted_iota(jnp.int32, (c, win), 0)
        src = lax.broadcasted_iota(jnp.int32, (c, win), 1)
        for v in range(shifts.shape[0]):
            shifts[v] = jnp.concatenate(
                [jnp.where(src == tok + (j - pad + v * lead), 1.0, 0.0).astype(BF16) for j in side_taps], axis=0)

    def conv_chunk(i):
        r = pl.multiple_of(i * c, c)
        start = pl.multiple_of(jnp.clip(r - lead, 0, seq - win), lead)
        sel = shifts[(r - start) // lead]

        def one(src_ref, w_off, dst, l2, scale):
            taps = jnp.dot(sel, src_ref[pl.ds(start, win), :], preferred_element_type=F32)
            yield
            for h in range(DN_HEADS):
                hs = slice(h * hd, (h + 1) * hd)
                wrow = lambda j: cw_ref[j:j + 1, w_off + h * hd:w_off + (h + 1) * hd]
                y = wrow(pad) * src_ref[pl.ds(r, c), hs].astype(F32)
                for n, j in enumerate(side_taps):
                    y = y + wrow(j) * taps[n * c:(n + 1) * c, hs]
                y = _silu(y)
                if l2:
                    y = y * lax.rsqrt(jnp.sum(y * y, axis=-1, keepdims=True) + EPS)
                    if scale != 1.0:
                        y = y * scale
                dst[i, :, hs] = y.astype(BF16)
                if dst is kn:
                    kt[i, hs, :] = y.T.astype(BF16)

        return [one(dq_ref, 0, qn, True, hd ** -0.5), one(dk_ref, width, kn, True, 1.0),
                one(dv_ref, 2 * width, vn, False, 1.0)]

    def conv_pair(i, carry):
        _lockstep(conv_chunk(2 * i) + conv_chunk(2 * i + 1))
        return carry

    lax.fori_loop(0, nc // 2, conv_pair, 0)

    st[...] = jnp.zeros(st.shape, F32)
    row = lax.broadcasted_iota(jnp.int32, (c, c), 0)
    col = lax.broadcasted_iota(jnp.int32, (c, c), 1)
    eye = row == col
    masks = ((row >= col, row > col), (row <= col, row < col))
    nh = DN_HEADS

    n_g = N_DIRS * nh
    n_r = abr_ref.shape[1]
    ab = jnp.concatenate([abr_ref[ci] for ci in range(nc)], axis=0)
    g = -jnp.exp(jnp.tile(alog_ref[...], (nc, 1))) * _softplus(ab + jnp.tile(dtb_ref[...], (nc, 1)))
    beta = jax.nn.sigmoid(ab)
    pre, suf = _cumsum_both(g, 1)
    cum = jnp.where(lax.broadcasted_iota(jnp.int32, g.shape, 0) % n_r < nh, pre, suf)
    gtot_all = jnp.broadcast_to(jnp.sum(g, axis=1, keepdims=True), g.shape)
    for ci in range(nc):
        rows = slice(ci * n_r, (ci + 1) * n_r)
        gcum[ci] = cum[rows]
        gbeta[ci] = beta[rows]
        gsum[ci] = gtot_all[rows]
        packed = jnp.concatenate([cum[ci * n_r:ci * n_r + n_g], -beta[ci * n_r + n_g:(ci + 1) * n_r],
                                  jnp.zeros((c - 2 * n_g, c), F32)], axis=0)
        gcols[ci] = packed.T

    def prep(ci, h, d, slot):
        j = d * nh + h
        jb = N_DIRS * nh + j
        gc_c = gcols[ci, :, j:j + 1]
        gc_r = gcum[ci, j:j + 1, :]
        neg_b_c = gcols[ci, :, jb:jb + 1]
        b_r = gbeta[ci, jb:jb + 1, :]
        gtot = gsum[ci, j:j + 1, 0:1]
        incl, strict = masks[d]
        hs = slice(h * hd, (h + 1) * hd)
        ktc = kt[ci, hs, :]
        qk2 = jnp.concatenate([qn[ci, :, hs], kn[ci, :, hs]], axis=0)
        gram = yield (qk2, ktc)
        decay = jnp.exp(jnp.where(incl, gc_c - gc_r, -jnp.inf))
        nmat = jnp.where(strict, gram[c:] * neg_b_c * decay, 0.0)
        y = yield from _unit_tri_inverse(nmat, row ^ col, lower=(d == 0))
        tinv = y + jnp.where(eye, 1.0, 0.0)
        eg_r = jnp.exp(gc_r)
        pm1[slot] = jnp.concatenate([tinv * b_r, tinv * -(b_r * eg_r)], axis=1).astype(BF16)
        pm2[slot] = jnp.concatenate([jnp.where(eye, eg_r, 0.0), gram[:c] * decay], axis=1).astype(BF16)
        pkte[slot] = (ktc.astype(F32) * jnp.exp(gtot - gc_r)).astype(BF16)
        pgl[slot] = jnp.broadcast_to(jnp.exp(gtot), pgl.shape[1:])

    def scan(ci, h, d, slot):
        j = d * nh + h
        hs = slice(h * hd, (h + 1) * hd)
        qk2 = jnp.concatenate([qn[ci, :, hs], kn[ci, :, hs]], axis=0)
        pm = yield (qk2, st[j].astype(BF16))
        v_new = jnp.dot(pm1[slot], jnp.concatenate([vn[ci, :, hs], pm[c:].astype(BF16)], axis=0),
                        preferred_element_type=F32)
        yield
        vb = v_new.astype(BF16)
        o = jnp.dot(pm2[slot], jnp.concatenate([pm[:c].astype(BF16), vb], axis=0), preferred_element_type=F32)
        ds = yield (pkte[slot], vb)
        st[j] = st[j] * pgl[slot][0:1, :] + ds
        oacc[ci, :, hs] = oacc[ci, :, hs] + o

    def chunk_of(step_idx, d):
        return step_idx if d == 0 else nc - 1 - step_idx

    n_chain = N_DIRS * nh
    group = pm1.shape[0] // n_chain

    def preps(step0):
        return [prep(chunk_of(step0 + g, d), h, d, g * n_chain + d * nh + h)
                for g in range(group) for d in range(N_DIRS) for h in range(nh)]

    def scans(step0):
        def run(h, d):
            for g in range(group):
                yield from scan(chunk_of(step0 + g, d), h, d, g * n_chain + d * nh + h)
        return [run(h, d) for d in range(N_DIRS) for h in range(nh)]

    oacc[...] = jnp.zeros(oacc.shape, F32)
    _lockstep(preps(0))

    def group_of_steps(i, carry):
        s0 = i * group
        _lockstep(scans(s0))
        _lockstep(preps(s0 + group))
        return carry

    lax.fori_loop(0, nc // group - 1, group_of_steps, 0)
    _lockstep(scans(nc - group))

    def finish(ci, carry):
        r = pl.multiple_of(ci * c, c)
        for h in range(nh):
            hs = slice(h * hd, (h + 1) * hd)
            gate = gate_ref[pl.ds(r, c), hs].astype(F32)
            o_ref[pl.ds(r, c), hs] = (_rms(oacc[ci, :, hs], gout_ref[...]) * _silu(gate)).astype(BF16)
        return carry

    lax.fori_loop(0, nc, finish, 0)


def _deltanet(dqkv, gate, abr, conv_w, alog_c, dtb_c, gout, batch, seq):
    width = DN_HEADS * DN_HEAD_DIM
    nc = seq // DN_CHUNK
    n_ab = abr.shape[1]
    col = lambda j: pl.BlockSpec((seq, width), lambda b, j=j: (b, j))
    return pl.pallas_call(
        _deltanet_kernel,
        out_shape=jax.ShapeDtypeStruct((batch * seq, width), BF16),
        grid=(batch,),
        in_specs=[col(0), col(1), col(2), col(0),
                  pl.BlockSpec((nc, n_ab, DN_CHUNK), lambda b: (b, 0, 0)),
                  _const_spec(conv_w.shape),
                  _const_spec(alog_c.shape), _const_spec(dtb_c.shape),
                  _const_spec(gout.shape)],
        out_specs=col(0),
        scratch_shapes=[
            pltpu.VMEM((3, (DN_CONV - 1) * DN_CHUNK, 2 * DN_CHUNK), BF16),
            pltpu.VMEM((nc, DN_CHUNK, width), BF16),
            pltpu.VMEM((nc, DN_CHUNK, width), BF16),
            pltpu.VMEM((nc, DN_CHUNK, width), BF16),
            pltpu.VMEM((nc, width, DN_CHUNK), BF16),
            pltpu.VMEM((nc, DN_CHUNK, width), F32),
            pltpu.VMEM((N_DIRS * DN_HEADS, DN_HEAD_DIM, DN_HEAD_DIM), F32),
            pltpu.VMEM((DN_GROUP * N_DIRS * DN_HEADS, DN_CHUNK, 2 * DN_CHUNK), BF16),
            pltpu.VMEM((DN_GROUP * N_DIRS * DN_HEADS, DN_CHUNK, 2 * DN_CHUNK), BF16),
            pltpu.VMEM((DN_GROUP * N_DIRS * DN_HEADS, DN_HEAD_DIM, DN_CHUNK), BF16),
            pltpu.VMEM((DN_GROUP * N_DIRS * DN_HEADS, 8, LANES), F32),
            pltpu.VMEM((nc, n_ab, DN_CHUNK), F32),
            pltpu.VMEM((nc, n_ab, DN_CHUNK), F32),
            pltpu.VMEM((nc, DN_CHUNK, LANES), F32),
            pltpu.VMEM((nc, n_ab, DN_CHUNK), F32),
        ],
        compiler_params=_cparams(1),
        name="deltanet",
    )(dqkv, dqkv, dqkv, gate, abr, conv_w, alog_c, dtb_c, gout)


def _mem_kv_kernel(mem_ref, gain_ref, w_ref, k_ref, v_ref):
    d = k_ref.shape[1]
    mn = _rms(mem_ref[...], gain_ref[...]).astype(BF16)
    k_ref[...] = jnp.dot(mn, w_ref[:, :d], preferred_element_type=F32).astype(BF16)
    v_ref[...] = jnp.dot(mn, w_ref[:, d:], preferred_element_type=F32).astype(BF16)


def _mem_kv(mem2, gain, w_ckv, mem_len):
    rows, d = mem2.shape
    spec = pl.BlockSpec((mem_len, d), lambda b: (b, 0))
    return pl.pallas_call(
        _mem_kv_kernel,
        out_shape=(jax.ShapeDtypeStruct((rows, d), BF16),) * 2,
        grid=(rows // mem_len,),
        in_specs=[spec, _const_spec((1, d)), _const_spec(w_ckv.shape)],
        out_specs=(spec, spec),
        compiler_params=_cparams(1),
        name="mem_kv",
    )(mem2, gain, w_ckv)


def _out_cross_kernel(x_ref, ao_ref, do_ref, wout_ref, gpost_ref, gpre_ref, wq_ref, km_ref, vm_ref,
                      wo_ref, gcpost_ref, y_ref):
    aw = ao_ref.shape[1]
    d = x_ref.shape[1]
    chd = d // CROSS_HEADS
    nt = (((1,), (1,)), ((), ()))

    def rows_chain(r0):
        rs = slice(r0, r0 + ROW_SUB)
        mix = (jnp.dot(ao_ref[rs, :], wout_ref[:aw, :], preferred_element_type=F32)
               + jnp.dot(do_ref[rs, :], wout_ref[aw:, :], preferred_element_type=F32))
        yield
        x1 = x_ref[rs, :] + _rms(mix, gpost_ref[...])
        q = jnp.dot(_rms(x1, gpre_ref[...]).astype(BF16), wq_ref[...], preferred_element_type=F32)
        yield
        q = (q * (chd ** -0.5)).astype(BF16)
        hcols = [slice(h * chd, (h + 1) * chd) for h in range(CROSS_HEADS)]
        scores = [lax.dot_general(q[:, hs], km_ref[:, hs], nt, preferred_element_type=F32) for hs in hcols]
        yield
        probs = [jnp.exp(s - jnp.max(s, axis=-1, keepdims=True)) for s in scores]
        outs = [jnp.dot(p.astype(BF16), vm_ref[:, hs], preferred_element_type=F32) for p, hs in zip(probs, hcols)]
        yield
        heads = [(o / jnp.sum(p, axis=-1, keepdims=True)).astype(BF16) for o, p in zip(outs, probs)]
        cproj = jnp.dot(jnp.concatenate(heads, axis=1), wo_ref[...], preferred_element_type=F32)
        yield
        y_ref[rs, :] = x1 + _rms(cproj, gcpost_ref[...])

    _lockstep([rows_chain(r0) for r0 in range(0, x_ref.shape[0], ROW_SUB)])


def _out_cross(x2, attn_o, dn_o, w_out, g_post, g_pre, w_cq, kmem, vmem, w_co, g_cpost, seq, mem_len, tm):
    t, d = x2.shape
    per_batch = seq // tm
    row = lambda w: pl.BlockSpec((tm, w), lambda i: (i, 0))
    memspec = pl.BlockSpec((mem_len, d), lambda i: (i // per_batch, 0))
    vec = _const_spec((1, d))
    return pl.pallas_call(
        _out_cross_kernel,
        out_shape=jax.ShapeDtypeStruct((t, d), F32),
        grid=(t // tm,),
        in_specs=[row(d), row(attn_o.shape[1]), row(dn_o.shape[1]), _const_spec(w_out.shape), vec, vec,
                  _const_spec(w_cq.shape), memspec, memspec, _const_spec(w_co.shape), vec],
        out_specs=row(d),
        compiler_params=_cparams(1),
        name="out_cross",
    )(x2, attn_o, dn_o, w_out, g_post, g_pre, w_cq, kmem, vmem, w_co, g_cpost)


def _ffn_kernel(x_ref, gpre_ref, wgu_ref, wdown_ref, gpost_ref, y_ref):
    dff = wdown_ref.shape[0]

    def rows_chain(r0):
        rs = slice(r0, r0 + ROW_SUB)
        hf = _rms(x_ref[rs, :], gpre_ref[...]).astype(BF16)
        gu = jnp.dot(hf, wgu_ref[...], preferred_element_type=F32)
        yield
        act = (_silu(gu[:, :dff]) * gu[:, dff:]).astype(BF16)
        f = jnp.dot(act, wdown_ref[...], preferred_element_type=F32)
        yield
        y_ref[rs, :] = x_ref[rs, :] + _rms(f, gpost_ref[...])

    _lockstep([rows_chain(r0) for r0 in range(0, x_ref.shape[0], ROW_SUB)])


def _ffn(x2, g_pre, w_gu, w_down, g_post, tm):
    t, d = x2.shape
    row = pl.BlockSpec((tm, d), lambda i: (i, 0))
    vec = _const_spec((1, d))
    return pl.pallas_call(
        _ffn_kernel,
        out_shape=jax.ShapeDtypeStruct((t, d), F32),
        grid=(t // tm,),
        in_specs=[row, vec, _const_spec(w_gu.shape), _const_spec(w_down.shape), vec],
        out_specs=row,
        compiler_params=_cparams(1),
        name="ffn",
    )(x2, g_pre, w_gu, w_down, g_post)


def _layer(x, mem, positions, g_mix_pre, w_in, conv_w, a_log, dt_bias, g_dn_out, attn_sink, w_out,
           g_mix_post, g_cross_pre, g_mem, w_cq, w_ckv, w_co, g_cross_post, g_ffn_pre, w_gate_up,
           w_down, g_ffn_post):
    batch, seq, d = x.shape
    mem_len = mem.shape[1]
    t = batch * seq
    x2 = x.reshape(t, d)
    vec = lambda g: g.reshape(1, -1).astype(F32)

    half = ATTN_HEAD_DIM // 2
    inv_freq = ROPE_THETA ** (-jnp.arange(half, dtype=F32) / half)
    invf_row = jnp.tile(inv_freq, LANES // half).reshape(1, LANES)

    n_ab = 2 * N_DIRS * DN_HEADS
    n_main = w_in.shape[1] - n_ab
    w_main = w_in[:, :n_main].astype(BF16)
    w_abt = w_in[:, n_main:].T.astype(BF16)
    q, k, v, dqkv, dgate, abr = _in_proj(x2, positions.reshape(t, 1), invf_row, vec(g_mix_pre),
                                              w_main, w_abt, tm=min(1024, t))

    attn_o = _win_attn(attn_sink.astype(F32), q, k, v, batch, seq, nq=min(4, seq // ATTN_BLOCK))

    pad_c = lambda p: jnp.pad(p.reshape(-1, 1).astype(F32), ((0, n_ab - N_DIRS * DN_HEADS), (0, 0)))
    dn_o = _deltanet(dqkv, dgate, abr, conv_w.astype(F32), pad_c(a_log), pad_c(dt_bias), vec(g_dn_out), batch, seq)

    kmem, vmem = _mem_kv(mem.reshape(batch * mem_len, d), vec(g_mem), w_ckv.astype(BF16), mem_len)
    x3 = _out_cross(x2, attn_o, dn_o, w_out.astype(BF16), vec(g_mix_post), vec(g_cross_pre),
                    w_cq.astype(BF16), kmem, vmem, w_co.astype(BF16), vec(g_cross_post), seq, mem_len, tm=min(1024, seq))
    y = _ffn(x3, vec(g_ffn_pre), w_gate_up.astype(BF16), w_down.astype(BF16), vec(g_ffn_post), tm=512)
    return y.reshape(batch, seq, d)


def kernel(x, mem, positions, g_mix_pre, w_in, conv_w, a_log, dt_bias, g_dn_out, attn_sink, w_out,
           g_mix_post, g_cross_pre, g_mem, w_cq, w_ckv, w_co, g_cross_post, g_ffn_pre, w_gate_up,
           w_down, g_ffn_post):
    depth = w_in.shape[0]
    for l in range(depth):
        x = _layer(x, mem, positions, g_mix_pre[l], w_in[l], conv_w[l], a_log[l], dt_bias[l], g_dn_out[l],
                   attn_sink[l], w_out[l], g_mix_post[l], g_cross_pre[l], g_mem[l], w_cq[l], w_ckv[l],
                   w_co[l], g_cross_post[l], g_ffn_pre[l], w_gate_up[l], w_down[l], g_ffn_post[l])
    return x
```

```python
import functools

import jax
import jax.numpy as jnp
from jax import lax
from jax.experimental import pallas as pl
from jax.experimental.pallas import tpu as pltpu

F32 = jnp.float32
BF16 = jnp.bfloat16

EPS = 1e-6
ROPE_THETA = 10000.0
ATTN_HEADS = 8
ATTN_KV_HEADS = 2
ATTN_HEAD_DIM = 64
ATTN_BLOCK = 128
DN_HEADS = 4
DN_HEAD_DIM = 128
DN_CONV = 5
N_DIRS = 2
CROSS_HEADS = 4

LANES = 128
DN_CHUNK = 128
ROW_SUB = 256
DN_GROUP = 2
TRI_BASE = 16
VMEM_LIMIT = 56 * 1024 * 1024


def _cparams(n_grid_dims):
    return pltpu.CompilerParams(
        dimension_semantics=("arbitrary",) * n_grid_dims,
        vmem_limit_bytes=VMEM_LIMIT,
    )


def _rms(x, gain):
    return x * lax.rsqrt(jnp.mean(x * x, axis=-1, keepdims=True) + EPS) * gain


def _silu(x):
    return x * jax.nn.sigmoid(x)


def _softplus(x):
    return jnp.maximum(x, 0.0) + jnp.log(1.0 + jnp.exp(-jnp.abs(x)))


def _const_spec(shape):
    nd = len(shape)
    return pl.BlockSpec(shape, lambda *_: (0,) * nd, pipeline_mode=pl.Buffered(1))


def _in_proj_kernel(x_ref, pos_ref, invf_ref, gain_ref, w_ref, wabt_ref,
                    q_ref, k_ref, v_ref, dqkv_ref, gate_ref, abr_ref):
    half = ATTN_HEAD_DIM // 2
    lane = lax.broadcasted_iota(jnp.int32, (1, LANES), 1)
    first_half = (lane % ATTN_HEAD_DIM) < half
    aw = ATTN_HEADS * ATTN_HEAD_DIM
    kvw = ATTN_KV_HEADS * ATTN_HEAD_DIM
    off = aw + 2 * kvw
    dn = dqkv_ref.shape[1]
    per_sub = ROW_SUB // DN_CHUNK

    def rows_chain(r0):
        rs = slice(r0, r0 + ROW_SUB)
        h = _rms(x_ref[rs, :], gain_ref[...]).astype(BF16)
        proj = lambda lo, hi: jnp.dot(h, w_ref[:, lo:hi], preferred_element_type=F32)
        q = proj(0, aw)
        kv = proj(aw, off)
        dqkv = proj(off, off + dn)
        gate = proj(off + dn, off + dn + gate_ref.shape[1])
        abr = lax.dot_general(wabt_ref[...], h, (((1,), (1,)), ((), ())), preferred_element_type=F32)
        yield
        pos = jnp.concatenate(
            [jnp.broadcast_to(pos_ref[g:g + 1, :].astype(F32), (LANES, LANES)).T
             for g in range(r0 // LANES, (r0 + ROW_SUB) // LANES)], axis=0)
        ang = pos * invf_ref[...]
        cos = jnp.cos(ang)
        sin = jnp.sin(ang)

        def rope(t):
            rot = jnp.where(first_half, -pltpu.roll(t, LANES - half, 1), pltpu.roll(t, half, 1))
            return t * cos + rot * sin

        for m in range(aw // LANES):
            cols = slice(m * LANES, (m + 1) * LANES)
            q_ref[rs, cols] = (rope(q[:, cols]) * (ATTN_HEAD_DIM ** -0.5)).astype(BF16)
        k_ref[rs, :] = rope(kv[:, :kvw]).astype(BF16)
        v_ref[rs, :] = kv[:, kvw:].astype(BF16)
        dqkv_ref[rs, :] = dqkv.astype(BF16)
        gate_ref[rs, :] = gate.astype(BF16)
        for j in range(per_sub):
            abr_ref[r0 // DN_CHUNK + j] = abr[:, j * DN_CHUNK:(j + 1) * DN_CHUNK]

    _lockstep([rows_chain(r0) for r0 in range(0, x_ref.shape[0], ROW_SUB)])


def _in_proj(x2, pos_rows, invf_row, gain, w_main, w_abt, tm):
    t, d = x2.shape
    aw = ATTN_HEADS * ATTN_HEAD_DIM
    kvw = ATTN_KV_HEADS * ATTN_HEAD_DIM
    dnw = DN_HEADS * DN_HEAD_DIM
    n_ab = w_abt.shape[0]
    row = lambda w: pl.BlockSpec((tm, w), lambda i: (i, 0))
    out_shape = (
        jax.ShapeDtypeStruct((t, aw), BF16),
        jax.ShapeDtypeStruct((t, kvw), BF16),
        jax.ShapeDtypeStruct((t, kvw), BF16),
        jax.ShapeDtypeStruct((t, 3 * dnw), BF16),
        jax.ShapeDtypeStruct((t, dnw), BF16),
        jax.ShapeDtypeStruct((t // DN_CHUNK, n_ab, DN_CHUNK), F32),
    )
    out_specs = (
        row(aw), row(kvw), row(kvw), row(3 * dnw), row(dnw),
        pl.BlockSpec((tm // DN_CHUNK, n_ab, DN_CHUNK), lambda i: (i, 0, 0)),
    )
    return pl.pallas_call(
        _in_proj_kernel,
        out_shape=out_shape,
        grid=(t // tm,),
        in_specs=[row(d), pl.BlockSpec((tm // LANES, LANES), lambda i: (i, 0)), _const_spec((1, LANES)),
                  _const_spec((1, d)), _const_spec(w_main.shape), _const_spec(w_abt.shape)],
        out_specs=out_specs,
        compiler_params=_cparams(1),
        name="in_proj",
    )(x2, pos_rows, invf_row, gain, w_main, w_abt)


def _paired_dots(reqs):
    out = [None] * len(reqs)
    todo = [i for i, r in enumerate(reqs) if r is not None]
    while todo:
        i = todo.pop(0)
        l1, r1 = reqs[i]
        j = next((t for t in todo if reqs[t][0].shape == l1.shape and reqs[t][1].shape == r1.shape), None)
        if j is None or r1.shape[1] != LANES:
            out[i] = jnp.dot(l1, r1, preferred_element_type=F32)
            continue
        todo.remove(j)
        l2, r2 = reqs[j]
        z = jnp.zeros_like(r1)
        rhs = jnp.concatenate([jnp.concatenate([r1, z], axis=1), jnp.concatenate([z, r2], axis=1)], axis=0)
        both = jnp.dot(jnp.concatenate([l1, l2], axis=1), rhs, preferred_element_type=F32)
        out[i] = both[:, :LANES]
        out[j] = both[:, LANES:]
    return out


def _lockstep(gens):
    live = list(gens)
    sends = [None] * len(live)
    while live:
        reqs, still = [], []
        for g, val in zip(live, sends):
            try:
                reqs.append(g.send(val))
                still.append(g)
            except StopIteration:
                pass
        live = still
        sends = _paired_dots(reqs)


def _win_attn_kernel(sink_ref, q_ref, kp_ref, kc_ref, kn_ref, vp_ref, vc_ref, vn_ref, o_ref):
    n = pl.program_id(1)
    ns = pl.num_programs(1)
    blk = ATTN_BLOCK
    hd = ATTN_HEAD_DIM
    nq = q_ref.shape[0] // blk
    group = ATTN_HEADS // ATTN_KV_HEADS
    lane = lax.broadcasted_iota(jnp.int32, (1, LANES), 1)
    lo = lane < hd

    def halves(x):
        swapped = pltpu.roll(x.astype(F32), hd, 1).astype(BF16)
        z = jnp.zeros_like(x)
        return ((jnp.where(lo, x, z), jnp.where(lo, z, swapped)),
                (jnp.where(lo, swapped, z), jnp.where(lo, z, x)))

    kh = [halves(kp_ref[...])] + [halves(kc_ref[i * blk:(i + 1) * blk, :]) for i in range(nq)] + [halves(kn_ref[...])]
    vh = [halves(vp_ref[...])] + [halves(vc_ref[i * blk:(i + 1) * blk, :]) for i in range(nq)] + [halves(vn_ref[...])]

    def blockdiag(parts, i, g):
        return jnp.concatenate([parts[i + w][g][e] for e in range(2) for w in range(3)], axis=0)

    qi = lax.broadcasted_iota(jnp.int32, (blk, blk), 0)
    ki = lax.broadcasted_iota(jnp.int32, (blk, blk), 1)
    ninf = jnp.float32(-jnp.inf)
    zer = jnp.zeros((blk, blk), F32)

    def band_bias(i):
        has_prev = jnp.where(n > 0, 0.0, ninf) if i == 0 else jnp.float32(0.0)
        has_next = jnp.where(n < ns - 1, 0.0, ninf) if i == nq - 1 else jnp.float32(0.0)
        return jnp.concatenate([jnp.where(ki >= qi, has_prev, ninf), zer,
                                jnp.where(ki <= qi, has_next, ninf)], axis=1)

    def chain(i, m, kbd, vbd, bias):
        rows = slice(i * blk, (i + 1) * blk)
        cols = slice(m * LANES, (m + 1) * LANES)
        s = lax.dot_general(q_ref[rows, cols], kbd, (((1,), (1,)), ((), ())), preferred_element_type=F32)
        yield
        ps, ds = [], []
        for e in range(2):
            sk = sink_ref[2 * m + e]
            se = s[:, e * 3 * blk:(e + 1) * 3 * blk] + bias
            mx = jnp.maximum(jnp.max(se, axis=-1, keepdims=True), sk)
            p = jnp.exp(se - mx)
            ds.append(jnp.sum(p, axis=-1, keepdims=True) + jnp.exp(sk - mx))
            ps.append(p.astype(BF16))
        o = jnp.dot(jnp.concatenate(ps, axis=1), vbd, preferred_element_type=F32)
        yield
        o_ref[rows, cols] = (o / jnp.where(lo, ds[0], ds[1])).astype(BF16)

    chains = []
    for i in range(nq):
        bias = band_bias(i)
        for g in range(ATTN_KV_HEADS):
            kbd = blockdiag(kh, i, g)
            vbd = blockdiag(vh, i, g)
            for m in range(g * group // 2, (g + 1) * group // 2):
                chains.append(chain(i, m, kbd, vbd, bias))
    _lockstep(chains)


def _win_attn(sink, q, k, v, batch, seq, nq):
    nb = seq // ATTN_BLOCK
    ns = nb // nq
    aw = q.shape[1]
    kvw = k.shape[1]
    cur = lambda b, n: (b * ns + n, 0)
    prev = lambda b, n: (b * nb + jnp.maximum(n * nq - 1, 0), 0)
    nxt = lambda b, n: (b * nb + jnp.minimum(n * nq + nq, nb - 1), 0)
    edge = lambda f: pl.BlockSpec((ATTN_BLOCK, kvw), f)
    own = pl.BlockSpec((nq * ATTN_BLOCK, kvw), cur)
    return pl.pallas_call(
        _win_attn_kernel,
        out_shape=jax.ShapeDtypeStruct(q.shape, BF16),
        grid=(batch, ns),
        in_specs=[pl.BlockSpec(memory_space=pltpu.SMEM),
                  pl.BlockSpec((nq * ATTN_BLOCK, aw), cur),
                  edge(prev), own, edge(nxt), edge(prev), own, edge(nxt)],
        out_specs=pl.BlockSpec((nq * ATTN_BLOCK, aw), cur),
        compiler_params=_cparams(2),
        name="win_attn",
    )(sink, q, k, k, k, v, v, v)


def _cumsum_both(x, axis):
    n = x.shape[axis]
    idx = lax.broadcasted_iota(jnp.int32, x.shape, axis)
    pre, suf = x, x
    s = 1
    while s < n:
        pre = pre + jnp.where(idx >= s, pltpu.roll(pre, s, axis), 0.0)
        suf = suf + jnp.where(idx < n - s, pltpu.roll(suf, n - s, axis), 0.0)
        s *= 2
    return pre, suf


def _unit_tri_inverse(nmat, rc_xor, lower):
    c = nmat.shape[0]
    n0 = jnp.where(rc_xor < TRI_BASE, nmat, 0.0)
    nb16 = n0.astype(BF16)
    p = yield (nb16, nb16)
    y = n0
    n_sq = (TRI_BASE - 1).bit_length() - 1
    for m in range(n_sq):
        pb = p.astype(BF16)
        if m < n_sq - 1:
            r2 = jnp.dot(pb, jnp.concatenate([y.astype(BF16), pb], axis=1), preferred_element_type=F32)
            yield
            y = y + p + r2[:, :c]
            p = r2[:, c:]
        else:
            py = yield (pb, y.astype(BF16))
            y = y + p + py
    neg_n = -nmat
    blk = TRI_BASE
    while blk < c:
        first = blk if lower else 0
        held = [slice(s, s + blk) for s in range(first, c, 2 * blk)]
        kept = [slice(s, s + blk) for s in range(blk - first, c, 2 * blk)]
        take = lambda a: jnp.concatenate([a[s] for s in held], axis=0)
        rx = take(rc_xor)
        l_rows = jnp.where((rx >= blk) & (rx < 2 * blk), take(neg_n), 0.0)
        y_rows = take(y)
        ly = yield (l_rows.astype(BF16), y.astype(BF16))
        u_rows = l_rows + ly
        ub = u_rows.astype(BF16)
        zero = jnp.zeros((blk, c), BF16)
        pieces = [ub[i * blk:(i + 1) * blk] for i in range(len(held))]
        order = [zero, None] if lower else [None, zero]
        u_full = jnp.concatenate([(pieces[i] if o is None else o) for i in range(len(held)) for o in order], axis=0)
        yu = yield (y_rows.astype(BF16), u_full)
        new_rows = y_rows - u_rows - yu
        parts = {}
        for i, s in enumerate(held):
            parts[s.start] = new_rows[i * blk:(i + 1) * blk]
        for s in kept:
            parts[s.start] = y[s]
        y = jnp.concatenate([parts[k] for k in sorted(parts)], axis=0)
        blk *= 2
    return y


def _deltanet_kernel(dq_ref, dk_ref, dv_ref, gate_ref, abr_ref, cw_ref, alog_ref, dtb_ref, gout_ref,
                     o_ref, shifts, qn, kn, vn, kt, oacc, st, pu, pw, pm2, pkte, pgl, gcum, gbeta, gcols, gsum):
    seq = dq_ref.shape[0]
    nc = seq // DN_CHUNK
    c = DN_CHUNK
    hd = DN_HEAD_DIM
    width = DN_HEADS * hd

    win = shifts.shape[2]
    pad = DN_CONV // 2
    lead = (win - c) // 2
    side_taps = [j for j in range(DN_CONV) if j != pad]

    @pl.when(pl.program_id(0) == 0)
    def _():
        tok = lax.broadcasted_iota(jnp.int32, (c, win), 0)
        src = lax.broadcasted_iota(jnp.int32, (c, win), 1)
        for v in range(shifts.shape[0]):
            shifts[v] = jnp.concatenate(
                [jnp.where(src == tok + (j - pad + v * lead), 1.0, 0.0).astype(BF16) for j in side_taps], axis=0)

    def conv_chunk(i):
        r = pl.multiple_of(i * c, c)
        start = pl.multiple_of(jnp.clip(r - lead, 0, seq - win), lead)
        sel = shifts[(r - start) // lead]

        def one(src_ref, w_off, dst, l2, scale):
            taps = jnp.dot(sel, src_ref[pl.ds(start, win), :], preferred_element_type=F32)
            yield
            for h in range(DN_HEADS):
                hs = slice(h * hd, (h + 1) * hd)
                wrow = lambda j: cw_ref[j:j + 1, w_off + h * hd:w_off + (h + 1) * hd]
                y = wrow(pad) * src_ref[pl.ds(r, c), hs].astype(F32)
                for n, j in enumerate(side_taps):
                    y = y + wrow(j) * taps[n * c:(n + 1) * c, hs]
                y = _silu(y)
                if l2:
                    y = y * lax.rsqrt(jnp.sum(y * y, axis=-1, keepdims=True) + EPS)
                    if scale != 1.0:
                        y = y * scale
                dst[i, :, hs] = y.astype(BF16)
                if dst is kn:
                    kt[i, hs, :] = y.T.astype(BF16)

        return [one(dq_ref, 0, qn, True, hd ** -0.5), one(dk_ref, width, kn, True, 1.0),
                one(dv_ref, 2 * width, vn, False, 1.0)]

    def conv_pair(i, carry):
        _lockstep(conv_chunk(2 * i) + conv_chunk(2 * i + 1))
        return carry

    lax.fori_loop(0, nc // 2, conv_pair, 0)

    st[...] = jnp.zeros(st.shape, F32)
    row = lax.broadcasted_iota(jnp.int32, (c, c), 0)
    col = lax.broadcasted_iota(jnp.int32, (c, c), 1)
    eye = row == col
    masks = ((row >= col, row > col), (row <= col, row < col))
    nh = DN_HEADS

    n_g = N_DIRS * nh
    n_r = abr_ref.shape[1]
    ab = jnp.concatenate([abr_ref[ci] for ci in range(nc)], axis=0)
    g = -jnp.exp(jnp.tile(alog_ref[...], (nc, 1))) * _softplus(ab + jnp.tile(dtb_ref[...], (nc, 1)))
    beta = jax.nn.sigmoid(ab)
    pre, suf = _cumsum_both(g, 1)
    cum = jnp.where(lax.broadcasted_iota(jnp.int32, g.shape, 0) % n_r < nh, pre, suf)
    gtot_all = jnp.broadcast_to(jnp.sum(g, axis=1, keepdims=True), g.shape)
    for ci in range(nc):
        rows = slice(ci * n_r, (ci + 1) * n_r)
        gcum[ci] = cum[rows]
        gbeta[ci] = beta[rows]
        gsum[ci] = gtot_all[rows]
        packed = jnp.concatenate([cum[ci * n_r:ci * n_r + n_g], -beta[ci * n_r + n_g:(ci + 1) * n_r],
                                  jnp.zeros((c - 2 * n_g, c), F32)], axis=0)
        gcols[ci] = packed.T

    def prep(ci, h, d, slot):
        j = d * nh + h
        jb = N_DIRS * nh + j
        gc_c = gcols[ci, :, j:j + 1]
        gc_r = gcum[ci, j:j + 1, :]
        neg_b_c = gcols[ci, :, jb:jb + 1]
        b_r = gbeta[ci, jb:jb + 1, :]
        gtot = gsum[ci, j:j + 1, 0:1]
        incl, strict = masks[d]
        hs = slice(h * hd, (h + 1) * hd)
        ktc = kt[ci, hs, :]
        qk2 = jnp.concatenate([qn[ci, :, hs], kn[ci, :, hs]], axis=0)
        gram = yield (qk2, ktc)
        decay = jnp.exp(jnp.where(incl, gc_c - gc_r, -jnp.inf))
        nmat = jnp.where(strict, gram[c:] * neg_b_c * decay, 0.0)
        y = yield from _unit_tri_inverse(nmat, row ^ col, lower=(d == 0))
        tinv = y + jnp.where(eye, 1.0, 0.0)
        eg_r = jnp.exp(gc_r)
        z = jnp.zeros((c, hd), BF16)
        vk = jnp.concatenate([jnp.concatenate([vn[ci, :, hs], z], axis=1),
                              jnp.concatenate([z, kn[ci, :, hs]], axis=1)], axis=0)
        uw = jnp.dot(jnp.concatenate([tinv * b_r, tinv * (b_r * eg_r)], axis=1).astype(BF16), vk,
                     preferred_element_type=F32)
        yield
        pu[slot] = uw[:, :hd]
        pw[slot] = uw[:, hd:].astype(BF16)
        pm2[slot] = jnp.concatenate([jnp.where(eye, eg_r, 0.0), gram[:c] * decay], axis=1).astype(BF16)
        pkte[slot] = (ktc.astype(F32) * jnp.exp(gtot - gc_r)).astype(BF16)
        pgl[slot] = jnp.broadcast_to(jnp.exp(gtot), pgl.shape[1:])

    def scan(ci, h, d, slot):
        j = d * nh + h
        hs = slice(h * hd, (h + 1) * hd)
        wq = jnp.concatenate([pw[slot], qn[ci, :, hs]], axis=0)
        pm = yield (wq, st[j].astype(BF16))
        vb = (pu[slot] - pm[:c]).astype(BF16)
        o = jnp.dot(pm2[slot], jnp.concatenate([pm[c:].astype(BF16), vb], axis=0), preferred_element_type=F32)
        ds = yield (pkte[slot], vb)
        st[j] = st[j] * pgl[slot][0:1, :] + ds
        oacc[ci, :, hs] = oacc[ci, :, hs] + o

    def chunk_of(step_idx, d):
        return step_idx if d == 0 else nc - 1 - step_idx

    n_chain = N_DIRS * nh
    group = pm2.shape[0] // n_chain

    def preps(step0):
        return [prep(chunk_of(step0 + g, d), h, d, g * n_chain + d * nh + h)
                for g in range(group) for d in range(N_DIRS) for h in range(nh)]

    def scans(step0):
        def run(h, d):
            for g in range(group):
                yield from scan(chunk_of(step0 + g, d), h, d, g * n_chain + d * nh + h)
        return [run(h, d) for d in range(N_DIRS) for h in range(nh)]

    oacc[...] = jnp.zeros(oacc.shape, F32)
    _lockstep(preps(0))

    def group_of_steps(i, carry):
        s0 = i * group
        _lockstep(scans(s0))
        _lockstep(preps(s0 + group))
        return carry

    lax.fori_loop(0, nc // group - 1, group_of_steps, 0)
    _lockstep(scans(nc - group))

    def finish(ci, carry):
        r = pl.multiple_of(ci * c, c)
        for h in range(nh):
            hs = slice(h * hd, (h + 1) * hd)
            gate = gate_ref[pl.ds(r, c), hs].astype(F32)
            o_ref[pl.ds(r, c), hs] = (_rms(oacc[ci, :, hs], gout_ref[...]) * _silu(gate)).astype(BF16)
        return carry

    lax.fori_loop(0, nc, finish, 0)


def _deltanet(dqkv, gate, abr, conv_w, alog_c, dtb_c, gout, batch, seq):
    width = DN_HEADS * DN_HEAD_DIM
    nc = seq // DN_CHUNK
    n_ab = abr.shape[1]
    col = lambda j: pl.BlockSpec((seq, width), lambda b, j=j: (b, j))
    return pl.pallas_call(
        _deltanet_kernel,
        out_shape=jax.ShapeDtypeStruct((batch * seq, width), BF16),
        grid=(batch,),
        in_specs=[col(0), col(1), col(2), col(0),
                  pl.BlockSpec((nc, n_ab, DN_CHUNK), lambda b: (b, 0, 0)),
                  _const_spec(conv_w.shape),
                  _const_spec(alog_c.shape), _const_spec(dtb_c.shape),
                  _const_spec(gout.shape)],
        out_specs=col(0),
        scratch_shapes=[
            pltpu.VMEM((3, (DN_CONV - 1) * DN_CHUNK, 2 * DN_CHUNK), BF16),
            pltpu.VMEM((nc, DN_CHUNK, width), BF16),
            pltpu.VMEM((nc, DN_CHUNK, width), BF16),
            pltpu.VMEM((nc, DN_CHUNK, width), BF16),
            pltpu.VMEM((nc, width, DN_CHUNK), BF16),
            pltpu.VMEM((nc, DN_CHUNK, width), F32),
            pltpu.VMEM((N_DIRS * DN_HEADS, DN_HEAD_DIM, DN_HEAD_DIM), F32),
            pltpu.VMEM((DN_GROUP * N_DIRS * DN_HEADS, DN_CHUNK, DN_HEAD_DIM), F32),
            pltpu.VMEM((DN_GROUP * N_DIRS * DN_HEADS, DN_CHUNK, DN_HEAD_DIM), BF16),
            pltpu.VMEM((DN_GROUP * N_DIRS * DN_HEADS, DN_CHUNK, 2 * DN_CHUNK), BF16),
            pltpu.VMEM((DN_GROUP * N_DIRS * DN_HEADS, DN_HEAD_DIM, DN_CHUNK), BF16),
            pltpu.VMEM((DN_GROUP * N_DIRS * DN_HEADS, 8, LANES), F32),
            pltpu.VMEM((nc, n_ab, DN_CHUNK), F32),
            pltpu.VMEM((nc, n_ab, DN_CHUNK), F32),
            pltpu.VMEM((nc, DN_CHUNK, LANES), F32),
            pltpu.VMEM((nc, n_ab, DN_CHUNK), F32),
        ],
        compiler_params=_cparams(1),
        name="deltanet",
    )(dqkv, dqkv, dqkv, gate, abr, conv_w, alog_c, dtb_c, gout)


def _mem_kv_kernel(mem_ref, gain_ref, w_ref, k_ref, v_ref):
    d = k_ref.shape[1]
    mn = _rms(mem_ref[...], gain_ref[...]).astype(BF16)
    k_ref[...] = jnp.dot(mn, w_ref[:, :d], preferred_element_type=F32).astype(BF16)
    v_ref[...] = jnp.dot(mn, w_ref[:, d:], preferred_element_type=F32).astype(BF16)


def _mem_kv(mem2, gain, w_ckv, mem_len):
    rows, d = mem2.shape
    spec = pl.BlockSpec((mem_len, d), lambda b: (b, 0))
    return pl.pallas_call(
        _mem_kv_kernel,
        out_shape=(jax.ShapeDtypeStruct((rows, d), BF16),) * 2,
        grid=(rows // mem_len,),
        in_specs=[spec, _const_spec((1, d)), _const_spec(w_ckv.shape)],
        out_specs=(spec, spec),
        compiler_params=_cparams(1),
        name="mem_kv",
    )(mem2, gain, w_ckv)


def _out_cross_kernel(x_ref, ao_ref, do_ref, wout_ref, gpost_ref, gpre_ref, wq_ref, km_ref, vm_ref,
                      wo_ref, gcpost_ref, y_ref):
    aw = ao_ref.shape[1]
    d = x_ref.shape[1]
    chd = d // CROSS_HEADS
    nt = (((1,), (1,)), ((), ()))

    def rows_chain(r0):
        rs = slice(r0, r0 + ROW_SUB)
        mix = (jnp.dot(ao_ref[rs, :], wout_ref[:aw, :], preferred_element_type=F32)
               + jnp.dot(do_ref[rs, :], wout_ref[aw:, :], preferred_element_type=F32))
        yield
        x1 = x_ref[rs, :] + _rms(mix, gpost_ref[...])
        q = jnp.dot(_rms(x1, gpre_ref[...]).astype(BF16), wq_ref[...], preferred_element_type=F32)
        yield
        q = (q * (chd ** -0.5)).astype(BF16)
        hcols = [slice(h * chd, (h + 1) * chd) for h in range(CROSS_HEADS)]
        scores = [lax.dot_general(q[:, hs], km_ref[:, hs], nt, preferred_element_type=F32) for hs in hcols]
        yield
        probs = [jnp.exp(s - jnp.max(s, axis=-1, keepdims=True)) for s in scores]
        outs = [jnp.dot(p.astype(BF16), vm_ref[:, hs], preferred_element_type=F32) for p, hs in zip(probs, hcols)]
        yield
        heads = [(o / jnp.sum(p, axis=-1, keepdims=True)).astype(BF16) for o, p in zip(outs, probs)]
        cproj = jnp.dot(jnp.concatenate(heads, axis=1), wo_ref[...], preferred_element_type=F32)
        yield
        y_ref[rs, :] = x1 + _rms(cproj, gcpost_ref[...])

    _lockstep([rows_chain(r0) for r0 in range(0, x_ref.shape[0], ROW_SUB)])


def _out_cross(x2, attn_o, dn_o, w_out, g_post, g_pre, w_cq, kmem, vmem, w_co, g_cpost, seq, mem_len, tm):
    t, d = x2.shape
    per_batch = seq // tm
    row = lambda w: pl.BlockSpec((tm, w), lambda i: (i, 0))
    memspec = pl.BlockSpec((mem_len, d), lambda i: (i // per_batch, 0))
    vec = _const_spec((1, d))
    return pl.pallas_call(
        _out_cross_kernel,
        out_shape=jax.ShapeDtypeStruct((t, d), F32),
        grid=(t // tm,),
        in_specs=[row(d), row(attn_o.shape[1]), row(dn_o.shape[1]), _const_spec(w_out.shape), vec, vec,
                  _const_spec(w_cq.shape), memspec, memspec, _const_spec(w_co.shape), vec],
        out_specs=row(d),
        compiler_params=_cparams(1),
        name="out_cross",
    )(x2, attn_o, dn_o, w_out, g_post, g_pre, w_cq, kmem, vmem, w_co, g_cpost)


def _ffn_kernel(x_ref, gpre_ref, wgu_ref, wdown_ref, gpost_ref, y_ref):
    dff = wdown_ref.shape[0]

    def rows_chain(r0):
        rs = slice(r0, r0 + ROW_SUB)
        hf = _rms(x_ref[rs, :], gpre_ref[...]).astype(BF16)
        gu = jnp.dot(hf, wgu_ref[...], preferred_element_type=F32)
        yield
        act = (_silu(gu[:, :dff]) * gu[:, dff:]).astype(BF16)
        f = jnp.dot(act, wdown_ref[...], preferred_element_type=F32)
        yield
        y_ref[rs, :] = x_ref[rs, :] + _rms(f, gpost_ref[...])

    _lockstep([rows_chain(r0) for r0 in range(0, x_ref.shape[0], ROW_SUB)])


def _ffn(x2, g_pre, w_gu, w_down, g_post, tm):
    t, d = x2.shape
    row = pl.BlockSpec((tm, d), lambda i: (i, 0))
    vec = _const_spec((1, d))
    return pl.pallas_call(
        _ffn_kernel,
        out_shape=jax.ShapeDtypeStruct((t, d), F32),
        grid=(t // tm,),
        in_specs=[row, vec, _const_spec(w_gu.shape), _const_spec(w_down.shape), vec],
        out_specs=row,
        compiler_params=_cparams(1),
        name="ffn",
    )(x2, g_pre, w_gu, w_down, g_post)


def _layer(x, mem, positions, g_mix_pre, w_in, conv_w, a_log, dt_bias, g_dn_out, attn_sink, w_out,
           g_mix_post, g_cross_pre, g_mem, w_cq, w_ckv, w_co, g_cross_post, g_ffn_pre, w_gate_up,
           w_down, g_ffn_post):
    batch, seq, d = x.shape
    mem_len = mem.shape[1]
    t = batch * seq
    x2 = x.reshape(t, d)
    vec = lambda g: g.reshape(1, -1).astype(F32)

    half = ATTN_HEAD_DIM // 2
    inv_freq = ROPE_THETA ** (-jnp.arange(half, dtype=F32) / half)
    invf_row = jnp.tile(inv_freq, LANES // half).reshape(1, LANES)

    n_ab = 2 * N_DIRS * DN_HEADS
    n_main = w_in.shape[1] - n_ab
    w_main = w_in[:, :n_main].astype(BF16)
    w_abt = w_in[:, n_main:].T.astype(BF16)
    q, k, v, dqkv, dgate, abr = _in_proj(x2, positions.reshape(t // LANES, LANES), invf_row, vec(g_mix_pre),
                                              w_main, w_abt, tm=min(1024, t))

    attn_o = _win_attn(attn_sink.astype(F32), q, k, v, batch, seq, nq=min(4, seq // ATTN_BLOCK))

    pad_c = lambda p: jnp.pad(p.reshape(-1, 1).astype(F32), ((0, n_ab - N_DIRS * DN_HEADS), (0, 0)))
    dn_o = _deltanet(dqkv, dgate, abr, conv_w.astype(F32), pad_c(a_log), pad_c(dt_bias), vec(g_dn_out), batch, seq)

    kmem, vmem = _mem_kv(mem.reshape(batch * mem_len, d), vec(g_mem), w_ckv.astype(BF16), mem_len)
    x3 = _out_cross(x2, attn_o, dn_o, w_out.astype(BF16), vec(g_mix_post), vec(g_cross_pre),
                    w_cq.astype(BF16), kmem, vmem, w_co.astype(BF16), vec(g_cross_post), seq, mem_len, tm=min(1024, seq))
    y = _ffn(x3, vec(g_ffn_pre), w_gate_up.astype(BF16), w_down.astype(BF16), vec(g_ffn_post), tm=512)
    return y.reshape(batch, seq, d)


def kernel(x, mem, positions, g_mix_pre, w_in, conv_w, a_log, dt_bias, g_dn_out, attn_sink, w_out,
           g_mix_post, g_cross_pre, g_mem, w_cq, w_ckv, w_co, g_cross_post, g_ffn_pre, w_gate_up,
           w_down, g_ffn_post):
    depth = w_in.shape[0]
    for l in range(depth):
        x = _layer(x, mem, positions, g_mix_pre[l], w_in[l], conv_w[l], a_log[l], dt_bias[l], g_dn_out[l],
                   attn_sink[l], w_out[l], g_mix_post[l], g_cross_pre[l], g_mem[l], w_cq[l], w_ckv[l],
                   w_co[l], g_cross_post[l], g_ffn_pre[l], w_gate_up[l], w_down[l], g_ffn_post[l])
    return x
```

```python
import functools

import jax
import jax.numpy as jnp
from jax import lax
from jax.experimental import pallas as pl
from jax.experimental.pallas import tpu as pltpu

F32 = jnp.float32
BF16 = jnp.bfloat16

EPS = 1e-6
LOG2E = 1.4426950408889634
ROPE_THETA = 10000.0
ATTN_HEADS = 8
ATTN_KV_HEADS = 2
ATTN_HEAD_DIM = 64
ATTN_BLOCK = 128
DN_HEADS = 4
DN_HEAD_DIM = 128
DN_CONV = 5
N_DIRS = 2
CROSS_HEADS = 4

LANES = 128
DN_CHUNK = 128
ROW_SUB = 256
DN_GROUP = 2
TRI_BASE = 16
VMEM_LIMIT = 56 * 1024 * 1024


def _cparams(n_grid_dims):
    return pltpu.CompilerParams(
        dimension_semantics=("arbitrary",) * n_grid_dims,
        vmem_limit_bytes=VMEM_LIMIT,
    )


def _rms(x, gain):
    return x * lax.rsqrt(jnp.mean(x * x, axis=-1, keepdims=True) + EPS) * gain


def _silu(x):
    return x * jax.nn.sigmoid(x)


def _softplus(x):
    return jnp.maximum(x, 0.0) + jnp.log(1.0 + jnp.exp(-jnp.abs(x)))


def _const_spec(shape):
    nd = len(shape)
    return pl.BlockSpec(shape, lambda *_: (0,) * nd, pipeline_mode=pl.Buffered(1))


def _in_proj_kernel(x_ref, pos_ref, invf_ref, gain_ref, w_ref,
                    q_ref, k_ref, v_ref, dqkv_ref, gate_ref, abr_ref):
    half = ATTN_HEAD_DIM // 2
    lane = lax.broadcasted_iota(jnp.int32, (1, LANES), 1)
    first_half = (lane % ATTN_HEAD_DIM) < half
    aw = ATTN_HEADS * ATTN_HEAD_DIM
    kvw = ATTN_KV_HEADS * ATTN_HEAD_DIM
    off = aw + 2 * kvw
    dn = dqkv_ref.shape[1]
    per_sub = ROW_SUB // DN_CHUNK
    n_main = off + dn + gate_ref.shape[1]
    w_abt = w_ref[:, n_main:n_main + LANES].astype(F32).T[:abr_ref.shape[1]].astype(BF16)

    def rows_chain(r0):
        rs = slice(r0, r0 + ROW_SUB)
        h = _rms(x_ref[rs, :], gain_ref[...]).astype(BF16)
        proj = lambda lo, hi: jnp.dot(h, w_ref[:, lo:hi], preferred_element_type=F32)
        q = proj(0, aw)
        kv = proj(aw, off)
        dqkv = proj(off, off + dn)
        gate = proj(off + dn, off + dn + gate_ref.shape[1])
        abr = lax.dot_general(w_abt, h, (((1,), (1,)), ((), ())), preferred_element_type=F32)
        yield
        pos = jnp.concatenate(
            [jnp.broadcast_to(pos_ref[g:g + 1, :].astype(F32), (LANES, LANES)).T
             for g in range(r0 // LANES, (r0 + ROW_SUB) // LANES)], axis=0)
        ang = pos * invf_ref[...]
        cos = jnp.cos(ang)
        sin = jnp.sin(ang)

        def rope(t):
            rot = jnp.where(first_half, -pltpu.roll(t, LANES - half, 1), pltpu.roll(t, half, 1))
            return t * cos + rot * sin

        for m in range(aw // LANES):
            cols = slice(m * LANES, (m + 1) * LANES)
            q_ref[rs, cols] = (rope(q[:, cols]) * (ATTN_HEAD_DIM ** -0.5 * LOG2E)).astype(BF16)
        k_ref[rs, :] = rope(kv[:, :kvw]).astype(BF16)
        v_ref[rs, :] = kv[:, kvw:].astype(BF16)
        dqkv_ref[rs, :] = dqkv.astype(BF16)
        gate_ref[rs, :] = gate.astype(BF16)
        for j in range(per_sub):
            abr_ref[r0 // DN_CHUNK + j] = abr[:, j * DN_CHUNK:(j + 1) * DN_CHUNK]

    _lockstep([rows_chain(r0) for r0 in range(0, x_ref.shape[0], ROW_SUB)])


def _in_proj(x2, pos_rows, invf_row, gain, w_all, n_ab, tm):
    t, d = x2.shape
    aw = ATTN_HEADS * ATTN_HEAD_DIM
    kvw = ATTN_KV_HEADS * ATTN_HEAD_DIM
    dnw = DN_HEADS * DN_HEAD_DIM
    row = lambda w: pl.BlockSpec((tm, w), lambda i: (i, 0))
    out_shape = (
        jax.ShapeDtypeStruct((t, aw), BF16),
        jax.ShapeDtypeStruct((t, kvw), BF16),
        jax.ShapeDtypeStruct((t, kvw), BF16),
        jax.ShapeDtypeStruct((t, 3 * dnw), BF16),
        jax.ShapeDtypeStruct((t, dnw), BF16),
        jax.ShapeDtypeStruct((t // DN_CHUNK, n_ab, DN_CHUNK), F32),
    )
    out_specs = (
        row(aw), row(kvw), row(kvw), row(3 * dnw), row(dnw),
        pl.BlockSpec((tm // DN_CHUNK, n_ab, DN_CHUNK), lambda i: (i, 0, 0)),
    )
    return pl.pallas_call(
        _in_proj_kernel,
        out_shape=out_shape,
        grid=(t // tm,),
        in_specs=[row(d), pl.BlockSpec((tm // LANES, LANES), lambda i: (i, 0)), _const_spec((1, LANES)),
                  _const_spec((1, d)), _const_spec(w_all.shape)],
        out_specs=out_specs,
        compiler_params=_cparams(1),
        name="in_proj",
    )(x2, pos_rows, invf_row, gain, w_all)


def _paired_dots(reqs):
    out = [None] * len(reqs)
    todo = [i for i, r in enumerate(reqs) if r is not None]
    while todo:
        i = todo.pop(0)
        l1, r1 = reqs[i]
        j = next((t for t in todo if reqs[t][0].shape == l1.shape and reqs[t][1].shape == r1.shape), None)
        if j is None or r1.shape[1] != LANES:
            out[i] = jnp.dot(l1, r1, preferred_element_type=F32)
            continue
        todo.remove(j)
        l2, r2 = reqs[j]
        z = jnp.zeros_like(r1)
        rhs = jnp.concatenate([jnp.concatenate([r1, z], axis=1), jnp.concatenate([z, r2], axis=1)], axis=0)
        both = jnp.dot(jnp.concatenate([l1, l2], axis=1), rhs, preferred_element_type=F32)
        out[i] = both[:, :LANES]
        out[j] = both[:, LANES:]
    return out


def _lockstep(gens):
    live = list(gens)
    sends = [None] * len(live)
    while live:
        reqs, still = [], []
        for g, val in zip(live, sends):
            try:
                reqs.append(g.send(val))
                still.append(g)
            except StopIteration:
                pass
        live = still
        sends = _paired_dots(reqs)


def _win_attn_kernel(sink_ref, q_ref, kp_ref, kc_ref, kn_ref, vp_ref, vc_ref, vn_ref, o_ref):
    n = pl.program_id(1)
    ns = pl.num_programs(1)
    blk = ATTN_BLOCK
    hd = ATTN_HEAD_DIM
    nq = q_ref.shape[0] // blk
    group = ATTN_HEADS // ATTN_KV_HEADS
    lane = lax.broadcasted_iota(jnp.int32, (1, LANES), 1)
    lo = lane < hd

    def halves(x):
        swapped = pltpu.roll(x.astype(F32), hd, 1).astype(BF16)
        z = jnp.zeros_like(x)
        return ((jnp.where(lo, x, z), jnp.where(lo, z, swapped)),
                (jnp.where(lo, swapped, z), jnp.where(lo, z, x)))

    kh = [halves(kp_ref[...])] + [halves(kc_ref[i * blk:(i + 1) * blk, :]) for i in range(nq)] + [halves(kn_ref[...])]
    vh = [halves(vp_ref[...])] + [halves(vc_ref[i * blk:(i + 1) * blk, :]) for i in range(nq)] + [halves(vn_ref[...])]

    def blockdiag(parts, i, g):
        return jnp.concatenate([parts[i + w][g][e] for e in range(2) for w in range(3)], axis=0)

    qi = lax.broadcasted_iota(jnp.int32, (blk, blk), 0)
    ki = lax.broadcasted_iota(jnp.int32, (blk, blk), 1)
    ninf = jnp.float32(-jnp.inf)
    zer = jnp.zeros((blk, blk), F32)

    def band_bias(i):
        has_prev = jnp.where(n > 0, 0.0, ninf) if i == 0 else jnp.float32(0.0)
        has_next = jnp.where(n < ns - 1, 0.0, ninf) if i == nq - 1 else jnp.float32(0.0)
        return jnp.concatenate([jnp.where(ki >= qi, has_prev, ninf), zer,
                                jnp.where(ki <= qi, has_next, ninf)], axis=1)

    def chain(i, m, kbd, vbd, bias):
        rows = slice(i * blk, (i + 1) * blk)
        cols = slice(m * LANES, (m + 1) * LANES)
        s = lax.dot_general(q_ref[rows, cols], kbd, (((1,), (1,)), ((), ())), preferred_element_type=F32)
        yield
        ps, ds = [], []
        for e in range(2):
            sk = sink_ref[2 * m + e] * LOG2E
            se = s[:, e * 3 * blk:(e + 1) * 3 * blk] + bias
            mx = jnp.maximum(jnp.max(se, axis=-1, keepdims=True), sk)
            p = jnp.exp2(se - mx)
            ds.append(jnp.sum(p, axis=-1, keepdims=True) + jnp.exp2(sk - mx))
            ps.append(p.astype(BF16))
        o = jnp.dot(jnp.concatenate(ps, axis=1), vbd, preferred_element_type=F32)
        yield
        o_ref[rows, cols] = (o / jnp.where(lo, ds[0], ds[1])).astype(BF16)

    chains = []
    for i in range(nq):
        bias = band_bias(i)
        for g in range(ATTN_KV_HEADS):
            kbd = blockdiag(kh, i, g)
            vbd = blockdiag(vh, i, g)
            for m in range(g * group // 2, (g + 1) * group // 2):
                chains.append(chain(i, m, kbd, vbd, bias))
    _lockstep(chains)


def _win_attn(sink, q, k, v, batch, seq, nq):
    nb = seq // ATTN_BLOCK
    ns = nb // nq
    aw = q.shape[1]
    kvw = k.shape[1]
    cur = lambda b, n: (b * ns + n, 0)
    prev = lambda b, n: (b * nb + jnp.maximum(n * nq - 1, 0), 0)
    nxt = lambda b, n: (b * nb + jnp.minimum(n * nq + nq, nb - 1), 0)
    edge = lambda f: pl.BlockSpec((ATTN_BLOCK, kvw), f)
    own = pl.BlockSpec((nq * ATTN_BLOCK, kvw), cur)
    return pl.pallas_call(
        _win_attn_kernel,
        out_shape=jax.ShapeDtypeStruct(q.shape, BF16),
        grid=(batch, ns),
        in_specs=[pl.BlockSpec(memory_space=pltpu.SMEM),
                  pl.BlockSpec((nq * ATTN_BLOCK, aw), cur),
                  edge(prev), own, edge(nxt), edge(prev), own, edge(nxt)],
        out_specs=pl.BlockSpec((nq * ATTN_BLOCK, aw), cur),
        compiler_params=_cparams(2),
        name="win_attn",
    )(sink, q, k, k, k, v, v, v)


def _cumsum_both(x, axis):
    n = x.shape[axis]
    idx = lax.broadcasted_iota(jnp.int32, x.shape, axis)
    pre, suf = x, x
    s = 1
    while s < n:
        pre = pre + jnp.where(idx >= s, pltpu.roll(pre, s, axis), 0.0)
        suf = suf + jnp.where(idx < n - s, pltpu.roll(suf, n - s, axis), 0.0)
        s *= 2
    return pre, suf


def _unit_tri_inverse(nmat, rc_xor, lower):
    c = nmat.shape[0]
    n0 = jnp.where(rc_xor < TRI_BASE, nmat, 0.0)
    nb16 = n0.astype(BF16)
    p = yield (nb16, nb16)
    y = n0
    n_sq = (TRI_BASE - 1).bit_length() - 1
    for m in range(n_sq):
        pb = p.astype(BF16)
        if m < n_sq - 1:
            r2 = jnp.dot(pb, jnp.concatenate([y.astype(BF16), pb], axis=1), preferred_element_type=F32)
            yield
            y = y + p + r2[:, :c]
            p = r2[:, c:]
        else:
            py = yield (pb, y.astype(BF16))
            y = y + p + py
    neg_n = -nmat
    blk = TRI_BASE
    while blk < c:
        first = blk if lower else 0
        held = [slice(s, s + blk) for s in range(first, c, 2 * blk)]
        kept = [slice(s, s + blk) for s in range(blk - first, c, 2 * blk)]
        take = lambda a: jnp.concatenate([a[s] for s in held], axis=0)
        rx = take(rc_xor)
        l_rows = jnp.where((rx >= blk) & (rx < 2 * blk), take(neg_n), 0.0)
        y_rows = take(y)
        ly = yield (l_rows.astype(BF16), y.astype(BF16))
        u_rows = l_rows + ly
        ub = u_rows.astype(BF16)
        zero = jnp.zeros((blk, c), BF16)
        pieces = [ub[i * blk:(i + 1) * blk] for i in range(len(held))]
        order = [zero, None] if lower else [None, zero]
        u_full = jnp.concatenate([(pieces[i] if o is None else o) for i in range(len(held)) for o in order], axis=0)
        yu = yield (y_rows.astype(BF16), u_full)
        new_rows = y_rows - u_rows - yu
        parts = {}
        for i, s in enumerate(held):
            parts[s.start] = new_rows[i * blk:(i + 1) * blk]
        for s in kept:
            parts[s.start] = y[s]
        y = jnp.concatenate([parts[k] for k in sorted(parts)], axis=0)
        blk *= 2
    return y


def _deltanet_kernel(dq_ref, dk_ref, dv_ref, gate_ref, abr_ref, cw_ref, alog_ref, dtb_ref, gout_ref,
                     o_ref, shifts, qn, kn, vn, kt, oacc, st, pu, pw, pm2, pkte, pgl, gcum, gbeta, gcols, gsum):
    seq = dq_ref.shape[0]
    nc = seq // DN_CHUNK
    c = DN_CHUNK
    hd = DN_HEAD_DIM
    width = DN_HEADS * hd

    win = shifts.shape[2]
    pad = DN_CONV // 2
    lead = (win - c) // 2
    side_taps = [j for j in range(DN_CONV) if j != pad]

    @pl.when(pl.program_id(0) == 0)
    def _():
        tok = lax.broadcasted_iota(jnp.int32, (c, win), 0)
        src = lax.broadcasted_iota(jnp.int32, (c, win), 1)
        for v in range(shifts.shape[0]):
            shifts[v] = jnp.concatenate(
                [jnp.where(src == tok + (j - pad + v * lead), 1.0, 0.0).astype(BF16) for j in side_taps], axis=0)

    def conv_chunk(i):
        r = pl.multiple_of(i * c, c)
        start = pl.multiple_of(jnp.clip(r - lead, 0, seq - win), lead)
        sel = shifts[(r - start) // lead]

        def one(src_ref, w_off, dst, l2, scale):
            taps = jnp.dot(sel, src_ref[pl.ds(start, win), :], preferred_element_type=F32)
            yield
            for h in range(DN_HEADS):
                hs = slice(h * hd, (h + 1) * hd)
                wrow = lambda j: cw_ref[j:j + 1, w_off + h * hd:w_off + (h + 1) * hd]
                y = wrow(pad) * src_ref[pl.ds(r, c), hs].astype(F32)
                for n, j in enumerate(side_taps):
                    y = y + wrow(j) * taps[n * c:(n + 1) * c, hs]
                y = _silu(y)
                if l2:
                    y = y * lax.rsqrt(jnp.sum(y * y, axis=-1, keepdims=True) + EPS)
                    if scale != 1.0:
                        y = y * scale
                dst[i, :, hs] = y.astype(BF16)
                if dst is kn:
                    kt[i, hs, :] = y.T.astype(BF16)

        return [one(dq_ref, 0, qn, True, hd ** -0.5), one(dk_ref, width, kn, True, 1.0),
                one(dv_ref, 2 * width, vn, False, 1.0)]

    def conv_pair(i, carry):
        _lockstep(conv_chunk(2 * i) + conv_chunk(2 * i + 1))
        return carry

    lax.fori_loop(0, nc // 2, conv_pair, 0)

    st[...] = jnp.zeros(st.shape, F32)
    row = lax.broadcasted_iota(jnp.int32, (c, c), 0)
    col = lax.broadcasted_iota(jnp.int32, (c, c), 1)
    eye = row == col
    masks = ((row >= col, row > col), (row <= col, row < col))
    nh = DN_HEADS

    n_g = N_DIRS * nh
    n_r = abr_ref.shape[1]
    ab = jnp.concatenate([abr_ref[ci] for ci in range(nc)], axis=0)
    g = -jnp.exp(jnp.tile(alog_ref[...], (nc, 1))) * _softplus(ab + jnp.tile(dtb_ref[...], (nc, 1)))
    beta = jax.nn.sigmoid(ab)
    pre, suf = _cumsum_both(g, 1)
    cum = jnp.where(lax.broadcasted_iota(jnp.int32, g.shape, 0) % n_r < nh, pre, suf)
    gtot_all = jnp.broadcast_to(jnp.sum(g, axis=1, keepdims=True), g.shape)
    for ci in range(nc):
        rows = slice(ci * n_r, (ci + 1) * n_r)
        gcum[ci] = cum[rows]
        gbeta[ci] = beta[rows]
        gsum[ci] = gtot_all[rows]
        packed = jnp.concatenate([cum[ci * n_r:ci * n_r + n_g], -beta[ci * n_r + n_g:(ci + 1) * n_r],
                                  jnp.zeros((c - 2 * n_g, c), F32)], axis=0)
        gcols[ci] = packed.T

    def prep(ci, h, d, slot):
        j = d * nh + h
        jb = N_DIRS * nh + j
        gc_c = gcols[ci, :, j:j + 1]
        gc_r = gcum[ci, j:j + 1, :]
        neg_b_c = gcols[ci, :, jb:jb + 1]
        b_r = gbeta[ci, jb:jb + 1, :]
        gtot = gsum[ci, j:j + 1, 0:1]
        incl, strict = masks[d]
        hs = slice(h * hd, (h + 1) * hd)
        ktc = kt[ci, hs, :]
        qk2 = jnp.concatenate([qn[ci, :, hs], kn[ci, :, hs]], axis=0)
        gram = yield (qk2, ktc)
        decay = jnp.exp(jnp.where(incl, gc_c - gc_r, -jnp.inf))
        nmat = jnp.where(strict, gram[c:] * neg_b_c * decay, 0.0)
        y = yield from _unit_tri_inverse(nmat, row ^ col, lower=(d == 0))
        tinv = y + jnp.where(eye, 1.0, 0.0)
        eg_r = jnp.exp(gc_r)
        z = jnp.zeros((c, hd), BF16)
        vk = jnp.concatenate([jnp.concatenate([vn[ci, :, hs], z], axis=1),
                              jnp.concatenate([z, kn[ci, :, hs]], axis=1)], axis=0)
        uw = jnp.dot(jnp.concatenate([tinv * b_r, tinv * (b_r * eg_r)], axis=1).astype(BF16), vk,
                     preferred_element_type=F32)
        yield
        pu[slot] = uw[:, :hd]
        pw[slot] = uw[:, hd:].astype(BF16)
        pm2[slot] = jnp.concatenate([jnp.where(eye, eg_r, 0.0), gram[:c] * decay], axis=1).astype(BF16)
        pkte[slot] = (ktc.astype(F32) * jnp.exp(gtot - gc_r)).astype(BF16)
        pgl[slot] = jnp.broadcast_to(jnp.exp(gtot), pgl.shape[1:])

    def scan(ci, h, d, slot):
        j = d * nh + h
        hs = slice(h * hd, (h + 1) * hd)
        wq = jnp.concatenate([pw[slot], qn[ci, :, hs]], axis=0)
        pm = yield (wq, st[j].astype(BF16))
        vb = (pu[slot] - pm[:c]).astype(BF16)
        o = jnp.dot(pm2[slot], jnp.concatenate([pm[c:].astype(BF16), vb], axis=0), preferred_element_type=F32)
        ds = yield (pkte[slot], vb)
        st[j] = st[j] * pgl[slot][0:1, :] + ds
        oacc[ci, :, hs] = oacc[ci, :, hs] + o

    def chunk_of(step_idx, d):
        return step_idx if d == 0 else nc - 1 - step_idx

    n_chain = N_DIRS * nh
    group = pm2.shape[0] // n_chain

    def preps(step0):
        return [prep(chunk_of(step0 + g, d), h, d, g * n_chain + d * nh + h)
                for g in range(group) for d in range(N_DIRS) for h in range(nh)]

    def scans(step0):
        def run(h, d):
            for g in range(group):
                yield from scan(chunk_of(step0 + g, d), h, d, g * n_chain + d * nh + h)
        return [run(h, d) for d in range(N_DIRS) for h in range(nh)]

    oacc[...] = jnp.zeros(oacc.shape, F32)
    _lockstep(preps(0))

    def group_of_steps(i, carry):
        s0 = i * group
        _lockstep(scans(s0))
        _lockstep(preps(s0 + group))
        return carry

    lax.fori_loop(0, nc // group - 1, group_of_steps, 0)
    _lockstep(scans(nc - group))

    def finish(ci, carry):
        r = pl.multiple_of(ci * c, c)
        for h in range(nh):
            hs = slice(h * hd, (h + 1) * hd)
            gate = gate_ref[pl.ds(r, c), hs].astype(F32)
            o_ref[pl.ds(r, c), hs] = (_rms(oacc[ci, :, hs], gout_ref[...]) * _silu(gate)).astype(BF16)
        return carry

    lax.fori_loop(0, nc, finish, 0)


def _deltanet(dqkv, gate, abr, conv_w, alog_c, dtb_c, gout, batch, seq):
    width = DN_HEADS * DN_HEAD_DIM
    nc = seq // DN_CHUNK
    n_ab = abr.shape[1]
    col = lambda j: pl.BlockSpec((seq, width), lambda b, j=j: (b, j))
    return pl.pallas_call(
        _deltanet_kernel,
        out_shape=jax.ShapeDtypeStruct((batch * seq, width), BF16),
        grid=(batch,),
        in_specs=[col(0), col(1), col(2), col(0),
                  pl.BlockSpec((nc, n_ab, DN_CHUNK), lambda b: (b, 0, 0)),
                  _const_spec(conv_w.shape),
                  _const_spec(alog_c.shape), _const_spec(dtb_c.shape),
                  _const_spec(gout.shape)],
        out_specs=col(0),
        scratch_shapes=[
            pltpu.VMEM((3, (DN_CONV - 1) * DN_CHUNK, 2 * DN_CHUNK), BF16),
            pltpu.VMEM((nc, DN_CHUNK, width), BF16),
            pltpu.VMEM((nc, DN_CHUNK, width), BF16),
            pltpu.VMEM((nc, DN_CHUNK, width), BF16),
            pltpu.VMEM((nc, width, DN_CHUNK), BF16),
            pltpu.VMEM((nc, DN_CHUNK, width), F32),
            pltpu.VMEM((N_DIRS * DN_HEADS, DN_HEAD_DIM, DN_HEAD_DIM), F32),
            pltpu.VMEM((DN_GROUP * N_DIRS * DN_HEADS, DN_CHUNK, DN_HEAD_DIM), F32),
            pltpu.VMEM((DN_GROUP * N_DIRS * DN_HEADS, DN_CHUNK, DN_HEAD_DIM), BF16),
            pltpu.VMEM((DN_GROUP * N_DIRS * DN_HEADS, DN_CHUNK, 2 * DN_CHUNK), BF16),
            pltpu.VMEM((DN_GROUP * N_DIRS * DN_HEADS, DN_HEAD_DIM, DN_CHUNK), BF16),
            pltpu.VMEM((DN_GROUP * N_DIRS * DN_HEADS, 8, LANES), F32),
            pltpu.VMEM((nc, n_ab, DN_CHUNK), F32),
            pltpu.VMEM((nc, n_ab, DN_CHUNK), F32),
            pltpu.VMEM((nc, DN_CHUNK, LANES), F32),
            pltpu.VMEM((nc, n_ab, DN_CHUNK), F32),
        ],
        compiler_params=_cparams(1),
        name="deltanet",
    )(dqkv, dqkv, dqkv, gate, abr, conv_w, alog_c, dtb_c, gout)


def _mem_kv_kernel(mem_ref, gain_ref, w_ref, k_ref, v_ref):
    d = k_ref.shape[1]
    mn = _rms(mem_ref[...], gain_ref[...]).astype(BF16)
    k_ref[...] = jnp.dot(mn, w_ref[:, :d], preferred_element_type=F32).astype(BF16)
    v_ref[...] = jnp.dot(mn, w_ref[:, d:], preferred_element_type=F32).astype(BF16)


def _mem_kv(mem2, gain, w_ckv, mem_len):
    rows, d = mem2.shape
    spec = pl.BlockSpec((mem_len, d), lambda b: (b, 0))
    return pl.pallas_call(
        _mem_kv_kernel,
        out_shape=(jax.ShapeDtypeStruct((rows, d), BF16),) * 2,
        grid=(rows // mem_len,),
        in_specs=[spec, _const_spec((1, d)), _const_spec(w_ckv.shape)],
        out_specs=(spec, spec),
        compiler_params=_cparams(1),
        name="mem_kv",
    )(mem2, gain, w_ckv)


def _out_cross_kernel(x_ref, ao_ref, do_ref, wout_ref, gpost_ref, gpre_ref, wq_ref, km_ref, vm_ref,
                      wo_ref, gcpost_ref, y_ref):
    aw = ao_ref.shape[1]
    d = x_ref.shape[1]
    chd = d // CROSS_HEADS
    nt = (((1,), (1,)), ((), ()))

    def rows_chain(r0):
        rs = slice(r0, r0 + ROW_SUB)
        mix = (jnp.dot(ao_ref[rs, :], wout_ref[:aw, :], preferred_element_type=F32)
               + jnp.dot(do_ref[rs, :], wout_ref[aw:, :], preferred_element_type=F32))
        yield
        x1 = x_ref[rs, :] + _rms(mix, gpost_ref[...])
        q = jnp.dot(_rms(x1, gpre_ref[...]).astype(BF16), wq_ref[...], preferred_element_type=F32)
        yield
        q = (q * (chd ** -0.5)).astype(BF16)
        hcols = [slice(h * chd, (h + 1) * chd) for h in range(CROSS_HEADS)]
        scores = [lax.dot_general(q[:, hs], km_ref[:, hs], nt, preferred_element_type=F32) for hs in hcols]
        yield
        probs = [jnp.exp(s - jnp.max(s, axis=-1, keepdims=True)) for s in scores]
        outs = [jnp.dot(p.astype(BF16), vm_ref[:, hs], preferred_element_type=F32) for p, hs in zip(probs, hcols)]
        yield
        heads = [(o / jnp.sum(p, axis=-1, keepdims=True)).astype(BF16) for o, p in zip(outs, probs)]
        cproj = jnp.dot(jnp.concatenate(heads, axis=1), wo_ref[...], preferred_element_type=F32)
        yield
        y_ref[rs, :] = x1 + _rms(cproj, gcpost_ref[...])

    _lockstep([rows_chain(r0) for r0 in range(0, x_ref.shape[0], ROW_SUB)])


def _out_cross(x2, attn_o, dn_o, w_out, g_post, g_pre, w_cq, kmem, vmem, w_co, g_cpost, seq, mem_len, tm):
    t, d = x2.shape
    per_batch = seq // tm
    row = lambda w: pl.BlockSpec((tm, w), lambda i: (i, 0))
    memspec = pl.BlockSpec((mem_len, d), lambda i: (i // per_batch, 0))
    vec = _const_spec((1, d))
    return pl.pallas_call(
        _out_cross_kernel,
        out_shape=jax.ShapeDtypeStruct((t, d), F32),
        grid=(t // tm,),
        in_specs=[row(d), row(attn_o.shape[1]), row(dn_o.shape[1]), _const_spec(w_out.shape), vec, vec,
                  _const_spec(w_cq.shape), memspec, memspec, _const_spec(w_co.shape), vec],
        out_specs=row(d),
        compiler_params=_cparams(1),
        name="out_cross",
    )(x2, attn_o, dn_o, w_out, g_post, g_pre, w_cq, kmem, vmem, w_co, g_cpost)


def _ffn_kernel(x_ref, gpre_ref, wgu_ref, wdown_ref, gpost_ref, y_ref):
    dff = wdown_ref.shape[0]

    def rows_chain(r0):
        rs = slice(r0, r0 + ROW_SUB)
        hf = _rms(x_ref[rs, :], gpre_ref[...]).astype(BF16)
        gu = jnp.dot(hf, wgu_ref[...], preferred_element_type=F32)
        yield
        act = (_silu(gu[:, :dff]) * gu[:, dff:]).astype(BF16)
        f = jnp.dot(act, wdown_ref[...], preferred_element_type=F32)
        yield
        y_ref[rs, :] = x_ref[rs, :] + _rms(f, gpost_ref[...])

    _lockstep([rows_chain(r0) for r0 in range(0, x_ref.shape[0], ROW_SUB)])


def _ffn(x2, g_pre, w_gu, w_down, g_post, tm):
    t, d = x2.shape
    row = pl.BlockSpec((tm, d), lambda i: (i, 0))
    vec = _const_spec((1, d))
    return pl.pallas_call(
        _ffn_kernel,
        out_shape=jax.ShapeDtypeStruct((t, d), F32),
        grid=(t // tm,),
        in_specs=[row, vec, _const_spec(w_gu.shape), _const_spec(w_down.shape), vec],
        out_specs=row,
        compiler_params=_cparams(1),
        name="ffn",
    )(x2, g_pre, w_gu, w_down, g_post)


def _layer(x, mem, positions, g_mix_pre, w_in, conv_w, a_log, dt_bias, g_dn_out, attn_sink, w_out,
           g_mix_post, g_cross_pre, g_mem, w_cq, w_ckv, w_co, g_cross_post, g_ffn_pre, w_gate_up,
           w_down, g_ffn_post):
    batch, seq, d = x.shape
    mem_len = mem.shape[1]
    t = batch * seq
    x2 = x.reshape(t, d)
    vec = lambda g: g.reshape(1, -1).astype(F32)

    half = ATTN_HEAD_DIM // 2
    inv_freq = ROPE_THETA ** (-jnp.arange(half, dtype=F32) / half)
    invf_row = jnp.tile(inv_freq, LANES // half).reshape(1, LANES)

    n_ab = 2 * N_DIRS * DN_HEADS
    w_all = jnp.pad(w_in, ((0, 0), (0, LANES - n_ab))).astype(BF16)
    q, k, v, dqkv, dgate, abr = _in_proj(x2, positions.reshape(t // LANES, LANES), invf_row, vec(g_mix_pre),
                                              w_all, n_ab, tm=min(1024, t))

    attn_o = _win_attn(attn_sink.astype(F32), q, k, v, batch, seq, nq=min(4, seq // ATTN_BLOCK))

    pad_c = lambda p: jnp.pad(p.reshape(-1, 1).astype(F32), ((0, n_ab - N_DIRS * DN_HEADS), (0, 0)))
    dn_o = _deltanet(dqkv, dgate, abr, conv_w.astype(F32), pad_c(a_log), pad_c(dt_bias), vec(g_dn_out), batch, seq)

    kmem, vmem = _mem_kv(mem.reshape(batch * mem_len, d), vec(g_mem), w_ckv.astype(BF16), mem_len)
    x3 = _out_cross(x2, attn_o, dn_o, w_out.astype(BF16), vec(g_mix_post), vec(g_cross_pre),
                    w_cq.astype(BF16), kmem, vmem, w_co.astype(BF16), vec(g_cross_post), seq, mem_len, tm=min(1024, seq))
    y = _ffn(x3, vec(g_ffn_pre), w_gate_up.astype(BF16), w_down.astype(BF16), vec(g_ffn_post), tm=512)
    return y.reshape(batch, seq, d)


def kernel(x, mem, positions, g_mix_pre, w_in, conv_w, a_log, dt_bias, g_dn_out, attn_sink, w_out,
           g_mix_post, g_cross_pre, g_mem, w_cq, w_ckv, w_co, g_cross_post, g_ffn_pre, w_gate_up,
           w_down, g_ffn_post):
    depth = w_in.shape[0]
    for l in range(depth):
        x = _layer(x, mem, positions, g_mix_pre[l], w_in[l], conv_w[l], a_log[l], dt_bias[l], g_dn_out[l],
                   attn_sink[l], w_out[l], g_mix_post[l], g_cross_pre[l], g_mem[l], w_cq[l], w_ckv[l],
                   w_co[l], g_cross_post[l], g_ffn_pre[l], w_gate_up[l], w_down[l], g_ffn_post[l])
    return x
```

```python
import functools

import jax
import jax.numpy as jnp
from jax import lax
from jax.experimental import pallas as pl
from jax.experimental.pallas import tpu as pltpu

F32 = jnp.float32
BF16 = jnp.bfloat16

EPS = 1e-6
LOG2E = 1.4426950408889634
ROPE_THETA = 10000.0
ATTN_HEADS = 8
ATTN_KV_HEADS = 2
ATTN_HEAD_DIM = 64
ATTN_BLOCK = 128
DN_HEADS = 4
DN_HEAD_DIM = 128
DN_CONV = 5
N_DIRS = 2
CROSS_HEADS = 4

LANES = 128
DN_CHUNK = 128
ROW_SUB = 256
DN_GROUP = 2
TRI_BASE = 16
VMEM_LIMIT = 56 * 1024 * 1024


def _cparams(n_grid_dims):
    return pltpu.CompilerParams(
        dimension_semantics=("arbitrary",) * n_grid_dims,
        vmem_limit_bytes=VMEM_LIMIT,
    )


def _rms(x, gain):
    return x * lax.rsqrt(jnp.mean(x * x, axis=-1, keepdims=True) + EPS) * gain


def _silu(x):
    return x * jax.nn.sigmoid(x)


def _softplus(x):
    return jnp.maximum(x, 0.0) + jnp.log(1.0 + jnp.exp(-jnp.abs(x)))


def _cast_once(src_ref, dst_ref, rows=256):
    @pl.when(pl.program_id(0) == 0)
    def _():
        for r in range(0, src_ref.shape[0], rows):
            dst_ref[r:r + rows, :] = src_ref[r:r + rows, :].astype(BF16)


def _const_spec(shape):
    nd = len(shape)
    return pl.BlockSpec(shape, lambda *_: (0,) * nd, pipeline_mode=pl.Buffered(1))


def _in_proj_kernel(x_ref, pos_ref, invf_ref, gain_ref, wt_ref,
                    q_ref, k_ref, v_ref, dqkv_ref, gate_ref, abr_ref, w_ref):
    half = ATTN_HEAD_DIM // 2
    lane = lax.broadcasted_iota(jnp.int32, (1, LANES), 1)
    first_half = (lane % ATTN_HEAD_DIM) < half
    aw = ATTN_HEADS * ATTN_HEAD_DIM
    kvw = ATTN_KV_HEADS * ATTN_HEAD_DIM
    off = aw + 2 * kvw
    dn = dqkv_ref.shape[1]
    per_sub = ROW_SUB // DN_CHUNK
    n_main = off + dn + gate_ref.shape[1]

    @pl.when(pl.program_id(0) == 0)
    def _():
        for n in range(n_main // LANES):
            cols = slice(n * LANES, (n + 1) * LANES)
            w_ref[:, cols] = wt_ref[cols, :].T.astype(BF16)

    w_abt = wt_ref[n_main:, :].astype(BF16)

    def rows_chain(r0):
        rs = slice(r0, r0 + ROW_SUB)
        h = _rms(x_ref[rs, :], gain_ref[...]).astype(BF16)
        proj = lambda lo, hi: jnp.dot(h, w_ref[:, lo:hi], preferred_element_type=F32)
        q = proj(0, aw)
        kv = proj(aw, off)
        dqkv = proj(off, off + dn)
        gate = proj(off + dn, off + dn + gate_ref.shape[1])
        abr = lax.dot_general(w_abt, h, (((1,), (1,)), ((), ())), preferred_element_type=F32)
        yield
        pos = jnp.concatenate(
            [jnp.broadcast_to(pos_ref[g:g + 1, :].astype(F32), (LANES, LANES)).T
             for g in range(r0 // LANES, (r0 + ROW_SUB) // LANES)], axis=0)
        ang = pos * invf_ref[...]
        cos = jnp.cos(ang)
        sin = jnp.sin(ang)

        def rope(t):
            rot = jnp.where(first_half, -pltpu.roll(t, LANES - half, 1), pltpu.roll(t, half, 1))
            return t * cos + rot * sin

        for m in range(aw // LANES):
            cols = slice(m * LANES, (m + 1) * LANES)
            q_ref[rs, cols] = (rope(q[:, cols]) * (ATTN_HEAD_DIM ** -0.5 * LOG2E)).astype(BF16)
        k_ref[rs, :] = rope(kv[:, :kvw]).astype(BF16)
        v_ref[rs, :] = kv[:, kvw:].astype(BF16)
        dqkv_ref[rs, :] = dqkv.astype(BF16)
        gate_ref[rs, :] = gate.astype(BF16)
        for j in range(per_sub):
            abr_ref[r0 // DN_CHUNK + j] = abr[:, j * DN_CHUNK:(j + 1) * DN_CHUNK]

    _lockstep([rows_chain(r0) for r0 in range(0, x_ref.shape[0], ROW_SUB)])


def _in_proj(x2, pos_rows, invf_row, gain, w_t, n_ab, tm):
    t, d = x2.shape
    aw = ATTN_HEADS * ATTN_HEAD_DIM
    kvw = ATTN_KV_HEADS * ATTN_HEAD_DIM
    dnw = DN_HEADS * DN_HEAD_DIM
    row = lambda w: pl.BlockSpec((tm, w), lambda i: (i, 0))
    out_shape = (
        jax.ShapeDtypeStruct((t, aw), BF16),
        jax.ShapeDtypeStruct((t, kvw), BF16),
        jax.ShapeDtypeStruct((t, kvw), BF16),
        jax.ShapeDtypeStruct((t, 3 * dnw), BF16),
        jax.ShapeDtypeStruct((t, dnw), BF16),
        jax.ShapeDtypeStruct((t // DN_CHUNK, n_ab, DN_CHUNK), F32),
    )
    out_specs = (
        row(aw), row(kvw), row(kvw), row(3 * dnw), row(dnw),
        pl.BlockSpec((tm // DN_CHUNK, n_ab, DN_CHUNK), lambda i: (i, 0, 0)),
    )
    return pl.pallas_call(
        _in_proj_kernel,
        out_shape=out_shape,
        grid=(t // tm,),
        in_specs=[row(d), pl.BlockSpec((tm // LANES, LANES), lambda i: (i, 0)), _const_spec((1, LANES)),
                  _const_spec((1, d)), _const_spec(w_t.shape)],
        out_specs=out_specs,
        scratch_shapes=[pltpu.VMEM((d, w_t.shape[0] - n_ab), BF16)],
        compiler_params=_cparams(1),
        name="in_proj",
    )(x2, pos_rows, invf_row, gain, w_t)


def _paired_dots(reqs):
    out = [None] * len(reqs)
    todo = [i for i, r in enumerate(reqs) if r is not None]
    while todo:
        i = todo.pop(0)
        l1, r1 = reqs[i]
        j = next((t for t in todo if reqs[t][0].shape == l1.shape and reqs[t][1].shape == r1.shape), None)
        if j is None or r1.shape[1] != LANES:
            out[i] = jnp.dot(l1, r1, preferred_element_type=F32)
            continue
        todo.remove(j)
        l2, r2 = reqs[j]
        z = jnp.zeros_like(r1)
        rhs = jnp.concatenate([jnp.concatenate([r1, z], axis=1), jnp.concatenate([z, r2], axis=1)], axis=0)
        both = jnp.dot(jnp.concatenate([l1, l2], axis=1), rhs, preferred_element_type=F32)
        out[i] = both[:, :LANES]
        out[j] = both[:, LANES:]
    return out


def _lockstep(gens):
    live = list(gens)
    sends = [None] * len(live)
    while live:
        reqs, still = [], []
        for g, val in zip(live, sends):
            try:
                reqs.append(g.send(val))
                still.append(g)
            except StopIteration:
                pass
        live = still
        sends = _paired_dots(reqs)


def _win_attn_kernel(sink_ref, q_ref, kp_ref, kc_ref, kn_ref, vp_ref, vc_ref, vn_ref, o_ref):
    n = pl.program_id(1)
    ns = pl.num_programs(1)
    blk = ATTN_BLOCK
    hd = ATTN_HEAD_DIM
    nq = q_ref.shape[0] // blk
    group = ATTN_HEADS // ATTN_KV_HEADS
    lane = lax.broadcasted_iota(jnp.int32, (1, LANES), 1)
    lo = lane < hd

    def halves(x):
        swapped = pltpu.roll(x.astype(F32), hd, 1).astype(BF16)
        z = jnp.zeros_like(x)
        return ((jnp.where(lo, x, z), jnp.where(lo, z, swapped)),
                (jnp.where(lo, swapped, z), jnp.where(lo, z, x)))

    kh = [halves(kp_ref[...])] + [halves(kc_ref[i * blk:(i + 1) * blk, :]) for i in range(nq)] + [halves(kn_ref[...])]
    vh = [halves(vp_ref[...])] + [halves(vc_ref[i * blk:(i + 1) * blk, :]) for i in range(nq)] + [halves(vn_ref[...])]

    def blockdiag(parts, i, g):
        return jnp.concatenate([parts[i + w][g][e] for e in range(2) for w in range(3)], axis=0)

    qi = lax.broadcasted_iota(jnp.int32, (blk, blk), 0)
    ki = lax.broadcasted_iota(jnp.int32, (blk, blk), 1)
    ninf = jnp.float32(-jnp.inf)
    zer = jnp.zeros((blk, blk), F32)

    def band_bias(i):
        has_prev = jnp.where(n > 0, 0.0, ninf) if i == 0 else jnp.float32(0.0)
        has_next = jnp.where(n < ns - 1, 0.0, ninf) if i == nq - 1 else jnp.float32(0.0)
        return jnp.concatenate([jnp.where(ki >= qi, has_prev, ninf), zer,
                                jnp.where(ki <= qi, has_next, ninf)], axis=1)

    def chain(i, m, kbd, vbd, bias):
        rows = slice(i * blk, (i + 1) * blk)
        cols = slice(m * LANES, (m + 1) * LANES)
        s = lax.dot_general(q_ref[rows, cols], kbd, (((1,), (1,)), ((), ())), preferred_element_type=F32)
        yield
        ps, ds = [], []
        for e in range(2):
            sk = sink_ref[2 * m + e] * LOG2E
            se = s[:, e * 3 * blk:(e + 1) * 3 * blk] + bias
            mx = jnp.maximum(jnp.max(se, axis=-1, keepdims=True), sk)
            p = jnp.exp2(se - mx)
            ds.append(jnp.sum(p, axis=-1, keepdims=True) + jnp.exp2(sk - mx))
            ps.append(p.astype(BF16))
        o = jnp.dot(jnp.concatenate(ps, axis=1), vbd, preferred_element_type=F32)
        yield
        o_ref[rows, cols] = (o / jnp.where(lo, ds[0], ds[1])).astype(BF16)

    chains = []
    for i in range(nq):
        bias = band_bias(i)
        for g in range(ATTN_KV_HEADS):
            kbd = blockdiag(kh, i, g)
            vbd = blockdiag(vh, i, g)
            for m in range(g * group // 2, (g + 1) * group // 2):
                chains.append(chain(i, m, kbd, vbd, bias))
    _lockstep(chains)


def _win_attn(sink, q, k, v, batch, seq, nq):
    nb = seq // ATTN_BLOCK
    ns = nb // nq
    aw = q.shape[1]
    kvw = k.shape[1]
    cur = lambda b, n: (b * ns + n, 0)
    prev = lambda b, n: (b * nb + jnp.maximum(n * nq - 1, 0), 0)
    nxt = lambda b, n: (b * nb + jnp.minimum(n * nq + nq, nb - 1), 0)
    edge = lambda f: pl.BlockSpec((ATTN_BLOCK, kvw), f)
    own = pl.BlockSpec((nq * ATTN_BLOCK, kvw), cur)
    return pl.pallas_call(
        _win_attn_kernel,
        out_shape=jax.ShapeDtypeStruct(q.shape, BF16),
        grid=(batch, ns),
        in_specs=[pl.BlockSpec(memory_space=pltpu.SMEM),
                  pl.BlockSpec((nq * ATTN_BLOCK, aw), cur),
                  edge(prev), own, edge(nxt), edge(prev), own, edge(nxt)],
        out_specs=pl.BlockSpec((nq * ATTN_BLOCK, aw), cur),
        compiler_params=_cparams(2),
        name="win_attn",
    )(sink, q, k, k, k, v, v, v)


def _cumsum_both(x, axis):
    n = x.shape[axis]
    idx = lax.broadcasted_iota(jnp.int32, x.shape, axis)
    pre, suf = x, x
    s = 1
    while s < n:
        pre = pre + jnp.where(idx >= s, pltpu.roll(pre, s, axis), 0.0)
        suf = suf + jnp.where(idx < n - s, pltpu.roll(suf, n - s, axis), 0.0)
        s *= 2
    return pre, suf


def _unit_tri_inverse(nmat, rc_xor, lower):
    c = nmat.shape[0]
    n0 = jnp.where(rc_xor < TRI_BASE, nmat, 0.0)
    nb16 = n0.astype(BF16)
    p = yield (nb16, nb16)
    y = n0
    n_sq = (TRI_BASE - 1).bit_length() - 1
    for m in range(n_sq):
        pb = p.astype(BF16)
        if m < n_sq - 1:
            r2 = jnp.dot(pb, jnp.concatenate([y.astype(BF16), pb], axis=1), preferred_element_type=F32)
            yield
            y = y + p + r2[:, :c]
            p = r2[:, c:]
        else:
            py = yield (pb, y.astype(BF16))
            y = y + p + py
    neg_n = -nmat
    blk = TRI_BASE
    while blk < c:
        first = blk if lower else 0
        held = [slice(s, s + blk) for s in range(first, c, 2 * blk)]
        kept = [slice(s, s + blk) for s in range(blk - first, c, 2 * blk)]
        take = lambda a: jnp.concatenate([a[s] for s in held], axis=0)
        rx = take(rc_xor)
        l_rows = jnp.where((rx >= blk) & (rx < 2 * blk), take(neg_n), 0.0)
        y_rows = take(y)
        ly = yield (l_rows.astype(BF16), y.astype(BF16))
        u_rows = l_rows + ly
        ub = u_rows.astype(BF16)
        zero = jnp.zeros((blk, c), BF16)
        pieces = [ub[i * blk:(i + 1) * blk] for i in range(len(held))]
        order = [zero, None] if lower else [None, zero]
        u_full = jnp.concatenate([(pieces[i] if o is None else o) for i in range(len(held)) for o in order], axis=0)
        yu = yield (y_rows.astype(BF16), u_full)
        new_rows = y_rows - u_rows - yu
        parts = {}
        for i, s in enumerate(held):
            parts[s.start] = new_rows[i * blk:(i + 1) * blk]
        for s in kept:
            parts[s.start] = y[s]
        y = jnp.concatenate([parts[k] for k in sorted(parts)], axis=0)
        blk *= 2
    return y


def _deltanet_kernel(dq_ref, dk_ref, dv_ref, gate_ref, abr_ref, cw_ref, alog_ref, dtb_ref, gout_ref,
                     o_ref, shifts, qn, kn, vn, kt, oacc, st, pu, pw, pm2, pkte, pgl, gcum, gbeta, gcols, gsum):
    seq = dq_ref.shape[0]
    nc = seq // DN_CHUNK
    c = DN_CHUNK
    hd = DN_HEAD_DIM
    width = DN_HEADS * hd

    win = shifts.shape[2]
    pad = DN_CONV // 2
    lead = (win - c) // 2
    side_taps = [j for j in range(DN_CONV) if j != pad]

    @pl.when(pl.program_id(0) == 0)
    def _():
        tok = lax.broadcasted_iota(jnp.int32, (c, win), 0)
        src = lax.broadcasted_iota(jnp.int32, (c, win), 1)
        for v in range(shifts.shape[0]):
            shifts[v] = jnp.concatenate(
                [jnp.where(src == tok + (j - pad + v * lead), 1.0, 0.0).astype(BF16) for j in side_taps], axis=0)

    def conv_chunk(i):
        r = pl.multiple_of(i * c, c)
        start = pl.multiple_of(jnp.clip(r - lead, 0, seq - win), lead)
        sel = shifts[(r - start) // lead]

        def one(src_ref, w_off, dst, l2, scale):
            taps = jnp.dot(sel, src_ref[pl.ds(start, win), :], preferred_element_type=F32)
            yield
            for h in range(DN_HEADS):
                hs = slice(h * hd, (h + 1) * hd)
                wrow = lambda j: cw_ref[j:j + 1, w_off + h * hd:w_off + (h + 1) * hd]
                y = wrow(pad) * src_ref[pl.ds(r, c), hs].astype(F32)
                for n, j in enumerate(side_taps):
                    y = y + wrow(j) * taps[n * c:(n + 1) * c, hs]
                y = _silu(y)
                if l2:
                    y = y * lax.rsqrt(jnp.sum(y * y, axis=-1, keepdims=True) + EPS)
                    if scale != 1.0:
                        y = y * scale
                dst[i, :, hs] = y.astype(BF16)
                if dst is kn:
                    kt[i, hs, :] = y.T.astype(BF16)

        return [one(dq_ref, 0, qn, True, hd ** -0.5), one(dk_ref, width, kn, True, 1.0),
                one(dv_ref, 2 * width, vn, False, 1.0)]

    def conv_pair(i, carry):
        _lockstep(conv_chunk(2 * i) + conv_chunk(2 * i + 1))
        return carry

    lax.fori_loop(0, nc // 2, conv_pair, 0)

    st[...] = jnp.zeros(st.shape, F32)
    row = lax.broadcasted_iota(jnp.int32, (c, c), 0)
    col = lax.broadcasted_iota(jnp.int32, (c, c), 1)
    eye = row == col
    masks = ((row >= col, row > col), (row <= col, row < col))
    nh = DN_HEADS

    n_g = N_DIRS * nh
    n_r = abr_ref.shape[1]
    ab = jnp.concatenate([abr_ref[ci] for ci in range(nc)], axis=0)
    g = -jnp.exp(jnp.tile(alog_ref[...], (nc, 1))) * _softplus(ab + jnp.tile(dtb_ref[...], (nc, 1)))
    beta = jax.nn.sigmoid(ab)
    pre, suf = _cumsum_both(g, 1)
    cum = jnp.where(lax.broadcasted_iota(jnp.int32, g.shape, 0) % n_r < nh, pre, suf)
    gtot_all = jnp.broadcast_to(jnp.sum(g, axis=1, keepdims=True), g.shape)
    for ci in range(nc):
        rows = slice(ci * n_r, (ci + 1) * n_r)
        gcum[ci] = cum[rows]
        gbeta[ci] = beta[rows]
        gsum[ci] = gtot_all[rows]
        packed = jnp.concatenate([cum[ci * n_r:ci * n_r + n_g], -beta[ci * n_r + n_g:(ci + 1) * n_r],
                                  jnp.zeros((c - 2 * n_g, c), F32)], axis=0)
        gcols[ci] = packed.T

    def prep(ci, h, d, slot):
        j = d * nh + h
        jb = N_DIRS * nh + j
        gc_c = gcols[ci, :, j:j + 1]
        gc_r = gcum[ci, j:j + 1, :]
        neg_b_c = gcols[ci, :, jb:jb + 1]
        b_r = gbeta[ci, jb:jb + 1, :]
        gtot = gsum[ci, j:j + 1, 0:1]
        incl, strict = masks[d]
        hs = slice(h * hd, (h + 1) * hd)
        ktc = kt[ci, hs, :]
        qk2 = jnp.concatenate([qn[ci, :, hs], kn[ci, :, hs]], axis=0)
        gram = yield (qk2, ktc)
        decay = jnp.exp(jnp.where(incl, gc_c - gc_r, -jnp.inf))
        nmat = jnp.where(strict, gram[c:] * neg_b_c * decay, 0.0)
        y = yield from _unit_tri_inverse(nmat, row ^ col, lower=(d == 0))
        tinv = y + jnp.where(eye, 1.0, 0.0)
        eg_r = jnp.exp(gc_r)
        z = jnp.zeros((c, hd), BF16)
        vk = jnp.concatenate([jnp.concatenate([vn[ci, :, hs], z], axis=1),
                              jnp.concatenate([z, kn[ci, :, hs]], axis=1)], axis=0)
        uw = jnp.dot(jnp.concatenate([tinv * b_r, tinv * (b_r * eg_r)], axis=1).astype(BF16), vk,
                     preferred_element_type=F32)
        yield
        pu[slot] = uw[:, :hd]
        pw[slot] = uw[:, hd:].astype(BF16)
        pm2[slot] = jnp.concatenate([jnp.where(eye, eg_r, 0.0), gram[:c] * decay], axis=1).astype(BF16)
        pkte[slot] = (ktc.astype(F32) * jnp.exp(gtot - gc_r)).astype(BF16)
        pgl[slot] = jnp.broadcast_to(jnp.exp(gtot), pgl.shape[1:])

    def scan(ci, h, d, slot):
        j = d * nh + h
        hs = slice(h * hd, (h + 1) * hd)
        wq = jnp.concatenate([pw[slot], qn[ci, :, hs]], axis=0)
        pm = yield (wq, st[j].astype(BF16))
        vb = (pu[slot] - pm[:c]).astype(BF16)
        o = jnp.dot(pm2[slot], jnp.concatenate([pm[c:].astype(BF16), vb], axis=0), preferred_element_type=F32)
        ds = yield (pkte[slot], vb)
        st[j] = st[j] * pgl[slot][0:1, :] + ds
        oacc[ci, :, hs] = oacc[ci, :, hs] + o

    def chunk_of(step_idx, d):
        return step_idx if d == 0 else nc - 1 - step_idx

    n_chain = N_DIRS * nh
    group = pm2.shape[0] // n_chain

    def preps(step0):
        return [prep(chunk_of(step0 + g, d), h, d, g * n_chain + d * nh + h)
                for g in range(group) for d in range(N_DIRS) for h in range(nh)]

    def scans(step0):
        def run(h, d):
            for g in range(group):
                yield from scan(chunk_of(step0 + g, d), h, d, g * n_chain + d * nh + h)
        return [run(h, d) for d in range(N_DIRS) for h in range(nh)]

    oacc[...] = jnp.zeros(oacc.shape, F32)
    _lockstep(preps(0))

    def group_of_steps(i, carry):
        s0 = i * group
        _lockstep(scans(s0))
        _lockstep(preps(s0 + group))
        return carry

    lax.fori_loop(0, nc // group - 1, group_of_steps, 0)
    _lockstep(scans(nc - group))

    def finish(ci, carry):
        r = pl.multiple_of(ci * c, c)
        for h in range(nh):
            hs = slice(h * hd, (h + 1) * hd)
            gate = gate_ref[pl.ds(r, c), hs].astype(F32)
            o_ref[pl.ds(r, c), hs] = (_rms(oacc[ci, :, hs], gout_ref[...]) * _silu(gate)).astype(BF16)
        return carry

    lax.fori_loop(0, nc, finish, 0)


def _deltanet(dqkv, gate, abr, conv_w, alog_c, dtb_c, gout, batch, seq):
    width = DN_HEADS * DN_HEAD_DIM
    nc = seq // DN_CHUNK
    n_ab = abr.shape[1]
    col = lambda j: pl.BlockSpec((seq, width), lambda b, j=j: (b, j))
    return pl.pallas_call(
        _deltanet_kernel,
        out_shape=jax.ShapeDtypeStruct((batch * seq, width), BF16),
        grid=(batch,),
        in_specs=[col(0), col(1), col(2), col(0),
                  pl.BlockSpec((nc, n_ab, DN_CHUNK), lambda b: (b, 0, 0)),
                  _const_spec(conv_w.shape),
                  _const_spec(alog_c.shape), _const_spec(dtb_c.shape),
                  _const_spec(gout.shape)],
        out_specs=col(0),
        scratch_shapes=[
            pltpu.VMEM((3, (DN_CONV - 1) * DN_CHUNK, 2 * DN_CHUNK), BF16),
            pltpu.VMEM((nc, DN_CHUNK, width), BF16),
            pltpu.VMEM((nc, DN_CHUNK, width), BF16),
            pltpu.VMEM((nc, DN_CHUNK, width), BF16),
            pltpu.VMEM((nc, width, DN_CHUNK), BF16),
            pltpu.VMEM((nc, DN_CHUNK, width), F32),
            pltpu.VMEM((N_DIRS * DN_HEADS, DN_HEAD_DIM, DN_HEAD_DIM), F32),
            pltpu.VMEM((DN_GROUP * N_DIRS * DN_HEADS, DN_CHUNK, DN_HEAD_DIM), F32),
            pltpu.VMEM((DN_GROUP * N_DIRS * DN_HEADS, DN_CHUNK, DN_HEAD_DIM), BF16),
            pltpu.VMEM((DN_GROUP * N_DIRS * DN_HEADS, DN_CHUNK, 2 * DN_CHUNK), BF16),
            pltpu.VMEM((DN_GROUP * N_DIRS * DN_HEADS, DN_HEAD_DIM, DN_CHUNK), BF16),
            pltpu.VMEM((DN_GROUP * N_DIRS * DN_HEADS, 8, LANES), F32),
            pltpu.VMEM((nc, n_ab, DN_CHUNK), F32),
            pltpu.VMEM((nc, n_ab, DN_CHUNK), F32),
            pltpu.VMEM((nc, DN_CHUNK, LANES), F32),
            pltpu.VMEM((nc, n_ab, DN_CHUNK), F32),
        ],
        compiler_params=_cparams(1),
        name="deltanet",
    )(dqkv, dqkv, dqkv, gate, abr, conv_w, alog_c, dtb_c, gout)


def _mem_kv_kernel(mem_ref, gain_ref, w32_ref, k_ref, v_ref, w_ref):
    d = k_ref.shape[1]
    _cast_once(w32_ref, w_ref)
    mn = _rms(mem_ref[...], gain_ref[...]).astype(BF16)
    k_ref[...] = jnp.dot(mn, w_ref[:, :d], preferred_element_type=F32).astype(BF16)
    v_ref[...] = jnp.dot(mn, w_ref[:, d:], preferred_element_type=F32).astype(BF16)


def _mem_kv(mem2, gain, w_ckv, mem_len):
    rows, d = mem2.shape
    spec = pl.BlockSpec((mem_len, d), lambda b: (b, 0))
    return pl.pallas_call(
        _mem_kv_kernel,
        out_shape=(jax.ShapeDtypeStruct((rows, d), BF16),) * 2,
        grid=(rows // mem_len,),
        in_specs=[spec, _const_spec((1, d)), _const_spec(w_ckv.shape)],
        out_specs=(spec, spec),
        scratch_shapes=[pltpu.VMEM(w_ckv.shape, BF16)],
        compiler_params=_cparams(1),
        name="mem_kv",
    )(mem2, gain, w_ckv)


def _out_cross_kernel(x_ref, ao_ref, do_ref, wout32_ref, gpost_ref, gpre_ref, wq32_ref, km_ref, vm_ref,
                      wo32_ref, gcpost_ref, y_ref, wout_ref, wq_ref, wo_ref):
    _cast_once(wout32_ref, wout_ref)
    _cast_once(wq32_ref, wq_ref)
    _cast_once(wo32_ref, wo_ref)
    aw = ao_ref.shape[1]
    d = x_ref.shape[1]
    chd = d // CROSS_HEADS
    nt = (((1,), (1,)), ((), ()))

    def rows_chain(r0):
        rs = slice(r0, r0 + ROW_SUB)
        mix = (jnp.dot(ao_ref[rs, :], wout_ref[:aw, :], preferred_element_type=F32)
               + jnp.dot(do_ref[rs, :], wout_ref[aw:, :], preferred_element_type=F32))
        yield
        x1 = x_ref[rs, :] + _rms(mix, gpost_ref[...])
        q = jnp.dot(_rms(x1, gpre_ref[...]).astype(BF16), wq_ref[...], preferred_element_type=F32)
        yield
        q = (q * (chd ** -0.5)).astype(BF16)
        hcols = [slice(h * chd, (h + 1) * chd) for h in range(CROSS_HEADS)]
        scores = [lax.dot_general(q[:, hs], km_ref[:, hs], nt, preferred_element_type=F32) for hs in hcols]
        yield
        probs = [jnp.exp(s - jnp.max(s, axis=-1, keepdims=True)) for s in scores]
        outs = [jnp.dot(p.astype(BF16), vm_ref[:, hs], preferred_element_type=F32) for p, hs in zip(probs, hcols)]
        yield
        heads = [(o / jnp.sum(p, axis=-1, keepdims=True)).astype(BF16) for o, p in zip(outs, probs)]
        cproj = jnp.dot(jnp.concatenate(heads, axis=1), wo_ref[...], preferred_element_type=F32)
        yield
        y_ref[rs, :] = x1 + _rms(cproj, gcpost_ref[...])

    _lockstep([rows_chain(r0) for r0 in range(0, x_ref.shape[0], ROW_SUB)])


def _out_cross(x2, attn_o, dn_o, w_out, g_post, g_pre, w_cq, kmem, vmem, w_co, g_cpost, seq, mem_len, tm):
    t, d = x2.shape
    per_batch = seq // tm
    row = lambda w: pl.BlockSpec((tm, w), lambda i: (i, 0))
    memspec = pl.BlockSpec((mem_len, d), lambda i: (i // per_batch, 0))
    vec = _const_spec((1, d))
    return pl.pallas_call(
        _out_cross_kernel,
        out_shape=jax.ShapeDtypeStruct((t, d), F32),
        grid=(t // tm,),
        in_specs=[row(d), row(attn_o.shape[1]), row(dn_o.shape[1]), _const_spec(w_out.shape), vec, vec,
                  _const_spec(w_cq.shape), memspec, memspec, _const_spec(w_co.shape), vec],
        out_specs=row(d),
        scratch_shapes=[pltpu.VMEM(w_out.shape, BF16), pltpu.VMEM(w_cq.shape, BF16), pltpu.VMEM(w_co.shape, BF16)],
        compiler_params=_cparams(1),
        name="out_cross",
    )(x2, attn_o, dn_o, w_out, g_post, g_pre, w_cq, kmem, vmem, w_co, g_cpost)


def _ffn_kernel(x_ref, gpre_ref, wgu_ref, wdown_ref, gpost_ref, y_ref):
    dff = wdown_ref.shape[0]

    def rows_chain(r0):
        rs = slice(r0, r0 + ROW_SUB)
        hf = _rms(x_ref[rs, :], gpre_ref[...]).astype(BF16)
        gu = jnp.dot(hf, wgu_ref[...], preferred_element_type=F32)
        yield
        act = (_silu(gu[:, :dff]) * gu[:, dff:]).astype(BF16)
        f = jnp.dot(act, wdown_ref[...], preferred_element_type=F32)
        yield
        y_ref[rs, :] = x_ref[rs, :] + _rms(f, gpost_ref[...])

    _lockstep([rows_chain(r0) for r0 in range(0, x_ref.shape[0], ROW_SUB)])


def _ffn(x2, g_pre, w_gu, w_down, g_post, tm):
    t, d = x2.shape
    row = pl.BlockSpec((tm, d), lambda i: (i, 0))
    vec = _const_spec((1, d))
    return pl.pallas_call(
        _ffn_kernel,
        out_shape=jax.ShapeDtypeStruct((t, d), F32),
        grid=(t // tm,),
        in_specs=[row, vec, _const_spec(w_gu.shape), _const_spec(w_down.shape), vec],
        out_specs=row,
        compiler_params=_cparams(1),
        name="ffn",
    )(x2, g_pre, w_gu, w_down, g_post)


def _layer(x, mem, positions, g_mix_pre, w_in, conv_w, a_log, dt_bias, g_dn_out, attn_sink, w_out,
           g_mix_post, g_cross_pre, g_mem, w_cq, w_ckv, w_co, g_cross_post, g_ffn_pre, w_gate_up,
           w_down, g_ffn_post):
    batch, seq, d = x.shape
    mem_len = mem.shape[1]
    t = batch * seq
    x2 = x.reshape(t, d)
    vec = lambda g: g.reshape(1, -1).astype(F32)

    half = ATTN_HEAD_DIM // 2
    inv_freq = ROPE_THETA ** (-jnp.arange(half, dtype=F32) / half)
    invf_row = jnp.tile(inv_freq, LANES // half).reshape(1, LANES)

    n_ab = 2 * N_DIRS * DN_HEADS
    w_t = w_in.astype(F32).T
    q, k, v, dqkv, dgate, abr = _in_proj(x2, positions.reshape(t // LANES, LANES), invf_row, vec(g_mix_pre),
                                              w_t, n_ab, tm=min(1024, t))

    attn_o = _win_attn(attn_sink.astype(F32), q, k, v, batch, seq, nq=min(4, seq // ATTN_BLOCK))

    pad_c = lambda p: jnp.pad(p.reshape(-1, 1).astype(F32), ((0, n_ab - N_DIRS * DN_HEADS), (0, 0)))
    dn_o = _deltanet(dqkv, dgate, abr, conv_w.astype(F32), pad_c(a_log), pad_c(dt_bias), vec(g_dn_out), batch, seq)

    kmem, vmem = _mem_kv(mem.reshape(batch * mem_len, d), vec(g_mem), w_ckv.astype(F32), mem_len)
    x3 = _out_cross(x2, attn_o, dn_o, w_out.astype(F32), vec(g_mix_post), vec(g_cross_pre),
                    w_cq.astype(F32), kmem, vmem, w_co.astype(F32), vec(g_cross_post), seq, mem_len, tm=min(1024, seq))
    y = _ffn(x3, vec(g_ffn_pre), w_gate_up.astype(BF16), w_down.astype(BF16), vec(g_ffn_post), tm=512)
    return y.reshape(batch, seq, d)


def kernel(x, mem, positions, g_mix_pre, w_in, conv_w, a_log, dt_bias, g_dn_out, attn_sink, w_out,
           g_mix_post, g_cross_pre, g_mem, w_cq, w_ckv, w_co, g_cross_post, g_ffn_pre, w_gate_up,
           w_down, g_ffn_post):
    depth = w_in.shape[0]
    for l in range(depth):
        x = _layer(x, mem, positions, g_mix_pre[l], w_in[l], conv_w[l], a_log[l], dt_bias[l], g_dn_out[l],
                   attn_sink[l], w_out[l], g_mix_post[l], g_cross_pre[l], g_mem[l], w_cq[l], w_ckv[l],
                   w_co[l], g_cross_post[l], g_ffn_pre[l], w_gate_up[l], w_down[l], g_ffn_post[l])
    return x
```

```python
import functools

import jax
import jax.numpy as jnp
from jax import lax
from jax.experimental import pallas as pl
from jax.experimental.pallas import tpu as pltpu

F32 = jnp.float32
BF16 = jnp.bfloat16

EPS = 1e-6
LOG2E = 1.4426950408889634
ROPE_THETA = 10000.0
ATTN_HEADS = 8
ATTN_KV_HEADS = 2
ATTN_HEAD_DIM = 64
ATTN_BLOCK = 128
DN_HEADS = 4
DN_HEAD_DIM = 128
DN_CONV = 5
N_DIRS = 2
CROSS_HEADS = 4

LANES = 128
DN_CHUNK = 128
WEIGHT_STAGE_ROWS = 128
ROW_SUB = 256
DN_GROUP = 2
TRI_BASE = 16
VMEM_LIMIT = 56 * 1024 * 1024


def _cparams(n_grid_dims):
    return pltpu.CompilerParams(
        dimension_semantics=("arbitrary",) * n_grid_dims,
        vmem_limit_bytes=VMEM_LIMIT,
    )


def _rms(x, gain):
    return x * lax.rsqrt(jnp.mean(x * x, axis=-1, keepdims=True) + EPS) * gain


def _silu(x):
    return x * jax.nn.sigmoid(x)


def _softplus(x):
    return jnp.maximum(x, 0.0) + jnp.log(1.0 + jnp.exp(-jnp.abs(x)))


def _cast_once(src_ref, dst_ref, rows=256):
    @pl.when(pl.program_id(0) == 0)
    def _():
        for r in range(0, src_ref.shape[0], rows):
            dst_ref[r:r + rows, :] = src_ref[r:r + rows, :].astype(BF16)


def _const_spec(shape):
    nd = len(shape)
    return pl.BlockSpec(shape, lambda *_: (0,) * nd, pipeline_mode=pl.Buffered(1))


def _in_proj_kernel(x_ref, pos_ref, invf_ref, gain_ref, wt_ref,
                    q_ref, k_ref, v_ref, dqkv_ref, gate_ref, abr_ref, w_ref):
    half = ATTN_HEAD_DIM // 2
    lane = lax.broadcasted_iota(jnp.int32, (1, LANES), 1)
    first_half = (lane % ATTN_HEAD_DIM) < half
    aw = ATTN_HEADS * ATTN_HEAD_DIM
    kvw = ATTN_KV_HEADS * ATTN_HEAD_DIM
    off = aw + 2 * kvw
    dn = dqkv_ref.shape[1]
    per_sub = ROW_SUB // DN_CHUNK
    n_main = off + dn + gate_ref.shape[1]

    @pl.when(pl.program_id(0) == 0)
    def _():
        for n in range(n_main // LANES):
            cols = slice(n * LANES, (n + 1) * LANES)
            w_ref[:, cols] = wt_ref[cols, :].T.astype(BF16)

    w_abt = wt_ref[n_main:, :].astype(BF16)

    def rows_chain(r0):
        rs = slice(r0, r0 + ROW_SUB)
        h = _rms(x_ref[rs, :], gain_ref[...]).astype(BF16)
        proj = lambda lo, hi: jnp.dot(h, w_ref[:, lo:hi], preferred_element_type=F32)
        q = proj(0, aw)
        kv = proj(aw, off)
        dqkv = proj(off, off + dn)
        gate = proj(off + dn, off + dn + gate_ref.shape[1])
        abr = lax.dot_general(w_abt, h, (((1,), (1,)), ((), ())), preferred_element_type=F32)
        yield
        pos = jnp.concatenate(
            [jnp.broadcast_to(pos_ref[g:g + 1, :].astype(F32), (LANES, LANES)).T
             for g in range(r0 // LANES, (r0 + ROW_SUB) // LANES)], axis=0)
        ang = pos * invf_ref[...]
        cos = jnp.cos(ang)
        sin = jnp.sin(ang)

        def rope(t):
            rot = jnp.where(first_half, -pltpu.roll(t, LANES - half, 1), pltpu.roll(t, half, 1))
            return t * cos + rot * sin

        for m in range(aw // LANES):
            cols = slice(m * LANES, (m + 1) * LANES)
            q_ref[rs, cols] = (rope(q[:, cols]) * (ATTN_HEAD_DIM ** -0.5 * LOG2E)).astype(BF16)
        k_ref[rs, :] = rope(kv[:, :kvw]).astype(BF16)
        v_ref[rs, :] = kv[:, kvw:].astype(BF16)
        dqkv_ref[rs, :] = dqkv.astype(BF16)
        gate_ref[rs, :] = gate.astype(BF16)
        for j in range(per_sub):
            abr_ref[r0 // DN_CHUNK + j] = abr[:, j * DN_CHUNK:(j + 1) * DN_CHUNK]

    _lockstep([rows_chain(r0) for r0 in range(0, x_ref.shape[0], ROW_SUB)])


def _in_proj(x2, pos_rows, invf_row, gain, w_t, n_ab, tm):
    t, d = x2.shape
    aw = ATTN_HEADS * ATTN_HEAD_DIM
    kvw = ATTN_KV_HEADS * ATTN_HEAD_DIM
    dnw = DN_HEADS * DN_HEAD_DIM
    row = lambda w: pl.BlockSpec((tm, w), lambda i: (i, 0))
    out_shape = (
        jax.ShapeDtypeStruct((t, aw), BF16),
        jax.ShapeDtypeStruct((t, kvw), BF16),
        jax.ShapeDtypeStruct((t, kvw), BF16),
        jax.ShapeDtypeStruct((t, 3 * dnw), BF16),
        jax.ShapeDtypeStruct((t, dnw), BF16),
        jax.ShapeDtypeStruct((t // DN_CHUNK, n_ab, DN_CHUNK), F32),
    )
    out_specs = (
        row(aw), row(kvw), row(kvw), row(3 * dnw), row(dnw),
        pl.BlockSpec((tm // DN_CHUNK, n_ab, DN_CHUNK), lambda i: (i, 0, 0)),
    )
    return pl.pallas_call(
        _in_proj_kernel,
        out_shape=out_shape,
        grid=(t // tm,),
        in_specs=[row(d), pl.BlockSpec((tm // LANES, LANES), lambda i: (i, 0)), _const_spec((1, LANES)),
                  _const_spec((1, d)), _const_spec(w_t.shape)],
        out_specs=out_specs,
        scratch_shapes=[pltpu.VMEM((d, w_t.shape[0] - n_ab), BF16)],
        compiler_params=_cparams(1),
        name="in_proj",
    )(x2, pos_rows, invf_row, gain, w_t)


def _paired_dots(reqs):
    out = [None] * len(reqs)
    todo = [i for i, r in enumerate(reqs) if r is not None]
    while todo:
        i = todo.pop(0)
        l1, r1 = reqs[i]
        j = next((t for t in todo if reqs[t][0].shape == l1.shape and reqs[t][1].shape == r1.shape), None)
        if j is None or r1.shape[1] != LANES:
            out[i] = jnp.dot(l1, r1, preferred_element_type=F32)
            continue
        todo.remove(j)
        l2, r2 = reqs[j]
        z = jnp.zeros_like(r1)
        rhs = jnp.concatenate([jnp.concatenate([r1, z], axis=1), jnp.concatenate([z, r2], axis=1)], axis=0)
        both = jnp.dot(jnp.concatenate([l1, l2], axis=1), rhs, preferred_element_type=F32)
        out[i] = both[:, :LANES]
        out[j] = both[:, LANES:]
    return out


def _lockstep(gens):
    live = list(gens)
    sends = [None] * len(live)
    while live:
        reqs, still = [], []
        for g, val in zip(live, sends):
            try:
                reqs.append(g.send(val))
                still.append(g)
            except StopIteration:
                pass
        live = still
        sends = _paired_dots(reqs)


def _win_attn_kernel(sink_ref, q_ref, kp_ref, kc_ref, kn_ref, vp_ref, vc_ref, vn_ref, o_ref):
    n = pl.program_id(1)
    ns = pl.num_programs(1)
    blk = ATTN_BLOCK
    hd = ATTN_HEAD_DIM
    nq = q_ref.shape[0] // blk
    group = ATTN_HEADS // ATTN_KV_HEADS
    lane = lax.broadcasted_iota(jnp.int32, (1, LANES), 1)
    lo = lane < hd

    def halves(x):
        swapped = pltpu.roll(x.astype(F32), hd, 1).astype(BF16)
        z = jnp.zeros_like(x)
        return ((jnp.where(lo, x, z), jnp.where(lo, z, swapped)),
                (jnp.where(lo, swapped, z), jnp.where(lo, z, x)))

    kh = [halves(kp_ref[...])] + [halves(kc_ref[i * blk:(i + 1) * blk, :]) for i in range(nq)] + [halves(kn_ref[...])]
    vh = [halves(vp_ref[...])] + [halves(vc_ref[i * blk:(i + 1) * blk, :]) for i in range(nq)] + [halves(vn_ref[...])]

    def blockdiag(parts, i, g):
        return jnp.concatenate([parts[i + w][g][e] for e in range(2) for w in range(3)], axis=0)

    qi = lax.broadcasted_iota(jnp.int32, (blk, blk), 0)
    ki = lax.broadcasted_iota(jnp.int32, (blk, blk), 1)
    ninf = jnp.float32(-jnp.inf)
    zer = jnp.zeros((blk, blk), F32)

    def band_bias(i):
        has_prev = jnp.where(n > 0, 0.0, ninf) if i == 0 else jnp.float32(0.0)
        has_next = jnp.where(n < ns - 1, 0.0, ninf) if i == nq - 1 else jnp.float32(0.0)
        return jnp.concatenate([jnp.where(ki >= qi, has_prev, ninf), zer,
                                jnp.where(ki <= qi, has_next, ninf)], axis=1)

    def chain(i, m, kbd, vbd, bias):
        rows = slice(i * blk, (i + 1) * blk)
        cols = slice(m * LANES, (m + 1) * LANES)
        s = lax.dot_general(q_ref[rows, cols], kbd, (((1,), (1,)), ((), ())), preferred_element_type=F32)
        yield
        ps, ds = [], []
        for e in range(2):
            sk = sink_ref[2 * m + e] * LOG2E
            se = s[:, e * 3 * blk:(e + 1) * 3 * blk] + bias
            mx = jnp.maximum(jnp.max(se, axis=-1, keepdims=True), sk)
            p = jnp.exp2(se - mx)
            ds.append(jnp.sum(p, axis=-1, keepdims=True) + jnp.exp2(sk - mx))
            ps.append(p.astype(BF16))
        o = jnp.dot(jnp.concatenate(ps, axis=1), vbd, preferred_element_type=F32)
        yield
        o_ref[rows, cols] = (o / jnp.where(lo, ds[0], ds[1])).astype(BF16)

    chains = []
    for i in range(nq):
        bias = band_bias(i)
        for g in range(ATTN_KV_HEADS):
            kbd = blockdiag(kh, i, g)
            vbd = blockdiag(vh, i, g)
            for m in range(g * group // 2, (g + 1) * group // 2):
                chains.append(chain(i, m, kbd, vbd, bias))
    _lockstep(chains)


def _win_attn(sink, q, k, v, batch, seq, nq):
    nb = seq // ATTN_BLOCK
    ns = nb // nq
    aw = q.shape[1]
    kvw = k.shape[1]
    cur = lambda b, n: (b * ns + n, 0)
    prev = lambda b, n: (b * nb + jnp.maximum(n * nq - 1, 0), 0)
    nxt = lambda b, n: (b * nb + jnp.minimum(n * nq + nq, nb - 1), 0)
    edge = lambda f: pl.BlockSpec((ATTN_BLOCK, kvw), f)
    own = pl.BlockSpec((nq * ATTN_BLOCK, kvw), cur)
    return pl.pallas_call(
        _win_attn_kernel,
        out_shape=jax.ShapeDtypeStruct(q.shape, BF16),
        grid=(batch, ns),
        in_specs=[pl.BlockSpec(memory_space=pltpu.SMEM),
                  pl.BlockSpec((nq * ATTN_BLOCK, aw), cur),
                  edge(prev), own, edge(nxt), edge(prev), own, edge(nxt)],
        out_specs=pl.BlockSpec((nq * ATTN_BLOCK, aw), cur),
        compiler_params=_cparams(2),
        name="win_attn",
    )(sink, q, k, k, k, v, v, v)


def _cumsum_both(x, axis):
    n = x.shape[axis]
    idx = lax.broadcasted_iota(jnp.int32, x.shape, axis)
    pre, suf = x, x
    s = 1
    while s < n:
        pre = pre + jnp.where(idx >= s, pltpu.roll(pre, s, axis), 0.0)
        suf = suf + jnp.where(idx < n - s, pltpu.roll(suf, n - s, axis), 0.0)
        s *= 2
    return pre, suf


def _unit_tri_inverse(nmat, rc_xor, lower):
    c = nmat.shape[0]
    n0 = jnp.where(rc_xor < TRI_BASE, nmat, 0.0)
    nb16 = n0.astype(BF16)
    p = yield (nb16, nb16)
    y = n0
    n_sq = (TRI_BASE - 1).bit_length() - 1
    for m in range(n_sq):
        pb = p.astype(BF16)
        if m < n_sq - 1:
            r2 = jnp.dot(pb, jnp.concatenate([y.astype(BF16), pb], axis=1), preferred_element_type=F32)
            yield
            y = y + p + r2[:, :c]
            p = r2[:, c:]
        else:
            py = yield (pb, y.astype(BF16))
            y = y + p + py
    neg_n = -nmat
    blk = TRI_BASE
    while blk < c:
        first = blk if lower else 0
        held = [slice(s, s + blk) for s in range(first, c, 2 * blk)]
        kept = [slice(s, s + blk) for s in range(blk - first, c, 2 * blk)]
        take = lambda a: jnp.concatenate([a[s] for s in held], axis=0)
        rx = take(rc_xor)
        l_rows = jnp.where((rx >= blk) & (rx < 2 * blk), take(neg_n), 0.0)
        y_rows = take(y)
        ly = yield (l_rows.astype(BF16), y.astype(BF16))
        u_rows = l_rows + ly
        ub = u_rows.astype(BF16)
        zero = jnp.zeros((blk, c), BF16)
        pieces = [ub[i * blk:(i + 1) * blk] for i in range(len(held))]
        order = [zero, None] if lower else [None, zero]
        u_full = jnp.concatenate([(pieces[i] if o is None else o) for i in range(len(held)) for o in order], axis=0)
        yu = yield (y_rows.astype(BF16), u_full)
        new_rows = y_rows - u_rows - yu
        parts = {}
        for i, s in enumerate(held):
            parts[s.start] = new_rows[i * blk:(i + 1) * blk]
        for s in kept:
            parts[s.start] = y[s]
        y = jnp.concatenate([parts[k] for k in sorted(parts)], axis=0)
        blk *= 2
    return y


def _deltanet_kernel(dq_ref, dk_ref, dv_ref, gate_ref, abr_ref, cw_ref, alog_ref, dtb_ref, gout_ref,
                     o_ref, shifts, qn, kn, vn, kt, oacc, st, pu, pw, pm2, pkte, pgl, gcum, gbeta, gcols, gsum):
    seq = dq_ref.shape[0]
    nc = seq // DN_CHUNK
    c = DN_CHUNK
    hd = DN_HEAD_DIM
    width = DN_HEADS * hd

    win = shifts.shape[2]
    pad = DN_CONV // 2
    lead = (win - c) // 2
    side_taps = [j for j in range(DN_CONV) if j != pad]

    @pl.when(pl.program_id(0) == 0)
    def _():
        tok = lax.broadcasted_iota(jnp.int32, (c, win), 0)
        src = lax.broadcasted_iota(jnp.int32, (c, win), 1)
        for v in range(shifts.shape[0]):
            shifts[v] = jnp.concatenate(
                [jnp.where(src == tok + (j - pad + v * lead), 1.0, 0.0).astype(BF16) for j in side_taps], axis=0)

    def conv_chunk(i):
        r = pl.multiple_of(i * c, c)
        start = pl.multiple_of(jnp.clip(r - lead, 0, seq - win), lead)
        sel = shifts[(r - start) // lead]

        def one(src_ref, w_off, dst, l2, scale):
            taps = jnp.dot(sel, src_ref[pl.ds(start, win), :], preferred_element_type=F32)
            yield
            for h in range(DN_HEADS):
                hs = slice(h * hd, (h + 1) * hd)
                wrow = lambda j: cw_ref[j:j + 1, w_off + h * hd:w_off + (h + 1) * hd]
                y = wrow(pad) * src_ref[pl.ds(r, c), hs].astype(F32)
                for n, j in enumerate(side_taps):
                    y = y + wrow(j) * taps[n * c:(n + 1) * c, hs]
                y = _silu(y)
                if l2:
                    y = y * lax.rsqrt(jnp.sum(y * y, axis=-1, keepdims=True) + EPS)
                    if scale != 1.0:
                        y = y * scale
                dst[i, :, hs] = y.astype(BF16)
                if dst is kn:
                    kt[i, hs, :] = y.T.astype(BF16)

        return [one(dq_ref, 0, qn, True, hd ** -0.5), one(dk_ref, width, kn, True, 1.0),
                one(dv_ref, 2 * width, vn, False, 1.0)]

    def conv_pair(i, carry):
        _lockstep(conv_chunk(2 * i) + conv_chunk(2 * i + 1))
        return carry

    lax.fori_loop(0, nc // 2, conv_pair, 0)

    st[...] = jnp.zeros(st.shape, F32)
    row = lax.broadcasted_iota(jnp.int32, (c, c), 0)
    col = lax.broadcasted_iota(jnp.int32, (c, c), 1)
    eye = row == col
    masks = ((row >= col, row > col), (row <= col, row < col))
    nh = DN_HEADS

    n_g = N_DIRS * nh
    n_r = abr_ref.shape[1]
    ab = jnp.concatenate([abr_ref[ci] for ci in range(nc)], axis=0)
    g = -jnp.exp(jnp.tile(alog_ref[...], (nc, 1))) * _softplus(ab + jnp.tile(dtb_ref[...], (nc, 1)))
    beta = jax.nn.sigmoid(ab)
    pre, suf = _cumsum_both(g, 1)
    cum = jnp.where(lax.broadcasted_iota(jnp.int32, g.shape, 0) % n_r < nh, pre, suf)
    gtot_all = jnp.broadcast_to(jnp.sum(g, axis=1, keepdims=True), g.shape)
    for ci in range(nc):
        rows = slice(ci * n_r, (ci + 1) * n_r)
        gcum[ci] = cum[rows]
        gbeta[ci] = beta[rows]
        gsum[ci] = gtot_all[rows]
        packed = jnp.concatenate([cum[ci * n_r:ci * n_r + n_g], -beta[ci * n_r + n_g:(ci + 1) * n_r],
                                  jnp.zeros((c - 2 * n_g, c), F32)], axis=0)
        gcols[ci] = packed.T

    def prep(ci, h, d, slot):
        j = d * nh + h
        jb = N_DIRS * nh + j
        gc_c = gcols[ci, :, j:j + 1]
        gc_r = gcum[ci, j:j + 1, :]
        neg_b_c = gcols[ci, :, jb:jb + 1]
        b_r = gbeta[ci, jb:jb + 1, :]
        gtot = gsum[ci, j:j + 1, 0:1]
        incl, strict = masks[d]
        hs = slice(h * hd, (h + 1) * hd)
        ktc = kt[ci, hs, :]
        qk2 = jnp.concatenate([qn[ci, :, hs], kn[ci, :, hs]], axis=0)
        gram = yield (qk2, ktc)
        decay = jnp.exp(jnp.where(incl, gc_c - gc_r, -jnp.inf))
        nmat = jnp.where(strict, gram[c:] * neg_b_c * decay, 0.0)
        y = yield from _unit_tri_inverse(nmat, row ^ col, lower=(d == 0))
        tinv = y + jnp.where(eye, 1.0, 0.0)
        eg_r = jnp.exp(gc_r)
        z = jnp.zeros((c, hd), BF16)
        vk = jnp.concatenate([jnp.concatenate([vn[ci, :, hs], z], axis=1),
                              jnp.concatenate([z, kn[ci, :, hs]], axis=1)], axis=0)
        uw = jnp.dot(jnp.concatenate([tinv * b_r, tinv * (b_r * eg_r)], axis=1).astype(BF16), vk,
                     preferred_element_type=F32)
        yield
        pu[slot] = uw[:, :hd]
        pw[slot] = uw[:, hd:].astype(BF16)
        pm2[slot] = jnp.concatenate([jnp.where(eye, eg_r, 0.0), gram[:c] * decay], axis=1).astype(BF16)
        pkte[slot] = (ktc.astype(F32) * jnp.exp(gtot - gc_r)).astype(BF16)
        pgl[slot] = jnp.broadcast_to(jnp.exp(gtot), pgl.shape[1:])

    def scan(ci, h, d, slot):
        j = d * nh + h
        hs = slice(h * hd, (h + 1) * hd)
        wq = jnp.concatenate([pw[slot], qn[ci, :, hs]], axis=0)
        pm = yield (wq, st[j].astype(BF16))
        vb = (pu[slot] - pm[:c]).astype(BF16)
        o = jnp.dot(pm2[slot], jnp.concatenate([pm[c:].astype(BF16), vb], axis=0), preferred_element_type=F32)
        ds = yield (pkte[slot], vb)
        st[j] = st[j] * pgl[slot][0:1, :] + ds
        oacc[ci, :, hs] = oacc[ci, :, hs] + o

    def chunk_of(step_idx, d):
        return step_idx if d == 0 else nc - 1 - step_idx

    n_chain = N_DIRS * nh
    group = pm2.shape[0] // n_chain

    def preps(step0):
        return [prep(chunk_of(step0 + g, d), h, d, g * n_chain + d * nh + h)
                for g in range(group) for d in range(N_DIRS) for h in range(nh)]

    def scans(step0):
        def run(h, d):
            for g in range(group):
                yield from scan(chunk_of(step0 + g, d), h, d, g * n_chain + d * nh + h)
        return [run(h, d) for d in range(N_DIRS) for h in range(nh)]

    oacc[...] = jnp.zeros(oacc.shape, F32)
    _lockstep(preps(0))

    def group_of_steps(i, carry):
        s0 = i * group
        _lockstep(scans(s0))
        _lockstep(preps(s0 + group))
        return carry

    lax.fori_loop(0, nc // group - 1, group_of_steps, 0)
    _lockstep(scans(nc - group))

    def finish(ci, carry):
        r = pl.multiple_of(ci * c, c)
        for h in range(nh):
            hs = slice(h * hd, (h + 1) * hd)
            gate = gate_ref[pl.ds(r, c), hs].astype(F32)
            o_ref[pl.ds(r, c), hs] = (_rms(oacc[ci, :, hs], gout_ref[...]) * _silu(gate)).astype(BF16)
        return carry

    lax.fori_loop(0, nc, finish, 0)


def _deltanet(dqkv, gate, abr, conv_w, alog_c, dtb_c, gout, batch, seq):
    width = DN_HEADS * DN_HEAD_DIM
    nc = seq // DN_CHUNK
    n_ab = abr.shape[1]
    col = lambda j: pl.BlockSpec((seq, width), lambda b, j=j: (b, j))
    return pl.pallas_call(
        _deltanet_kernel,
        out_shape=jax.ShapeDtypeStruct((batch * seq, width), BF16),
        grid=(batch,),
        in_specs=[col(0), col(1), col(2), col(0),
                  pl.BlockSpec((nc, n_ab, DN_CHUNK), lambda b: (b, 0, 0)),
                  _const_spec(conv_w.shape),
                  _const_spec(alog_c.shape), _const_spec(dtb_c.shape),
                  _const_spec(gout.shape)],
        out_specs=col(0),
        scratch_shapes=[
            pltpu.VMEM((3, (DN_CONV - 1) * DN_CHUNK, 2 * DN_CHUNK), BF16),
            pltpu.VMEM((nc, DN_CHUNK, width), BF16),
            pltpu.VMEM((nc, DN_CHUNK, width), BF16),
            pltpu.VMEM((nc, DN_CHUNK, width), BF16),
            pltpu.VMEM((nc, width, DN_CHUNK), BF16),
            pltpu.VMEM((nc, DN_CHUNK, width), F32),
            pltpu.VMEM((N_DIRS * DN_HEADS, DN_HEAD_DIM, DN_HEAD_DIM), F32),
            pltpu.VMEM((DN_GROUP * N_DIRS * DN_HEADS, DN_CHUNK, DN_HEAD_DIM), F32),
            pltpu.VMEM((DN_GROUP * N_DIRS * DN_HEADS, DN_CHUNK, DN_HEAD_DIM), BF16),
            pltpu.VMEM((DN_GROUP * N_DIRS * DN_HEADS, DN_CHUNK, 2 * DN_CHUNK), BF16),
            pltpu.VMEM((DN_GROUP * N_DIRS * DN_HEADS, DN_HEAD_DIM, DN_CHUNK), BF16),
            pltpu.VMEM((DN_GROUP * N_DIRS * DN_HEADS, 8, LANES), F32),
            pltpu.VMEM((nc, n_ab, DN_CHUNK), F32),
            pltpu.VMEM((nc, n_ab, DN_CHUNK), F32),
            pltpu.VMEM((nc, DN_CHUNK, LANES), F32),
            pltpu.VMEM((nc, n_ab, DN_CHUNK), F32),
        ],
        compiler_params=_cparams(1),
        name="deltanet",
    )(dqkv, dqkv, dqkv, gate, abr, conv_w, alog_c, dtb_c, gout)


def _mem_kv_kernel(mem_ref, gain_ref, w32_ref, k_ref, v_ref, w_ref):
    d = k_ref.shape[1]
    _cast_once(w32_ref, w_ref)
    mn = _rms(mem_ref[...], gain_ref[...]).astype(BF16)
    k_ref[...] = jnp.dot(mn, w_ref[:, :d], preferred_element_type=F32).astype(BF16)
    v_ref[...] = jnp.dot(mn, w_ref[:, d:], preferred_element_type=F32).astype(BF16)


def _mem_kv(mem2, gain, w_ckv, mem_len):
    rows, d = mem2.shape
    spec = pl.BlockSpec((mem_len, d), lambda b: (b, 0))
    return pl.pallas_call(
        _mem_kv_kernel,
        out_shape=(jax.ShapeDtypeStruct((rows, d), BF16),) * 2,
        grid=(rows // mem_len,),
        in_specs=[spec, _const_spec((1, d)), _const_spec(w_ckv.shape)],
        out_specs=(spec, spec),
        scratch_shapes=[pltpu.VMEM(w_ckv.shape, BF16)],
        compiler_params=_cparams(1),
        name="mem_kv",
    )(mem2, gain, w_ckv)


def _out_cross_kernel(x_ref, ao_ref, do_ref, wout32_ref, gpost_ref, gpre_ref, wq32_ref, km_ref, vm_ref,
                      wo32_ref, gcpost_ref, y_ref, wout_ref, wq_ref, wo_ref):
    _cast_once(wout32_ref, wout_ref)
    _cast_once(wq32_ref, wq_ref)
    _cast_once(wo32_ref, wo_ref)
    aw = ao_ref.shape[1]
    d = x_ref.shape[1]
    chd = d // CROSS_HEADS
    nt = (((1,), (1,)), ((), ()))

    def rows_chain(r0):
        rs = slice(r0, r0 + ROW_SUB)
        mix = (jnp.dot(ao_ref[rs, :], wout_ref[:aw, :], preferred_element_type=F32)
               + jnp.dot(do_ref[rs, :], wout_ref[aw:, :], preferred_element_type=F32))
        yield
        x1 = x_ref[rs, :] + _rms(mix, gpost_ref[...])
        q = jnp.dot(_rms(x1, gpre_ref[...]).astype(BF16), wq_ref[...], preferred_element_type=F32)
        yield
        q = (q * (chd ** -0.5)).astype(BF16)
        hcols = [slice(h * chd, (h + 1) * chd) for h in range(CROSS_HEADS)]
        scores = [lax.dot_general(q[:, hs], km_ref[:, hs], nt, preferred_element_type=F32) for hs in hcols]
        yield
        probs = [jnp.exp(s - jnp.max(s, axis=-1, keepdims=True)) for s in scores]
        outs = [jnp.dot(p.astype(BF16), vm_ref[:, hs], preferred_element_type=F32) for p, hs in zip(probs, hcols)]
        yield
        heads = [(o / jnp.sum(p, axis=-1, keepdims=True)).astype(BF16) for o, p in zip(outs, probs)]
        cproj = jnp.dot(jnp.concatenate(heads, axis=1), wo_ref[...], preferred_element_type=F32)
        yield
        y_ref[rs, :] = x1 + _rms(cproj, gcpost_ref[...])

    _lockstep([rows_chain(r0) for r0 in range(0, x_ref.shape[0], ROW_SUB)])


def _out_cross(x2, attn_o, dn_o, w_out, g_post, g_pre, w_cq, kmem, vmem, w_co, g_cpost, seq, mem_len, tm):
    t, d = x2.shape
    per_batch = seq // tm
    row = lambda w: pl.BlockSpec((tm, w), lambda i: (i, 0))
    memspec = pl.BlockSpec((mem_len, d), lambda i: (i // per_batch, 0))
    vec = _const_spec((1, d))
    return pl.pallas_call(
        _out_cross_kernel,
        out_shape=jax.ShapeDtypeStruct((t, d), F32),
        grid=(t // tm,),
        in_specs=[row(d), row(attn_o.shape[1]), row(dn_o.shape[1]), _const_spec(w_out.shape), vec, vec,
                  _const_spec(w_cq.shape), memspec, memspec, _const_spec(w_co.shape), vec],
        out_specs=row(d),
        scratch_shapes=[pltpu.VMEM(w_out.shape, BF16), pltpu.VMEM(w_cq.shape, BF16), pltpu.VMEM(w_co.shape, BF16)],
        compiler_params=_cparams(1),
        name="out_cross",
    )(x2, attn_o, dn_o, w_out, g_post, g_pre, w_cq, kmem, vmem, w_co, g_cpost)


def _stream_cast(hbm_ref, dst_ref, stage_ref, sem):
    rows = stage_ref.shape[1]
    n = hbm_ref.shape[0] // rows

    def chunk_copy(i):
        return pltpu.make_async_copy(hbm_ref.at[pl.ds(i * rows, rows), :], stage_ref.at[i % 2], sem.at[i % 2])

    chunk_copy(0).start()
    for i in range(n):
        if i + 1 < n:
            chunk_copy(i + 1).start()
        chunk_copy(i).wait()
        dst_ref[i * rows:(i + 1) * rows, :] = stage_ref[i % 2].astype(BF16)


def _ffn_kernel(x_ref, gpre_ref, wgu_hbm, wdown_hbm, gpost_ref, y_ref,
                wgu_ref, wdown_ref, gu_stage, down_stage, sems):
    dff = wdown_ref.shape[0]

    @pl.when(pl.program_id(0) == 0)
    def _():
        _stream_cast(wgu_hbm, wgu_ref, gu_stage, sems.at[0])
        _stream_cast(wdown_hbm, wdown_ref, down_stage, sems.at[1])

    def rows_chain(r0):
        rs = slice(r0, r0 + ROW_SUB)
        hf = _rms(x_ref[rs, :], gpre_ref[...]).astype(BF16)
        gu = jnp.dot(hf, wgu_ref[...], preferred_element_type=F32)
        yield
        act = (_silu(gu[:, :dff]) * gu[:, dff:]).astype(BF16)
        f = jnp.dot(act, wdown_ref[...], preferred_element_type=F32)
        yield
        y_ref[rs, :] = x_ref[rs, :] + _rms(f, gpost_ref[...])

    _lockstep([rows_chain(r0) for r0 in range(0, x_ref.shape[0], ROW_SUB)])


def _ffn(x2, g_pre, w_gu, w_down, g_post, tm):
    t, d = x2.shape
    row = pl.BlockSpec((tm, d), lambda i: (i, 0))
    vec = _const_spec((1, d))
    return pl.pallas_call(
        _ffn_kernel,
        out_shape=jax.ShapeDtypeStruct((t, d), F32),
        grid=(t // tm,),
        in_specs=[row, vec, pl.BlockSpec(memory_space=pl.ANY), pl.BlockSpec(memory_space=pl.ANY), vec],
        out_specs=row,
        scratch_shapes=[pltpu.VMEM(w_gu.shape, BF16), pltpu.VMEM(w_down.shape, BF16),
                        pltpu.VMEM((2, WEIGHT_STAGE_ROWS, w_gu.shape[1]), F32),
                        pltpu.VMEM((2, WEIGHT_STAGE_ROWS, w_down.shape[1]), F32),
                        pltpu.SemaphoreType.DMA((2, 2))],
        compiler_params=_cparams(1),
        name="ffn",
    )(x2, g_pre, w_gu, w_down, g_post)


def _layer(x, mem, positions, g_mix_pre, w_in, conv_w, a_log, dt_bias, g_dn_out, attn_sink, w_out,
           g_mix_post, g_cross_pre, g_mem, w_cq, w_ckv, w_co, g_cross_post, g_ffn_pre, w_gate_up,
           w_down, g_ffn_post):
    batch, seq, d = x.shape
    mem_len = mem.shape[1]
    t = batch * seq
    x2 = x.reshape(t, d)
    vec = lambda g: g.reshape(1, -1).astype(F32)

    half = ATTN_HEAD_DIM // 2
    inv_freq = ROPE_THETA ** (-jnp.arange(half, dtype=F32) / half)
    invf_row = jnp.tile(inv_freq, LANES // half).reshape(1, LANES)

    n_ab = 2 * N_DIRS * DN_HEADS
    w_t = w_in.astype(F32).T
    q, k, v, dqkv, dgate, abr = _in_proj(x2, positions.reshape(t // LANES, LANES), invf_row, vec(g_mix_pre),
                                              w_t, n_ab, tm=min(1024, t))

    attn_o = _win_attn(attn_sink.astype(F32), q, k, v, batch, seq, nq=min(4, seq // ATTN_BLOCK))

    pad_c = lambda p: jnp.pad(p.reshape(-1, 1).astype(F32), ((0, n_ab - N_DIRS * DN_HEADS), (0, 0)))
    dn_o = _deltanet(dqkv, dgate, abr, conv_w.astype(F32), pad_c(a_log), pad_c(dt_bias), vec(g_dn_out), batch, seq)

    kmem, vmem = _mem_kv(mem.reshape(batch * mem_len, d), vec(g_mem), w_ckv.astype(F32), mem_len)
    x3 = _out_cross(x2, attn_o, dn_o, w_out.astype(F32), vec(g_mix_post), vec(g_cross_pre),
                    w_cq.astype(F32), kmem, vmem, w_co.astype(F32), vec(g_cross_post), seq, mem_len, tm=min(1024, seq))
    y = _ffn(x3, vec(g_ffn_pre), w_gate_up.astype(F32), w_down.astype(F32), vec(g_ffn_post), tm=512)
    return y.reshape(batch, seq, d)


def kernel(x, mem, positions, g_mix_pre, w_in, conv_w, a_log, dt_bias, g_dn_out, attn_sink, w_out,
           g_mix_post, g_cross_pre, g_mem, w_cq, w_ckv, w_co, g_cross_post, g_ffn_pre, w_gate_up,
           w_down, g_ffn_post):
    depth = w_in.shape[0]
    for l in range(depth):
        x = _layer(x, mem, positions, g_mix_pre[l], w_in[l], conv_w[l], a_log[l], dt_bias[l], g_dn_out[l],
                   attn_sink[l], w_out[l], g_mix_post[l], g_cross_pre[l], g_mem[l], w_cq[l], w_ckv[l],
                   w_co[l], g_cross_post[l], g_ffn_pre[l], w_gate_up[l], w_down[l], g_ffn_post[l])
    return x
```

```python
import functools

import jax
import jax.numpy as jnp
from jax import lax
from jax.experimental import pallas as pl
from jax.experimental.pallas import tpu as pltpu

F32 = jnp.float32
BF16 = jnp.bfloat16

EPS = 1e-6
LOG2E = 1.4426950408889634
ROPE_THETA = 10000.0
ATTN_HEADS = 8
ATTN_KV_HEADS = 2
ATTN_HEAD_DIM = 64
ATTN_BLOCK = 128
DN_HEADS = 4
DN_HEAD_DIM = 128
DN_CONV = 5
N_DIRS = 2
CROSS_HEADS = 4

LANES = 128
DN_CHUNK = 128
ROW_SUB = 256
DN_GROUP = 2
TRI_BASE = 16
VMEM_LIMIT = 56 * 1024 * 1024


def _cparams(n_grid_dims):
    return pltpu.CompilerParams(
        dimension_semantics=("arbitrary",) * n_grid_dims,
        vmem_limit_bytes=VMEM_LIMIT,
    )


def _rms(x, gain):
    return x * lax.rsqrt(jnp.mean(x * x, axis=-1, keepdims=True) + EPS) * gain


def _silu(x):
    return x * jax.nn.sigmoid(x)


def _softplus(x):
    return jnp.maximum(x, 0.0) + jnp.log(1.0 + jnp.exp(-jnp.abs(x)))


def _cast_once(src_ref, dst_ref, rows=256):
    @pl.when(pl.program_id(0) == 0)
    def _():
        for r in range(0, src_ref.shape[0], rows):
            dst_ref[r:r + rows, :] = src_ref[r:r + rows, :].astype(BF16)


def _const_spec(shape):
    nd = len(shape)
    return pl.BlockSpec(shape, lambda *_: (0,) * nd, pipeline_mode=pl.Buffered(1))


def _in_proj_kernel(x_ref, pos_ref, invf_ref, gain_ref, wt_ref,
                    q_ref, k_ref, v_ref, dqkv_ref, gate_ref, abr_ref, w_ref):
    half = ATTN_HEAD_DIM // 2
    lane = lax.broadcasted_iota(jnp.int32, (1, LANES), 1)
    first_half = (lane % ATTN_HEAD_DIM) < half
    aw = ATTN_HEADS * ATTN_HEAD_DIM
    kvw = ATTN_KV_HEADS * ATTN_HEAD_DIM
    off = aw + 2 * kvw
    dn = dqkv_ref.shape[1]
    per_sub = ROW_SUB // DN_CHUNK
    n_main = off + dn + gate_ref.shape[1]

    @pl.when(pl.program_id(0) == 0)
    def _():
        for n in range(n_main // LANES):
            cols = slice(n * LANES, (n + 1) * LANES)
            w_ref[:, cols] = wt_ref[cols, :].T.astype(BF16)

    w_abt = wt_ref[n_main:, :].astype(BF16)

    def rows_chain(r0):
        rs = slice(r0, r0 + ROW_SUB)
        h = _rms(x_ref[rs, :], gain_ref[...]).astype(BF16)
        proj = lambda lo, hi: jnp.dot(h, w_ref[:, lo:hi], preferred_element_type=F32)
        q = proj(0, aw)
        kv = proj(aw, off)
        dqkv = proj(off, off + dn)
        gate = proj(off + dn, off + dn + gate_ref.shape[1])
        abr = lax.dot_general(w_abt, h, (((1,), (1,)), ((), ())), preferred_element_type=F32)
        yield
        pos = jnp.concatenate(
            [jnp.broadcast_to(pos_ref[g:g + 1, :].astype(F32), (LANES, LANES)).T
             for g in range(r0 // LANES, (r0 + ROW_SUB) // LANES)], axis=0)
        ang = pos * invf_ref[...]
        cos = jnp.cos(ang)
        sin = jnp.sin(ang)

        def rope(t):
            rot = jnp.where(first_half, -pltpu.roll(t, LANES - half, 1), pltpu.roll(t, half, 1))
            return t * cos + rot * sin

        for m in range(aw // LANES):
            cols = slice(m * LANES, (m + 1) * LANES)
            q_ref[rs, cols] = (rope(q[:, cols]) * (ATTN_HEAD_DIM ** -0.5 * LOG2E)).astype(BF16)
        k_ref[rs, :] = rope(kv[:, :kvw]).astype(BF16)
        v_ref[rs, :] = kv[:, kvw:].astype(BF16)
        dqkv_ref[rs, :] = dqkv.astype(BF16)
        gate_ref[rs, :] = gate.astype(BF16)
        for j in range(per_sub):
            abr_ref[r0 // DN_CHUNK + j] = abr[:, j * DN_CHUNK:(j + 1) * DN_CHUNK]

    _lockstep([rows_chain(r0) for r0 in range(0, x_ref.shape[0], ROW_SUB)])


def _in_proj(x2, pos_rows, invf_row, gain, w_t, n_ab, tm):
    t, d = x2.shape
    aw = ATTN_HEADS * ATTN_HEAD_DIM
    kvw = ATTN_KV_HEADS * ATTN_HEAD_DIM
    dnw = DN_HEADS * DN_HEAD_DIM
    row = lambda w: pl.BlockSpec((tm, w), lambda i: (i, 0))
    out_shape = (
        jax.ShapeDtypeStruct((t, aw), BF16),
        jax.ShapeDtypeStruct((t, kvw), BF16),
        jax.ShapeDtypeStruct((t, kvw), BF16),
        jax.ShapeDtypeStruct((t, 3 * dnw), BF16),
        jax.ShapeDtypeStruct((t, dnw), BF16),
        jax.ShapeDtypeStruct((t // DN_CHUNK, n_ab, DN_CHUNK), F32),
    )
    out_specs = (
        row(aw), row(kvw), row(kvw), row(3 * dnw), row(dnw),
        pl.BlockSpec((tm // DN_CHUNK, n_ab, DN_CHUNK), lambda i: (i, 0, 0)),
    )
    return pl.pallas_call(
        _in_proj_kernel,
        out_shape=out_shape,
        grid=(t // tm,),
        in_specs=[row(d), pl.BlockSpec((tm // LANES, LANES), lambda i: (i, 0)), _const_spec((1, LANES)),
                  _const_spec((1, d)), _const_spec(w_t.shape)],
        out_specs=out_specs,
        scratch_shapes=[pltpu.VMEM((d, w_t.shape[0] - n_ab), BF16)],
        compiler_params=_cparams(1),
        name="in_proj",
    )(x2, pos_rows, invf_row, gain, w_t)


def _paired_dots(reqs):
    out = [None] * len(reqs)
    todo = [i for i, r in enumerate(reqs) if r is not None]
    while todo:
        i = todo.pop(0)
        l1, r1 = reqs[i]
        j = next((t for t in todo if reqs[t][0].shape == l1.shape and reqs[t][1].shape == r1.shape), None)
        if j is None or r1.shape[1] != LANES:
            out[i] = jnp.dot(l1, r1, preferred_element_type=F32)
            continue
        todo.remove(j)
        l2, r2 = reqs[j]
        z = jnp.zeros_like(r1)
        rhs = jnp.concatenate([jnp.concatenate([r1, z], axis=1), jnp.concatenate([z, r2], axis=1)], axis=0)
        both = jnp.dot(jnp.concatenate([l1, l2], axis=1), rhs, preferred_element_type=F32)
        out[i] = both[:, :LANES]
        out[j] = both[:, LANES:]
    return out


def _lockstep(gens):
    live = list(gens)
    sends = [None] * len(live)
    while live:
        reqs, still = [], []
        for g, val in zip(live, sends):
            try:
                reqs.append(g.send(val))
                still.append(g)
            except StopIteration:
                pass
        live = still
        sends = _paired_dots(reqs)


def _win_attn_kernel(sink_ref, q_ref, kp_ref, kc_ref, kn_ref, vp_ref, vc_ref, vn_ref, o_ref):
    n = pl.program_id(1)
    ns = pl.num_programs(1)
    blk = ATTN_BLOCK
    hd = ATTN_HEAD_DIM
    nq = q_ref.shape[0] // blk
    group = ATTN_HEADS // ATTN_KV_HEADS
    lane = lax.broadcasted_iota(jnp.int32, (1, LANES), 1)
    lo = lane < hd

    def halves(x):
        swapped = pltpu.roll(x.astype(F32), hd, 1).astype(BF16)
        z = jnp.zeros_like(x)
        return ((jnp.where(lo, x, z), jnp.where(lo, z, swapped)),
                (jnp.where(lo, swapped, z), jnp.where(lo, z, x)))

    kh = [halves(kp_ref[...])] + [halves(kc_ref[i * blk:(i + 1) * blk, :]) for i in range(nq)] + [halves(kn_ref[...])]
    vh = [halves(vp_ref[...])] + [halves(vc_ref[i * blk:(i + 1) * blk, :]) for i in range(nq)] + [halves(vn_ref[...])]

    def blockdiag(parts, i, g):
        return jnp.concatenate([parts[i + w][g][e] for e in range(2) for w in range(3)], axis=0)

    qi = lax.broadcasted_iota(jnp.int32, (blk, blk), 0)
    ki = lax.broadcasted_iota(jnp.int32, (blk, blk), 1)
    ninf = jnp.float32(-jnp.inf)
    zer = jnp.zeros((blk, blk), F32)

    def band_bias(i):
        has_prev = jnp.where(n > 0, 0.0, ninf) if i == 0 else jnp.float32(0.0)
        has_next = jnp.where(n < ns - 1, 0.0, ninf) if i == nq - 1 else jnp.float32(0.0)
        return jnp.concatenate([jnp.where(ki >= qi, has_prev, ninf), zer,
                                jnp.where(ki <= qi, has_next, ninf)], axis=1)

    def chain(i, m, kbd, vbd, bias):
        rows = slice(i * blk, (i + 1) * blk)
        cols = slice(m * LANES, (m + 1) * LANES)
        s = lax.dot_general(q_ref[rows, cols], kbd, (((1,), (1,)), ((), ())), preferred_element_type=F32)
        yield
        ps, ds = [], []
        for e in range(2):
            sk = sink_ref[2 * m + e] * LOG2E
            se = s[:, e * 3 * blk:(e + 1) * 3 * blk] + bias
            mx = jnp.maximum(jnp.max(se, axis=-1, keepdims=True), sk)
            p = jnp.exp2(se - mx)
            ds.append(jnp.sum(p, axis=-1, keepdims=True) + jnp.exp2(sk - mx))
            ps.append(p.astype(BF16))
        o = jnp.dot(jnp.concatenate(ps, axis=1), vbd, preferred_element_type=F32)
        yield
        o_ref[rows, cols] = (o / jnp.where(lo, ds[0], ds[1])).astype(BF16)

    chains = []
    for i in range(nq):
        bias = band_bias(i)
        for g in range(ATTN_KV_HEADS):
            kbd = blockdiag(kh, i, g)
            vbd = blockdiag(vh, i, g)
            for m in range(g * group // 2, (g + 1) * group // 2):
                chains.append(chain(i, m, kbd, vbd, bias))
    _lockstep(chains)


def _win_attn(sink, q, k, v, batch, seq, nq):
    nb = seq // ATTN_BLOCK
    ns = nb // nq
    aw = q.shape[1]
    kvw = k.shape[1]
    cur = lambda b, n: (b * ns + n, 0)
    prev = lambda b, n: (b * nb + jnp.maximum(n * nq - 1, 0), 0)
    nxt = lambda b, n: (b * nb + jnp.minimum(n * nq + nq, nb - 1), 0)
    edge = lambda f: pl.BlockSpec((ATTN_BLOCK, kvw), f)
    own = pl.BlockSpec((nq * ATTN_BLOCK, kvw), cur)
    return pl.pallas_call(
        _win_attn_kernel,
        out_shape=jax.ShapeDtypeStruct(q.shape, BF16),
        grid=(batch, ns),
        in_specs=[pl.BlockSpec(memory_space=pltpu.SMEM),
                  pl.BlockSpec((nq * ATTN_BLOCK, aw), cur),
                  edge(prev), own, edge(nxt), edge(prev), own, edge(nxt)],
        out_specs=pl.BlockSpec((nq * ATTN_BLOCK, aw), cur),
        compiler_params=_cparams(2),
        name="win_attn",
    )(sink, q, k, k, k, v, v, v)


def _cumsum_both(x, axis):
    n = x.shape[axis]
    idx = lax.broadcasted_iota(jnp.int32, x.shape, axis)
    pre, suf = x, x
    s = 1
    while s < n:
        pre = pre + jnp.where(idx >= s, pltpu.roll(pre, s, axis), 0.0)
        suf = suf + jnp.where(idx < n - s, pltpu.roll(suf, n - s, axis), 0.0)
        s *= 2
    return pre, suf


def _unit_tri_inverse(nmat, rc_xor, lower):
    c = nmat.shape[0]
    n0 = jnp.where(rc_xor < TRI_BASE, nmat, 0.0)
    nb16 = n0.astype(BF16)
    p = yield (nb16, nb16)
    y = n0
    n_sq = (TRI_BASE - 1).bit_length() - 1
    for m in range(n_sq):
        pb = p.astype(BF16)
        if m < n_sq - 1:
            r2 = jnp.dot(pb, jnp.concatenate([y.astype(BF16), pb], axis=1), preferred_element_type=F32)
            yield
            y = y + p + r2[:, :c]
            p = r2[:, c:]
        else:
            py = yield (pb, y.astype(BF16))
            y = y + p + py
    neg_n = -nmat
    blk = TRI_BASE
    while blk < c:
        first = blk if lower else 0
        held = [slice(s, s + blk) for s in range(first, c, 2 * blk)]
        kept = [slice(s, s + blk) for s in range(blk - first, c, 2 * blk)]
        take = lambda a: jnp.concatenate([a[s] for s in held], axis=0)
        rx = take(rc_xor)
        l_rows = jnp.where((rx >= blk) & (rx < 2 * blk), take(neg_n), 0.0)
        y_rows = take(y)
        ly = yield (l_rows.astype(BF16), y.astype(BF16))
        u_rows = l_rows + ly
        ub = u_rows.astype(BF16)
        zero = jnp.zeros((blk, c), BF16)
        pieces = [ub[i * blk:(i + 1) * blk] for i in range(len(held))]
        order = [zero, None] if lower else [None, zero]
        u_full = jnp.concatenate([(pieces[i] if o is None else o) for i in range(len(held)) for o in order], axis=0)
        yu = yield (y_rows.astype(BF16), u_full)
        new_rows = y_rows - u_rows - yu
        parts = {}
        for i, s in enumerate(held):
            parts[s.start] = new_rows[i * blk:(i + 1) * blk]
        for s in kept:
            parts[s.start] = y[s]
        y = jnp.concatenate([parts[k] for k in sorted(parts)], axis=0)
        blk *= 2
    return y


def _deltanet_kernel(dq_ref, dk_ref, dv_ref, abr_ref, cw_ref, alog_ref, dtb_ref,
                     o_ref, shifts, qn, kn, vn, kt, oacc, st, pu, pw, pm2, pkte, pgl, gcum, gbeta, gcols, gsum):
    seq = dq_ref.shape[0]
    nc = seq // DN_CHUNK
    c = DN_CHUNK
    hd = DN_HEAD_DIM
    width = DN_HEADS * hd

    win = shifts.shape[2]
    pad = DN_CONV // 2
    lead = (win - c) // 2
    side_taps = [j for j in range(DN_CONV) if j != pad]

    @pl.when(pl.program_id(0) == 0)
    def _():
        tok = lax.broadcasted_iota(jnp.int32, (c, win), 0)
        src = lax.broadcasted_iota(jnp.int32, (c, win), 1)
        for v in range(shifts.shape[0]):
            shifts[v] = jnp.concatenate(
                [jnp.where(src == tok + (j - pad + v * lead), 1.0, 0.0).astype(BF16) for j in side_taps], axis=0)

    def conv_chunk(i):
        r = pl.multiple_of(i * c, c)
        start = pl.multiple_of(jnp.clip(r - lead, 0, seq - win), lead)
        sel = shifts[(r - start) // lead]

        def one(src_ref, w_off, dst, l2, scale):
            taps = jnp.dot(sel, src_ref[pl.ds(start, win), :], preferred_element_type=F32)
            yield
            for h in range(DN_HEADS):
                hs = slice(h * hd, (h + 1) * hd)
                wrow = lambda j: cw_ref[j:j + 1, w_off + h * hd:w_off + (h + 1) * hd]
                y = wrow(pad) * src_ref[pl.ds(r, c), hs].astype(F32)
                for n, j in enumerate(side_taps):
                    y = y + wrow(j) * taps[n * c:(n + 1) * c, hs]
                y = _silu(y)
                if l2:
                    y = y * lax.rsqrt(jnp.sum(y * y, axis=-1, keepdims=True) + EPS)
                    if scale != 1.0:
                        y = y * scale
                dst[i, :, hs] = y.astype(BF16)
                if dst is kn:
                    kt[i, hs, :] = y.T.astype(BF16)

        return [one(dq_ref, 0, qn, True, hd ** -0.5), one(dk_ref, width, kn, True, 1.0),
                one(dv_ref, 2 * width, vn, False, 1.0)]

    def conv_pair(i, carry):
        _lockstep(conv_chunk(2 * i) + conv_chunk(2 * i + 1))
        return carry

    lax.fori_loop(0, nc // 2, conv_pair, 0)

    st[...] = jnp.zeros(st.shape, F32)
    row = lax.broadcasted_iota(jnp.int32, (c, c), 0)
    col = lax.broadcasted_iota(jnp.int32, (c, c), 1)
    eye = row == col
    masks = ((row >= col, row > col), (row <= col, row < col))
    nh = DN_HEADS

    n_g = N_DIRS * nh
    n_r = abr_ref.shape[1]
    ab = jnp.concatenate([abr_ref[ci] for ci in range(nc)], axis=0)
    g = -jnp.exp(jnp.tile(alog_ref[...], (nc, 1))) * _softplus(ab + jnp.tile(dtb_ref[...], (nc, 1)))
    beta = jax.nn.sigmoid(ab)
    pre, suf = _cumsum_both(g, 1)
    cum = jnp.where(lax.broadcasted_iota(jnp.int32, g.shape, 0) % n_r < nh, pre, suf)
    gtot_all = jnp.broadcast_to(jnp.sum(g, axis=1, keepdims=True), g.shape)
    for ci in range(nc):
        rows = slice(ci * n_r, (ci + 1) * n_r)
        gcum[ci] = cum[rows]
        gbeta[ci] = beta[rows]
        gsum[ci] = gtot_all[rows]
        packed = jnp.concatenate([cum[ci * n_r:ci * n_r + n_g], -beta[ci * n_r + n_g:(ci + 1) * n_r],
                                  jnp.zeros((c - 2 * n_g, c), F32)], axis=0)
        gcols[ci] = packed.T

    def prep(ci, h, d, slot):
        j = d * nh + h
        jb = N_DIRS * nh + j
        gc_c = gcols[ci, :, j:j + 1]
        gc_r = gcum[ci, j:j + 1, :]
        neg_b_c = gcols[ci, :, jb:jb + 1]
        b_r = gbeta[ci, jb:jb + 1, :]
        gtot = gsum[ci, j:j + 1, 0:1]
        incl, strict = masks[d]
        hs = slice(h * hd, (h + 1) * hd)
        ktc = kt[ci, hs, :]
        qk2 = jnp.concatenate([qn[ci, :, hs], kn[ci, :, hs]], axis=0)
        gram = yield (qk2, ktc)
        decay = jnp.exp(jnp.where(incl, gc_c - gc_r, -jnp.inf))
        nmat = jnp.where(strict, gram[c:] * neg_b_c * decay, 0.0)
        y = yield from _unit_tri_inverse(nmat, row ^ col, lower=(d == 0))
        tinv = y + jnp.where(eye, 1.0, 0.0)
        eg_r = jnp.exp(gc_r)
        z = jnp.zeros((c, hd), BF16)
        vk = jnp.concatenate([jnp.concatenate([vn[ci, :, hs], z], axis=1),
                              jnp.concatenate([z, kn[ci, :, hs]], axis=1)], axis=0)
        uw = jnp.dot(jnp.concatenate([tinv * b_r, tinv * (b_r * eg_r)], axis=1).astype(BF16), vk,
                     preferred_element_type=F32)
        yield
        pu[slot] = uw[:, :hd]
        pw[slot] = uw[:, hd:].astype(BF16)
        pm2[slot] = jnp.concatenate([jnp.where(eye, eg_r, 0.0), gram[:c] * decay], axis=1).astype(BF16)
        pkte[slot] = (ktc.astype(F32) * jnp.exp(gtot - gc_r)).astype(BF16)
        pgl[slot] = jnp.broadcast_to(jnp.exp(gtot), pgl.shape[1:])

    def scan(ci, h, d, slot):
        j = d * nh + h
        hs = slice(h * hd, (h + 1) * hd)
        wq = jnp.concatenate([pw[slot], qn[ci, :, hs]], axis=0)
        pm = yield (wq, st[j].astype(BF16))
        vb = (pu[slot] - pm[:c]).astype(BF16)
        o = jnp.dot(pm2[slot], jnp.concatenate([pm[c:].astype(BF16), vb], axis=0), preferred_element_type=F32)
        ds = yield (pkte[slot], vb)
        st[j] = st[j] * pgl[slot][0:1, :] + ds
        oacc[ci, :, hs] = oacc[ci, :, hs] + o

    def chunk_of(step_idx, d):
        return step_idx if d == 0 else nc - 1 - step_idx

    n_chain = N_DIRS * nh
    group = pm2.shape[0] // n_chain

    def preps(step0):
        return [prep(chunk_of(step0 + g, d), h, d, g * n_chain + d * nh + h)
                for g in range(group) for d in range(N_DIRS) for h in range(nh)]

    def scans(step0):
        def run(h, d):
            for g in range(group):
                yield from scan(chunk_of(step0 + g, d), h, d, g * n_chain + d * nh + h)
        return [run(h, d) for d in range(N_DIRS) for h in range(nh)]

    oacc[...] = jnp.zeros(oacc.shape, F32)
    _lockstep(preps(0))

    def group_of_steps(i, carry):
        s0 = i * group
        _lockstep(scans(s0))
        _lockstep(preps(s0 + group))
        return carry

    lax.fori_loop(0, nc // group - 1, group_of_steps, 0)
    _lockstep(scans(nc - group))

    def finish(ci, carry):
        o_ref[pl.ds(pl.multiple_of(ci * c, c), c), :] = oacc[ci].astype(BF16)
        return carry

    lax.fori_loop(0, nc, finish, 0)


def _deltanet(dqkv, abr, conv_w, alog_c, dtb_c, batch, seq):
    width = DN_HEADS * DN_HEAD_DIM
    nc = seq // DN_CHUNK
    n_ab = abr.shape[1]
    col = lambda j: pl.BlockSpec((seq, width), lambda b, j=j: (b, j))
    return pl.pallas_call(
        _deltanet_kernel,
        out_shape=jax.ShapeDtypeStruct((batch * seq, width), BF16),
        grid=(batch,),
        in_specs=[col(0), col(1), col(2),
                  pl.BlockSpec((nc, n_ab, DN_CHUNK), lambda b: (b, 0, 0)),
                  _const_spec(conv_w.shape),
                  _const_spec(alog_c.shape), _const_spec(dtb_c.shape)],
        out_specs=col(0),
        scratch_shapes=[
            pltpu.VMEM((3, (DN_CONV - 1) * DN_CHUNK, 2 * DN_CHUNK), BF16),
            pltpu.VMEM((nc, DN_CHUNK, width), BF16),
            pltpu.VMEM((nc, DN_CHUNK, width), BF16),
            pltpu.VMEM((nc, DN_CHUNK, width), BF16),
            pltpu.VMEM((nc, width, DN_CHUNK), BF16),
            pltpu.VMEM((nc, DN_CHUNK, width), F32),
            pltpu.VMEM((N_DIRS * DN_HEADS, DN_HEAD_DIM, DN_HEAD_DIM), F32),
            pltpu.VMEM((DN_GROUP * N_DIRS * DN_HEADS, DN_CHUNK, DN_HEAD_DIM), F32),
            pltpu.VMEM((DN_GROUP * N_DIRS * DN_HEADS, DN_CHUNK, DN_HEAD_DIM), BF16),
            pltpu.VMEM((DN_GROUP * N_DIRS * DN_HEADS, DN_CHUNK, 2 * DN_CHUNK), BF16),
            pltpu.VMEM((DN_GROUP * N_DIRS * DN_HEADS, DN_HEAD_DIM, DN_CHUNK), BF16),
            pltpu.VMEM((DN_GROUP * N_DIRS * DN_HEADS, 8, LANES), F32),
            pltpu.VMEM((nc, n_ab, DN_CHUNK), F32),
            pltpu.VMEM((nc, n_ab, DN_CHUNK), F32),
            pltpu.VMEM((nc, DN_CHUNK, LANES), F32),
            pltpu.VMEM((nc, n_ab, DN_CHUNK), F32),
        ],
        compiler_params=_cparams(1),
        name="deltanet",
    )(dqkv, dqkv, dqkv, abr, conv_w, alog_c, dtb_c)


def _mem_kv_kernel(mem_ref, gain_ref, w32_ref, k_ref, v_ref, w_ref):
    d = k_ref.shape[1]
    _cast_once(w32_ref, w_ref)
    mn = _rms(mem_ref[...], gain_ref[...]).astype(BF16)
    k_ref[...] = jnp.dot(mn, w_ref[:, :d], preferred_element_type=F32).astype(BF16)
    v_ref[...] = jnp.dot(mn, w_ref[:, d:], preferred_element_type=F32).astype(BF16)


def _mem_kv(mem2, gain, w_ckv, mem_len):
    rows, d = mem2.shape
    spec = pl.BlockSpec((mem_len, d), lambda b: (b, 0))
    return pl.pallas_call(
        _mem_kv_kernel,
        out_shape=(jax.ShapeDtypeStruct((rows, d), BF16),) * 2,
        grid=(rows // mem_len,),
        in_specs=[spec, _const_spec((1, d)), _const_spec(w_ckv.shape)],
        out_specs=(spec, spec),
        scratch_shapes=[pltpu.VMEM(w_ckv.shape, BF16)],
        compiler_params=_cparams(1),
        name="mem_kv",
    )(mem2, gain, w_ckv)


def _out_cross_kernel(x_ref, ao_ref, do_ref, dgate_ref, gdn_ref, wout32_ref, gpost_ref, gpre_ref, wq32_ref, km_ref, vm_ref,
                      wo32_ref, gcpost_ref, y_ref, wout_ref, wq_ref, wo_ref):
    _cast_once(wout32_ref, wout_ref)
    _cast_once(wq32_ref, wq_ref)
    _cast_once(wo32_ref, wo_ref)
    aw = ao_ref.shape[1]
    d = x_ref.shape[1]
    chd = d // CROSS_HEADS
    nt = (((1,), (1,)), ((), ()))

    def rows_chain(r0):
        rs = slice(r0, r0 + ROW_SUB)
        dn = []
        for h in range(DN_HEADS):
            hs = slice(h * DN_HEAD_DIM, (h + 1) * DN_HEAD_DIM)
            gated = _rms(do_ref[rs, hs].astype(F32), gdn_ref[...]) * _silu(dgate_ref[rs, hs].astype(F32))
            dn.append(gated.astype(BF16))
        mix = (jnp.dot(ao_ref[rs, :], wout_ref[:aw, :], preferred_element_type=F32)
               + jnp.dot(jnp.concatenate(dn, axis=1), wout_ref[aw:, :], preferred_element_type=F32))
        yield
        x1 = x_ref[rs, :] + _rms(mix, gpost_ref[...])
        q = jnp.dot(_rms(x1, gpre_ref[...]).astype(BF16), wq_ref[...], preferred_element_type=F32)
        yield
        q = (q * (chd ** -0.5)).astype(BF16)
        hcols = [slice(h * chd, (h + 1) * chd) for h in range(CROSS_HEADS)]
        scores = [lax.dot_general(q[:, hs], km_ref[:, hs], nt, preferred_element_type=F32) for hs in hcols]
        yield
        probs = [jnp.exp(s - jnp.max(s, axis=-1, keepdims=True)) for s in scores]
        outs = [jnp.dot(p.astype(BF16), vm_ref[:, hs], preferred_element_type=F32) for p, hs in zip(probs, hcols)]
        yield
        heads = [(o / jnp.sum(p, axis=-1, keepdims=True)).astype(BF16) for o, p in zip(outs, probs)]
        cproj = jnp.dot(jnp.concatenate(heads, axis=1), wo_ref[...], preferred_element_type=F32)
        yield
        y_ref[rs, :] = x1 + _rms(cproj, gcpost_ref[...])

    _lockstep([rows_chain(r0) for r0 in range(0, x_ref.shape[0], ROW_SUB)])


def _out_cross(x2, attn_o, dn_raw, dgate, g_dn, w_out, g_post, g_pre, w_cq, kmem, vmem, w_co, g_cpost, seq, mem_len, tm):
    t, d = x2.shape
    per_batch = seq // tm
    row = lambda w: pl.BlockSpec((tm, w), lambda i: (i, 0))
    memspec = pl.BlockSpec((mem_len, d), lambda i: (i // per_batch, 0))
    vec = _const_spec((1, d))
    return pl.pallas_call(
        _out_cross_kernel,
        out_shape=jax.ShapeDtypeStruct((t, d), F32),
        grid=(t // tm,),
        in_specs=[row(d), row(attn_o.shape[1]), row(dn_raw.shape[1]), row(dgate.shape[1]), _const_spec(g_dn.shape),
                  _const_spec(w_out.shape), vec, vec,
                  _const_spec(w_cq.shape), memspec, memspec, _const_spec(w_co.shape), vec],
        out_specs=row(d),
        scratch_shapes=[pltpu.VMEM(w_out.shape, BF16), pltpu.VMEM(w_cq.shape, BF16), pltpu.VMEM(w_co.shape, BF16)],
        compiler_params=_cparams(1),
        name="out_cross",
    )(x2, attn_o, dn_raw, dgate, g_dn, w_out, g_post, g_pre, w_cq, kmem, vmem, w_co, g_cpost)


def _ffn_kernel(x_ref, gpre_ref, wgu_ref, wdown_ref, gpost_ref, y_ref):
    dff = wdown_ref.shape[0]

    def rows_chain(r0):
        rs = slice(r0, r0 + ROW_SUB)
        hf = _rms(x_ref[rs, :], gpre_ref[...]).astype(BF16)
        gu = jnp.dot(hf, wgu_ref[...], preferred_element_type=F32)
        yield
        act = (_silu(gu[:, :dff]) * gu[:, dff:]).astype(BF16)
        f = jnp.dot(act, wdown_ref[...], preferred_element_type=F32)
        yield
        y_ref[rs, :] = x_ref[rs, :] + _rms(f, gpost_ref[...])

    _lockstep([rows_chain(r0) for r0 in range(0, x_ref.shape[0], ROW_SUB)])


def _ffn(x2, g_pre, w_gu, w_down, g_post, tm):
    t, d = x2.shape
    row = pl.BlockSpec((tm, d), lambda i: (i, 0))
    vec = _const_spec((1, d))
    return pl.pallas_call(
        _ffn_kernel,
        out_shape=jax.ShapeDtypeStruct((t, d), F32),
        grid=(t // tm,),
        in_specs=[row, vec, _const_spec(w_gu.shape), _const_spec(w_down.shape), vec],
        out_specs=row,
        compiler_params=_cparams(1),
        name="ffn",
    )(x2, g_pre, w_gu, w_down, g_post)


def _layer(x, mem, positions, g_mix_pre, w_in, conv_w, a_log, dt_bias, g_dn_out, attn_sink, w_out,
           g_mix_post, g_cross_pre, g_mem, w_cq, w_ckv, w_co, g_cross_post, g_ffn_pre, w_gate_up,
           w_down, g_ffn_post):
    batch, seq, d = x.shape
    mem_len = mem.shape[1]
    t = batch * seq
    x2 = x.reshape(t, d)
    vec = lambda g: g.reshape(1, -1).astype(F32)

    half = ATTN_HEAD_DIM // 2
    inv_freq = ROPE_THETA ** (-jnp.arange(half, dtype=F32) / half)
    invf_row = jnp.tile(inv_freq, LANES // half).reshape(1, LANES)

    n_ab = 2 * N_DIRS * DN_HEADS
    w_t = w_in.astype(F32).T
    q, k, v, dqkv, dgate, abr = _in_proj(x2, positions.reshape(t // LANES, LANES), invf_row, vec(g_mix_pre),
                                              w_t, n_ab, tm=min(1024, t))

    attn_o = _win_attn(attn_sink.astype(F32), q, k, v, batch, seq, nq=min(4, seq // ATTN_BLOCK))

    pad_c = lambda p: jnp.pad(p.reshape(-1, 1).astype(F32), ((0, n_ab - N_DIRS * DN_HEADS), (0, 0)))
    dn_raw = _deltanet(dqkv, abr, conv_w.astype(F32), pad_c(a_log), pad_c(dt_bias), batch, seq)

    kmem, vmem = _mem_kv(mem.reshape(batch * mem_len, d), vec(g_mem), w_ckv.astype(F32), mem_len)
    x3 = _out_cross(x2, attn_o, dn_raw, dgate, vec(g_dn_out), w_out.astype(F32), vec(g_mix_post), vec(g_cross_pre),
                    w_cq.astype(F32), kmem, vmem, w_co.astype(F32), vec(g_cross_post), seq, mem_len, tm=min(1024, seq))
    y = _ffn(x3, vec(g_ffn_pre), w_gate_up.astype(BF16), w_down.astype(BF16), vec(g_ffn_post), tm=512)
    return y.reshape(batch, seq, d)


def kernel(x, mem, positions, g_mix_pre, w_in, conv_w, a_log, dt_bias, g_dn_out, attn_sink, w_out,
           g_mix_post, g_cross_pre, g_mem, w_cq, w_ckv, w_co, g_cross_post, g_ffn_pre, w_gate_up,
           w_down, g_ffn_post):
    depth = w_in.shape[0]
    for l in range(depth):
        x = _layer(x, mem, positions, g_mix_pre[l], w_in[l], conv_w[l], a_log[l], dt_bias[l], g_dn_out[l],
                   attn_sink[l], w_out[l], g_mix_post[l], g_cross_pre[l], g_mem[l], w_cq[l], w_ckv[l],
                   w_co[l], g_cross_post[l], g_ffn_pre[l], w_gate_up[l], w_down[l], g_ffn_post[l])
    return x
```

```python
import jax
import jax.numpy as jnp
from jax import lax
from jax.experimental import pallas as pl
from jax.experimental.pallas import tpu as pltpu

F32 = jnp.float32
BF16 = jnp.bfloat16

EPS = 1e-6
LOG2E = 1.4426950408889634
ROPE_THETA = 10000.0
ATTN_HEADS = 8
ATTN_KV_HEADS = 2
ATTN_HEAD_DIM = 64
ATTN_BLOCK = 128
DN_HEADS = 4
DN_HEAD_DIM = 128
DN_CONV = 5
N_DIRS = 2
CROSS_HEADS = 4

LANES = 128
SUBLANES = 8
VMEM_LIMIT = 56 * 2 ** 20

DN_CHUNK = LANES
CONV_CHUNKS = 4
DN_GROUP = 2
TRI_BASE = 16
ROW_SUB = 256
CAST_ROWS = 256
IN_PROJ_ROWS = 1024
OUT_CROSS_ROWS = 1024
FFN_ROWS = 512
ATTN_Q_BLOCKS = 4


def _cparams(n_grid_dims):
    return pltpu.CompilerParams(
        dimension_semantics=("arbitrary",) * n_grid_dims,
        vmem_limit_bytes=VMEM_LIMIT,
    )


def _rms(x, gain):
    return x * lax.rsqrt(jnp.mean(x * x, axis=-1, keepdims=True) + EPS) * gain


def _silu(x):
    return x * jax.nn.sigmoid(x)


def _softplus(x):
    return jnp.maximum(x, 0.0) + jnp.log(1.0 + jnp.exp(-jnp.abs(x)))


def _cast_once(src_ref, dst_ref):
    @pl.when(pl.program_id(0) == 0)
    def _():
        for r in range(0, src_ref.shape[0], CAST_ROWS):
            dst_ref[r:r + CAST_ROWS, :] = src_ref[r:r + CAST_ROWS, :].astype(BF16)


def _const_spec(shape):
    nd = len(shape)
    return pl.BlockSpec(shape, lambda *_: (0,) * nd, pipeline_mode=pl.Buffered(1))


def _in_proj_kernel(x_ref, pos_ref, invf_ref, gain_ref, wt_ref,
                    q_ref, k_ref, v_ref, dqkv_ref, gate_ref, abr_ref, w_ref):
    half = ATTN_HEAD_DIM // 2
    lane = lax.broadcasted_iota(jnp.int32, (1, LANES), 1)
    first_half = (lane % ATTN_HEAD_DIM) < half
    aw = ATTN_HEADS * ATTN_HEAD_DIM
    kvw = ATTN_KV_HEADS * ATTN_HEAD_DIM
    off = aw + 2 * kvw
    dn = dqkv_ref.shape[1]
    per_sub = ROW_SUB // DN_CHUNK
    n_main = off + dn + gate_ref.shape[1]

    @pl.when(pl.program_id(0) == 0)
    def _():
        for n in range(n_main // LANES):
            cols = slice(n * LANES, (n + 1) * LANES)
            w_ref[:, cols] = wt_ref[cols, :].T.astype(BF16)

    w_abt = wt_ref[n_main:, :].astype(BF16)

    def rows_chain(r0):
        rs = slice(r0, r0 + ROW_SUB)
        h = _rms(x_ref[rs, :], gain_ref[...]).astype(BF16)
        proj = lambda lo, hi: jnp.dot(h, w_ref[:, lo:hi], preferred_element_type=F32)
        q = proj(0, aw)
        kv = proj(aw, off)
        dqkv = proj(off, off + dn)
        gate = proj(off + dn, off + dn + gate_ref.shape[1])
        abr = lax.dot_general(w_abt, h, (((1,), (1,)), ((), ())), preferred_element_type=F32)
        yield
        pos = jnp.concatenate(
            [jnp.broadcast_to(pos_ref[g:g + 1, :].astype(F32), (LANES, LANES)).T
             for g in range(r0 // LANES, (r0 + ROW_SUB) // LANES)], axis=0)
        ang = pos * invf_ref[...]
        cos = jnp.cos(ang)
        sin = jnp.sin(ang)

        def rope(t):
            rot = jnp.where(first_half, -pltpu.roll(t, LANES - half, 1), pltpu.roll(t, half, 1))
            return t * cos + rot * sin

        for m in range(aw // LANES):
            cols = slice(m * LANES, (m + 1) * LANES)
            q_ref[rs, cols] = (rope(q[:, cols]) * (ATTN_HEAD_DIM ** -0.5 * LOG2E)).astype(BF16)
        k_ref[rs, :] = rope(kv[:, :kvw]).astype(BF16)
        v_ref[rs, :] = kv[:, kvw:].astype(BF16)
        dqkv_ref[rs, :] = dqkv.astype(BF16)
        gate_ref[rs, :] = gate.astype(BF16)
        for j in range(per_sub):
            abr_ref[r0 // DN_CHUNK + j] = abr[:, j * DN_CHUNK:(j + 1) * DN_CHUNK]

    _lockstep([rows_chain(r0) for r0 in range(0, x_ref.shape[0], ROW_SUB)])


def _in_proj(x2, pos_rows, invf_row, gain, w_t, n_ab, tm):
    t, d = x2.shape
    aw = ATTN_HEADS * ATTN_HEAD_DIM
    kvw = ATTN_KV_HEADS * ATTN_HEAD_DIM
    dnw = DN_HEADS * DN_HEAD_DIM
    row = lambda w: pl.BlockSpec((tm, w), lambda i: (i, 0))
    out_shape = (
        jax.ShapeDtypeStruct((t, aw), BF16),
        jax.ShapeDtypeStruct((t, kvw), BF16),
        jax.ShapeDtypeStruct((t, kvw), BF16),
        jax.ShapeDtypeStruct((t, 3 * dnw), BF16),
        jax.ShapeDtypeStruct((t, dnw), BF16),
        jax.ShapeDtypeStruct((t // DN_CHUNK, n_ab, DN_CHUNK), F32),
    )
    out_specs = (
        row(aw), row(kvw), row(kvw), row(3 * dnw), row(dnw),
        pl.BlockSpec((tm // DN_CHUNK, n_ab, DN_CHUNK), lambda i: (i, 0, 0)),
    )
    return pl.pallas_call(
        _in_proj_kernel,
        out_shape=out_shape,
        grid=(t // tm,),
        in_specs=[row(d), pl.BlockSpec((tm // LANES, LANES), lambda i: (i, 0)), _const_spec((1, LANES)),
                  _const_spec((1, d)), _const_spec(w_t.shape)],
        out_specs=out_specs,
        scratch_shapes=[pltpu.VMEM((d, w_t.shape[0] - n_ab), BF16)],
        compiler_params=_cparams(1),
        name="in_proj",
    )(x2, pos_rows, invf_row, gain, w_t)


def _paired_dots(reqs):
    out = [None] * len(reqs)
    todo = [i for i, r in enumerate(reqs) if r is not None]
    while todo:
        i = todo.pop(0)
        l1, r1 = reqs[i]
        j = next((t for t in todo if reqs[t][0].shape == l1.shape and reqs[t][1].shape == r1.shape), None)
        if j is None or r1.shape[1] != LANES:
            out[i] = jnp.dot(l1, r1, preferred_element_type=F32)
            continue
        todo.remove(j)
        l2, r2 = reqs[j]
        z = jnp.zeros_like(r1)
        rhs = jnp.concatenate([jnp.concatenate([r1, z], axis=1), jnp.concatenate([z, r2], axis=1)], axis=0)
        both = jnp.dot(jnp.concatenate([l1, l2], axis=1), rhs, preferred_element_type=F32)
        out[i] = both[:, :LANES]
        out[j] = both[:, LANES:]
    return out


def _lockstep(gens):
    live = list(gens)
    sends = [None] * len(live)
    while live:
        reqs, still = [], []
        for g, val in zip(live, sends):
            try:
                reqs.append(g.send(val))
                still.append(g)
            except StopIteration:
                pass
        live = still
        sends = _paired_dots(reqs)


def _win_attn_kernel(sink_ref, q_ref, kp_ref, kc_ref, kn_ref, vp_ref, vc_ref, vn_ref, o_ref):
    n = pl.program_id(1)
    ns = pl.num_programs(1)
    blk = ATTN_BLOCK
    hd = ATTN_HEAD_DIM
    nq = q_ref.shape[0] // blk
    group = ATTN_HEADS // ATTN_KV_HEADS
    lane = lax.broadcasted_iota(jnp.int32, (1, LANES), 1)
    lo = lane < hd

    def halves(x):
        swapped = pltpu.roll(x.astype(F32), hd, 1).astype(BF16)
        z = jnp.zeros_like(x)
        return ((jnp.where(lo, x, z), jnp.where(lo, z, swapped)),
                (jnp.where(lo, swapped, z), jnp.where(lo, z, x)))

    kh = [halves(kp_ref[...])] + [halves(kc_ref[i * blk:(i + 1) * blk, :]) for i in range(nq)] + [halves(kn_ref[...])]
    vh = [halves(vp_ref[...])] + [halves(vc_ref[i * blk:(i + 1) * blk, :]) for i in range(nq)] + [halves(vn_ref[...])]

    def blockdiag(parts, i, g):
        return jnp.concatenate([parts[i + w][g][e] for e in range(2) for w in range(3)], axis=0)

    qi = lax.broadcasted_iota(jnp.int32, (blk, blk), 0)
    ki = lax.broadcasted_iota(jnp.int32, (blk, blk), 1)
    ninf = jnp.float32(-jnp.inf)
    zer = jnp.zeros((blk, blk), F32)

    def band_bias(i):
        has_prev = jnp.where(n > 0, 0.0, ninf) if i == 0 else jnp.float32(0.0)
        has_next = jnp.where(n < ns - 1, 0.0, ninf) if i == nq - 1 else jnp.float32(0.0)
        return jnp.concatenate([jnp.where(ki >= qi, has_prev, ninf), zer,
                                jnp.where(ki <= qi, has_next, ninf)], axis=1)

    def chain(i, m, kbd, vbd, bias):
        rows = slice(i * blk, (i + 1) * blk)
        cols = slice(m * LANES, (m + 1) * LANES)
        s = lax.dot_general(q_ref[rows, cols], kbd, (((1,), (1,)), ((), ())), preferred_element_type=F32)
        yield
        ps, ds = [], []
        for e in range(2):
            sk = sink_ref[2 * m + e] * LOG2E
            se = s[:, e * 3 * blk:(e + 1) * 3 * blk] + bias
            mx = jnp.maximum(jnp.max(se, axis=-1, keepdims=True), sk)
            p = jnp.exp2(se - mx)
            ds.append(jnp.sum(p, axis=-1, keepdims=True) + jnp.exp2(sk - mx))
            ps.append(p.astype(BF16))
        o = jnp.dot(jnp.concatenate(ps, axis=1), vbd, preferred_element_type=F32)
        yield
        o_ref[rows, cols] = (o / jnp.where(lo, ds[0], ds[1])).astype(BF16)

    chains = []
    for i in range(nq):
        bias = band_bias(i)
        for g in range(ATTN_KV_HEADS):
            kbd = blockdiag(kh, i, g)
            vbd = blockdiag(vh, i, g)
            for m in range(g * group // 2, (g + 1) * group // 2):
                chains.append(chain(i, m, kbd, vbd, bias))
    _lockstep(chains)


def _win_attn(sink, q, k, v, batch, seq, nq):
    nb = seq // ATTN_BLOCK
    ns = nb // nq
    aw = q.shape[1]
    kvw = k.shape[1]
    cur = lambda b, n: (b * ns + n, 0)
    prev = lambda b, n: (b * nb + jnp.maximum(n * nq - 1, 0), 0)
    nxt = lambda b, n: (b * nb + jnp.minimum(n * nq + nq, nb - 1), 0)
    edge = lambda f: pl.BlockSpec((ATTN_BLOCK, kvw), f)
    own = pl.BlockSpec((nq * ATTN_BLOCK, kvw), cur)
    return pl.pallas_call(
        _win_attn_kernel,
        out_shape=jax.ShapeDtypeStruct(q.shape, BF16),
        grid=(batch, ns),
        in_specs=[pl.BlockSpec(memory_space=pltpu.SMEM),
                  pl.BlockSpec((nq * ATTN_BLOCK, aw), cur),
                  edge(prev), own, edge(nxt), edge(prev), own, edge(nxt)],
        out_specs=pl.BlockSpec((nq * ATTN_BLOCK, aw), cur),
        compiler_params=_cparams(2),
        name="win_attn",
    )(sink, q, k, k, k, v, v, v)


def _cumsum_both(x, axis):
    n = x.shape[axis]
    idx = lax.broadcasted_iota(jnp.int32, x.shape, axis)
    pre, suf = x, x
    s = 1
    while s < n:
        pre = pre + jnp.where(idx >= s, pltpu.roll(pre, s, axis), 0.0)
        suf = suf + jnp.where(idx < n - s, pltpu.roll(suf, n - s, axis), 0.0)
        s *= 2
    return pre, suf


def _unit_tri_inverse(nmat, rc_xor, lower):
    c = nmat.shape[0]
    n0 = jnp.where(rc_xor < TRI_BASE, nmat, 0.0)
    nb16 = n0.astype(BF16)
    p = yield (nb16, nb16)
    y = n0
    n_sq = (TRI_BASE - 1).bit_length() - 1
    for m in range(n_sq):
        pb = p.astype(BF16)
        if m < n_sq - 1:
            r2 = jnp.dot(pb, jnp.concatenate([y.astype(BF16), pb], axis=1), preferred_element_type=F32)
            yield
            y = y + p + r2[:, :c]
            p = r2[:, c:]
        else:
            py = yield (pb, y.astype(BF16))
            y = y + p + py
    neg_n = -nmat
    blk = TRI_BASE
    while blk < c:
        first = blk if lower else 0
        held = [slice(s, s + blk) for s in range(first, c, 2 * blk)]
        kept = [slice(s, s + blk) for s in range(blk - first, c, 2 * blk)]
        take = lambda a: jnp.concatenate([a[s] for s in held], axis=0)
        rx = take(rc_xor)
        l_rows = jnp.where((rx >= blk) & (rx < 2 * blk), take(neg_n), 0.0)
        y_rows = take(y)
        ly = yield (l_rows.astype(BF16), y.astype(BF16))
        u_rows = l_rows + ly
        ub = u_rows.astype(BF16)
        zero = jnp.zeros((blk, c), BF16)
        pieces = [ub[i * blk:(i + 1) * blk] for i in range(len(held))]
        order = [zero, None] if lower else [None, zero]
        u_full = jnp.concatenate([(pieces[i] if o is None else o) for i in range(len(held)) for o in order], axis=0)
        yu = yield (y_rows.astype(BF16), u_full)
        new_rows = y_rows - u_rows - yu
        parts = {}
        for i, s in enumerate(held):
            parts[s.start] = new_rows[i * blk:(i + 1) * blk]
        for s in kept:
            parts[s.start] = y[s]
        y = jnp.concatenate([parts[k] for k in sorted(parts)], axis=0)
        blk *= 2
    return y


def _deltanet_kernel(dq_ref, dk_ref, dv_ref, abr_ref, cw_ref, alog_ref, dtb_ref,
                     o_ref, shifts, qn, kn, vn, kt, oacc, st, pu, pw, pm2, pkte, pgl, gcum, gbeta, gcols, gsum):
    seq = dq_ref.shape[0]
    nc = seq // DN_CHUNK
    c = DN_CHUNK
    hd = DN_HEAD_DIM
    width = DN_HEADS * hd

    win = shifts.shape[2]
    pad = DN_CONV // 2
    lead = (win - c) // 2
    side_taps = [j for j in range(DN_CONV) if j != pad]

    @pl.when(pl.program_id(0) == 0)
    def _():
        tok = lax.broadcasted_iota(jnp.int32, (c, win), 0)
        src = lax.broadcasted_iota(jnp.int32, (c, win), 1)
        for v in range(shifts.shape[0]):
            shifts[v] = jnp.concatenate(
                [jnp.where(src == tok + (j - pad + v * lead), 1.0, 0.0).astype(BF16) for j in side_taps], axis=0)

    def conv_chunk(i):
        r = pl.multiple_of(i * c, c)
        start = pl.multiple_of(jnp.clip(r - lead, 0, seq - win), lead)
        sel = shifts[(r - start) // lead]

        def one(src_ref, w_off, dst, l2, scale):
            taps = jnp.dot(sel, src_ref[pl.ds(start, win), :], preferred_element_type=F32)
            yield
            for h in range(DN_HEADS):
                hs = slice(h * hd, (h + 1) * hd)
                wrow = lambda j: cw_ref[j:j + 1, w_off + h * hd:w_off + (h + 1) * hd]
                y = wrow(pad) * src_ref[pl.ds(r, c), hs].astype(F32)
                for n, j in enumerate(side_taps):
                    y = y + wrow(j) * taps[n * c:(n + 1) * c, hs]
                y = _silu(y)
                if l2:
                    y = y * lax.rsqrt(jnp.sum(y * y, axis=-1, keepdims=True) + EPS)
                    if scale != 1.0:
                        y = y * scale
                dst[i, :, hs] = y.astype(BF16)
                if dst is kn:
                    kt[i, hs, :] = y.T.astype(BF16)

        return [one(dq_ref, 0, qn, True, hd ** -0.5), one(dk_ref, width, kn, True, 1.0),
                one(dv_ref, 2 * width, vn, False, 1.0)]

    def conv_group(i, carry):
        _lockstep([g for n in range(CONV_CHUNKS) for g in conv_chunk(CONV_CHUNKS * i + n)])
        return carry

    lax.fori_loop(0, nc // CONV_CHUNKS, conv_group, 0)

    st[...] = jnp.zeros(st.shape, F32)
    row = lax.broadcasted_iota(jnp.int32, (c, c), 0)
    col = lax.broadcasted_iota(jnp.int32, (c, c), 1)
    eye = row == col
    masks = ((row >= col, row > col), (row <= col, row < col))
    nh = DN_HEADS

    n_g = N_DIRS * nh
    n_r = abr_ref.shape[1]
    ab = jnp.concatenate([abr_ref[ci] for ci in range(nc)], axis=0)
    g = -jnp.exp(jnp.tile(alog_ref[...], (nc, 1))) * _softplus(ab + jnp.tile(dtb_ref[...], (nc, 1)))
    beta = jax.nn.sigmoid(ab)
    pre, suf = _cumsum_both(g, 1)
    cum = jnp.where(lax.broadcasted_iota(jnp.int32, g.shape, 0) % n_r < nh, pre, suf)
    gtot_all = jnp.broadcast_to(jnp.sum(g, axis=1, keepdims=True), g.shape)
    for ci in range(nc):
        rows = slice(ci * n_r, (ci + 1) * n_r)
        gcum[ci] = cum[rows]
        gbeta[ci] = beta[rows]
        gsum[ci] = gtot_all[rows]
        packed = jnp.concatenate([cum[ci * n_r:ci * n_r + n_g], -beta[ci * n_r + n_g:(ci + 1) * n_r],
                                  jnp.zeros((c - 2 * n_g, c), F32)], axis=0)
        gcols[ci] = packed.T

    def prep(ci, h, d, slot):
        j = d * nh + h
        jb = N_DIRS * nh + j
        gc_c = gcols[ci, :, j:j + 1]
        gc_r = gcum[ci, j:j + 1, :]
        neg_b_c = gcols[ci, :, jb:jb + 1]
        b_r = gbeta[ci, jb:jb + 1, :]
        gtot = gsum[ci, j:j + 1, 0:1]
        incl, strict = masks[d]
        hs = slice(h * hd, (h + 1) * hd)
        ktc = kt[ci, hs, :]
        qk2 = jnp.concatenate([qn[ci, :, hs], kn[ci, :, hs]], axis=0)
        gram = yield (qk2, ktc)
        decay = jnp.exp(jnp.where(incl, gc_c - gc_r, -jnp.inf))
        nmat = jnp.where(strict, gram[c:] * neg_b_c * decay, 0.0)
        y = yield from _unit_tri_inverse(nmat, row ^ col, lower=(d == 0))
        tinv = y + jnp.where(eye, 1.0, 0.0)
        eg_r = jnp.exp(gc_r)
        z = jnp.zeros((c, hd), BF16)
        vk = jnp.concatenate([jnp.concatenate([vn[ci, :, hs], z], axis=1),
                              jnp.concatenate([z, kn[ci, :, hs]], axis=1)], axis=0)
        uw = jnp.dot(jnp.concatenate([tinv * b_r, tinv * (b_r * eg_r)], axis=1).astype(BF16), vk,
                     preferred_element_type=F32)
        yield
        pu[slot] = uw[:, :hd]
        pw[slot] = uw[:, hd:].astype(BF16)
        pm2[slot] = jnp.concatenate([jnp.where(eye, eg_r, 0.0), gram[:c] * decay], axis=1).astype(BF16)
        pkte[slot] = (ktc.astype(F32) * jnp.exp(gtot - gc_r)).astype(BF16)
        pgl[slot] = jnp.broadcast_to(jnp.exp(gtot), pgl.shape[1:])

    def scan(ci, h, d, slot):
        j = d * nh + h
        hs = slice(h * hd, (h + 1) * hd)
        wq = jnp.concatenate([pw[slot], qn[ci, :, hs]], axis=0)
        pm = yield (wq, st[j].astype(BF16))
        vb = (pu[slot] - pm[:c]).astype(BF16)
        o = jnp.dot(pm2[slot], jnp.concatenate([pm[c:].astype(BF16), vb], axis=0), preferred_element_type=F32)
        ds = yield (pkte[slot], vb)
        st[j] = st[j] * pgl[slot][0:1, :] + ds
        oacc[ci, :, hs] = oacc[ci, :, hs] + o

    def chunk_of(step_idx, d):
        return step_idx if d == 0 else nc - 1 - step_idx

    n_chain = N_DIRS * nh
    group = pm2.shape[0] // n_chain

    def preps(step0):
        return [prep(chunk_of(step0 + g, d), h, d, g * n_chain + d * nh + h)
                for g in range(group) for d in range(N_DIRS) for h in range(nh)]

    def scans(step0):
        def run(h, d):
            for g in range(group):
                yield from scan(chunk_of(step0 + g, d), h, d, g * n_chain + d * nh + h)
        return [run(h, d) for d in range(N_DIRS) for h in range(nh)]

    oacc[...] = jnp.zeros(oacc.shape, F32)
    _lockstep(preps(0))

    def group_of_steps(i, carry):
        s0 = i * group
        _lockstep(scans(s0))
        _lockstep(preps(s0 + group))
        return carry

    lax.fori_loop(0, nc // group - 1, group_of_steps, 0)
    _lockstep(scans(nc - group))

    def finish(ci, carry):
        o_ref[pl.ds(pl.multiple_of(ci * c, c), c), :] = oacc[ci].astype(BF16)
        return carry

    lax.fori_loop(0, nc, finish, 0)


def _deltanet(dqkv, abr, conv_w, alog_c, dtb_c, batch, seq):
    width = DN_HEADS * DN_HEAD_DIM
    nc = seq // DN_CHUNK
    n_ab = abr.shape[1]
    col = lambda j: pl.BlockSpec((seq, width), lambda b, j=j: (b, j))
    return pl.pallas_call(
        _deltanet_kernel,
        out_shape=jax.ShapeDtypeStruct((batch * seq, width), BF16),
        grid=(batch,),
        in_specs=[col(0), col(1), col(2),
                  pl.BlockSpec((nc, n_ab, DN_CHUNK), lambda b: (b, 0, 0)),
                  _const_spec(conv_w.shape),
                  _const_spec(alog_c.shape), _const_spec(dtb_c.shape)],
        out_specs=col(0),
        scratch_shapes=[
            pltpu.VMEM((3, (DN_CONV - 1) * DN_CHUNK, 2 * DN_CHUNK), BF16),
            pltpu.VMEM((nc, DN_CHUNK, width), BF16),
            pltpu.VMEM((nc, DN_CHUNK, width), BF16),
            pltpu.VMEM((nc, DN_CHUNK, width), BF16),
            pltpu.VMEM((nc, width, DN_CHUNK), BF16),
            pltpu.VMEM((nc, DN_CHUNK, width), F32),
            pltpu.VMEM((N_DIRS * DN_HEADS, DN_HEAD_DIM, DN_HEAD_DIM), F32),
            pltpu.VMEM((DN_GROUP * N_DIRS * DN_HEADS, DN_CHUNK, DN_HEAD_DIM), F32),
            pltpu.VMEM((DN_GROUP * N_DIRS * DN_HEADS, DN_CHUNK, DN_HEAD_DIM), BF16),
            pltpu.VMEM((DN_GROUP * N_DIRS * DN_HEADS, DN_CHUNK, 2 * DN_CHUNK), BF16),
            pltpu.VMEM((DN_GROUP * N_DIRS * DN_HEADS, DN_HEAD_DIM, DN_CHUNK), BF16),
            pltpu.VMEM((DN_GROUP * N_DIRS * DN_HEADS, SUBLANES, LANES), F32),
            pltpu.VMEM((nc, n_ab, DN_CHUNK), F32),
            pltpu.VMEM((nc, n_ab, DN_CHUNK), F32),
            pltpu.VMEM((nc, DN_CHUNK, LANES), F32),
            pltpu.VMEM((nc, n_ab, DN_CHUNK), F32),
        ],
        compiler_params=_cparams(1),
        name="deltanet",
    )(dqkv, dqkv, dqkv, abr, conv_w, alog_c, dtb_c)


def _mem_kv_kernel(mem_ref, gain_ref, w32_ref, k_ref, v_ref, w_ref):
    d = k_ref.shape[1]
    _cast_once(w32_ref, w_ref)
    mn = _rms(mem_ref[...], gain_ref[...]).astype(BF16)
    k_ref[...] = jnp.dot(mn, w_ref[:, :d], preferred_element_type=F32).astype(BF16)
    v_ref[...] = jnp.dot(mn, w_ref[:, d:], preferred_element_type=F32).astype(BF16)


def _mem_kv(mem2, gain, w_ckv, mem_len):
    rows, d = mem2.shape
    spec = pl.BlockSpec((mem_len, d), lambda b: (b, 0))
    return pl.pallas_call(
        _mem_kv_kernel,
        out_shape=(jax.ShapeDtypeStruct((rows, d), BF16),) * 2,
        grid=(rows // mem_len,),
        in_specs=[spec, _const_spec((1, d)), _const_spec(w_ckv.shape)],
        out_specs=(spec, spec),
        scratch_shapes=[pltpu.VMEM(w_ckv.shape, BF16)],
        compiler_params=_cparams(1),
        name="mem_kv",
    )(mem2, gain, w_ckv)


def _out_cross_kernel(x_ref, ao_ref, do_ref, dgate_ref, gdn_ref, wout32_ref, gpost_ref, gpre_ref, wq32_ref, km_ref, vm_ref,
                      wo32_ref, gcpost_ref, y_ref, wout_ref, wq_ref, wo_ref):
    _cast_once(wout32_ref, wout_ref)
    _cast_once(wq32_ref, wq_ref)
    _cast_once(wo32_ref, wo_ref)
    aw = ao_ref.shape[1]
    d = x_ref.shape[1]
    chd = d // CROSS_HEADS
    nt = (((1,), (1,)), ((), ()))

    def rows_chain(r0):
        rs = slice(r0, r0 + ROW_SUB)
        dn = []
        for h in range(DN_HEADS):
            hs = slice(h * DN_HEAD_DIM, (h + 1) * DN_HEAD_DIM)
            gated = _rms(do_ref[rs, hs].astype(F32), gdn_ref[...]) * _silu(dgate_ref[rs, hs].astype(F32))
            dn.append(gated.astype(BF16))
        mix = (jnp.dot(ao_ref[rs, :], wout_ref[:aw, :], preferred_element_type=F32)
               + jnp.dot(jnp.concatenate(dn, axis=1), wout_ref[aw:, :], preferred_element_type=F32))
        yield
        x1 = x_ref[rs, :] + _rms(mix, gpost_ref[...])
        q = jnp.dot(_rms(x1, gpre_ref[...]).astype(BF16), wq_ref[...], preferred_element_type=F32)
        yield
        q = (q * (chd ** -0.5)).astype(BF16)
        hcols = [slice(h * chd, (h + 1) * chd) for h in range(CROSS_HEADS)]
        scores = [lax.dot_general(q[:, hs], km_ref[:, hs], nt, preferred_element_type=F32) for hs in hcols]
        yield
        probs = [jnp.exp(s - jnp.max(s, axis=-1, keepdims=True)) for s in scores]
        outs = [jnp.dot(p.astype(BF16), vm_ref[:, hs], preferred_element_type=F32) for p, hs in zip(probs, hcols)]
        yield
        heads = [(o / jnp.sum(p, axis=-1, keepdims=True)).astype(BF16) for o, p in zip(outs, probs)]
        cproj = jnp.dot(jnp.concatenate(heads, axis=1), wo_ref[...], preferred_element_type=F32)
        yield
        y_ref[rs, :] = x1 + _rms(cproj, gcpost_ref[...])

    _lockstep([rows_chain(r0) for r0 in range(0, x_ref.shape[0], ROW_SUB)])


def _out_cross(x2, attn_o, dn_raw, dgate, g_dn, w_out, g_post, g_pre, w_cq, kmem, vmem, w_co, g_cpost, seq, mem_len, tm):
    t, d = x2.shape
    per_batch = seq // tm
    row = lambda w: pl.BlockSpec((tm, w), lambda i: (i, 0))
    memspec = pl.BlockSpec((mem_len, d), lambda i: (i // per_batch, 0))
    vec = _const_spec((1, d))
    return pl.pallas_call(
        _out_cross_kernel,
        out_shape=jax.ShapeDtypeStruct((t, d), F32),
        grid=(t // tm,),
        in_specs=[row(d), row(attn_o.shape[1]), row(dn_raw.shape[1]), row(dgate.shape[1]), _const_spec(g_dn.shape),
                  _const_spec(w_out.shape), vec, vec,
                  _const_spec(w_cq.shape), memspec, memspec, _const_spec(w_co.shape), vec],
        out_specs=row(d),
        scratch_shapes=[pltpu.VMEM(w_out.shape, BF16), pltpu.VMEM(w_cq.shape, BF16), pltpu.VMEM(w_co.shape, BF16)],
        compiler_params=_cparams(1),
        name="out_cross",
    )(x2, attn_o, dn_raw, dgate, g_dn, w_out, g_post, g_pre, w_cq, kmem, vmem, w_co, g_cpost)


def _ffn_kernel(x_ref, gpre_ref, wgu_ref, wdown_ref, gpost_ref, y_ref):
    dff = wdown_ref.shape[0]

    def rows_chain(r0):
        rs = slice(r0, r0 + ROW_SUB)
        hf = _rms(x_ref[rs, :], gpre_ref[...]).astype(BF16)
        gu = jnp.dot(hf, wgu_ref[...], preferred_element_type=F32)
        yield
        act = (_silu(gu[:, :dff]) * gu[:, dff:]).astype(BF16)
        f = jnp.dot(act, wdown_ref[...], preferred_element_type=F32)
        yield
        y_ref[rs, :] = x_ref[rs, :] + _rms(f, gpost_ref[...])

    _lockstep([rows_chain(r0) for r0 in range(0, x_ref.shape[0], ROW_SUB)])


def _ffn(x2, g_pre, w_gu, w_down, g_post, tm):
    t, d = x2.shape
    row = pl.BlockSpec((tm, d), lambda i: (i, 0))
    vec = _const_spec((1, d))
    return pl.pallas_call(
        _ffn_kernel,
        out_shape=jax.ShapeDtypeStruct((t, d), F32),
        grid=(t // tm,),
        in_specs=[row, vec, _const_spec(w_gu.shape), _const_spec(w_down.shape), vec],
        out_specs=row,
        compiler_params=_cparams(1),
        name="ffn",
    )(x2, g_pre, w_gu, w_down, g_post)


def _layer(x, mem, positions, g_mix_pre, w_in, conv_w, a_log, dt_bias, g_dn_out, attn_sink, w_out,
           g_mix_post, g_cross_pre, g_mem, w_cq, w_ckv, w_co, g_cross_post, g_ffn_pre, w_gate_up,
           w_down, g_ffn_post):
    batch, seq, d = x.shape
    mem_len = mem.shape[1]
    t = batch * seq
    x2 = x.reshape(t, d)
    vec = lambda g: g.reshape(1, -1).astype(F32)

    half = ATTN_HEAD_DIM // 2
    inv_freq = ROPE_THETA ** (-jnp.arange(half, dtype=F32) / half)
    invf_row = jnp.tile(inv_freq, LANES // half).reshape(1, LANES)

    n_ab = 2 * N_DIRS * DN_HEADS
    w_t = w_in.astype(F32).T
    q, k, v, dqkv, dgate, abr = _in_proj(x2, positions.reshape(t // LANES, LANES), invf_row, vec(g_mix_pre),
                                              w_t, n_ab, tm=min(IN_PROJ_ROWS, t))

    attn_o = _win_attn(attn_sink.astype(F32), q, k, v, batch, seq, nq=min(ATTN_Q_BLOCKS, seq // ATTN_BLOCK))

    pad_c = lambda p: jnp.pad(p.reshape(-1, 1).astype(F32), ((0, n_ab - N_DIRS * DN_HEADS), (0, 0)))
    dn_raw = _deltanet(dqkv, abr, conv_w.astype(F32), pad_c(a_log), pad_c(dt_bias), batch, seq)

    kmem, vmem = _mem_kv(mem.reshape(batch * mem_len, d), vec(g_mem), w_ckv.astype(F32), mem_len)
    x3 = _out_cross(x2, attn_o, dn_raw, dgate, vec(g_dn_out), w_out.astype(F32), vec(g_mix_post), vec(g_cross_pre),
                    w_cq.astype(F32), kmem, vmem, w_co.astype(F32), vec(g_cross_post), seq, mem_len, tm=min(OUT_CROSS_ROWS, seq))
    y = _ffn(x3, vec(g_ffn_pre), w_gate_up.astype(BF16), w_down.astype(BF16), vec(g_ffn_post), tm=FFN_ROWS)
    return y.reshape(batch, seq, d)


def kernel(x, mem, positions, g_mix_pre, w_in, conv_w, a_log, dt_bias, g_dn_out, attn_sink, w_out,
           g_mix_post, g_cross_pre, g_mem, w_cq, w_ckv, w_co, g_cross_post, g_ffn_pre, w_gate_up,
           w_down, g_ffn_post):
    depth = w_in.shape[0]
    for l in range(depth):
        x = _layer(x, mem, positions, g_mix_pre[l], w_in[l], conv_w[l], a_log[l], dt_bias[l], g_dn_out[l],
                   attn_sink[l], w_out[l], g_mix_post[l], g_cross_pre[l], g_mem[l], w_cq[l], w_ckv[l],
                   w_co[l], g_cross_post[l], g_ffn_pre[l], w_gate_up[l], w_down[l], g_ffn_post[l])
    return x
```

```python
import jax
import jax.numpy as jnp
from jax import lax
from jax.experimental import pallas as pl
from jax.experimental.pallas import tpu as pltpu

F32 = jnp.float32
BF16 = jnp.bfloat16

EPS = 1e-6
LOG2E = 1.4426950408889634
ROPE_THETA = 10000.0
ATTN_HEADS = 8
ATTN_KV_HEADS = 2
ATTN_HEAD_DIM = 64
ATTN_BLOCK = 128
DN_HEADS = 4
DN_HEAD_DIM = 128
DN_CONV = 5
N_DIRS = 2
CROSS_HEADS = 4

LANES = 128
SUBLANES = 8
VMEM_LIMIT = 56 * 2 ** 20

DN_CHUNK = LANES
CONV_CHUNKS = 4
DN_GROUP = 2
TRI_BASE = 16
ROW_SUB = 256
CAST_ROWS = 256
IN_PROJ_ROWS = 1024
OUT_CROSS_ROWS = 1024
FFN_ROWS = 512
ATTN_Q_BLOCKS = 4


def _cparams(n_grid_dims):
    return pltpu.CompilerParams(
        dimension_semantics=("arbitrary",) * n_grid_dims,
        vmem_limit_bytes=VMEM_LIMIT,
    )


def _rms(x, gain):
    return x * lax.rsqrt(jnp.mean(x * x, axis=-1, keepdims=True) + EPS) * gain


def _silu(x):
    return x * jax.nn.sigmoid(x)


def _softplus(x):
    return jnp.maximum(x, 0.0) + jnp.log(1.0 + jnp.exp(-jnp.abs(x)))


def _cast_once(src_ref, dst_ref):
    @pl.when(pl.program_id(0) == 0)
    def _():
        for r in range(0, src_ref.shape[0], CAST_ROWS):
            dst_ref[r:r + CAST_ROWS, :] = src_ref[r:r + CAST_ROWS, :].astype(BF16)


def _const_spec(shape):
    nd = len(shape)
    return pl.BlockSpec(shape, lambda *_: (0,) * nd, pipeline_mode=pl.Buffered(1))


def _in_proj_kernel(x_ref, pos_ref, invf_ref, gain_ref, wt_ref, wgu32_ref, wdown32_ref,
                    q_ref, k_ref, v_ref, dqkv_ref, gate_ref, abr_ref, wgu_ref, wdown_ref, w_ref):
    wgu_ref[...] = wgu32_ref[...].astype(BF16)
    wdown_ref[...] = wdown32_ref[...].astype(BF16)
    half = ATTN_HEAD_DIM // 2
    lane = lax.broadcasted_iota(jnp.int32, (1, LANES), 1)
    first_half = (lane % ATTN_HEAD_DIM) < half
    aw = ATTN_HEADS * ATTN_HEAD_DIM
    kvw = ATTN_KV_HEADS * ATTN_HEAD_DIM
    off = aw + 2 * kvw
    dn = dqkv_ref.shape[1]
    per_sub = ROW_SUB // DN_CHUNK
    n_main = off + dn + gate_ref.shape[1]

    @pl.when(pl.program_id(0) == 0)
    def _():
        for n in range(n_main // LANES):
            cols = slice(n * LANES, (n + 1) * LANES)
            w_ref[:, cols] = wt_ref[cols, :].T.astype(BF16)

    w_abt = wt_ref[n_main:, :].astype(BF16)

    def rows_chain(r0):
        rs = slice(r0, r0 + ROW_SUB)
        h = _rms(x_ref[rs, :], gain_ref[...]).astype(BF16)
        proj = lambda lo, hi: jnp.dot(h, w_ref[:, lo:hi], preferred_element_type=F32)
        q = proj(0, aw)
        kv = proj(aw, off)
        dqkv = proj(off, off + dn)
        gate = proj(off + dn, off + dn + gate_ref.shape[1])
        abr = lax.dot_general(w_abt, h, (((1,), (1,)), ((), ())), preferred_element_type=F32)
        yield
        pos = jnp.concatenate(
            [jnp.broadcast_to(pos_ref[g:g + 1, :].astype(F32), (LANES, LANES)).T
             for g in range(r0 // LANES, (r0 + ROW_SUB) // LANES)], axis=0)
        ang = pos * invf_ref[...]
        cos = jnp.cos(ang)
        sin = jnp.sin(ang)

        def rope(t):
            rot = jnp.where(first_half, -pltpu.roll(t, LANES - half, 1), pltpu.roll(t, half, 1))
            return t * cos + rot * sin

        for m in range(aw // LANES):
            cols = slice(m * LANES, (m + 1) * LANES)
            q_ref[rs, cols] = (rope(q[:, cols]) * (ATTN_HEAD_DIM ** -0.5 * LOG2E)).astype(BF16)
        k_ref[rs, :] = rope(kv[:, :kvw]).astype(BF16)
        v_ref[rs, :] = kv[:, kvw:].astype(BF16)
        dqkv_ref[rs, :] = dqkv.astype(BF16)
        gate_ref[rs, :] = gate.astype(BF16)
        for j in range(per_sub):
            abr_ref[r0 // DN_CHUNK + j] = abr[:, j * DN_CHUNK:(j + 1) * DN_CHUNK]

    _lockstep([rows_chain(r0) for r0 in range(0, x_ref.shape[0], ROW_SUB)])


def _in_proj(x2, pos_rows, invf_row, gain, w_t, n_ab, w_gu, w_down, tm):
    t, d = x2.shape
    aw = ATTN_HEADS * ATTN_HEAD_DIM
    kvw = ATTN_KV_HEADS * ATTN_HEAD_DIM
    dnw = DN_HEADS * DN_HEAD_DIM
    row = lambda w: pl.BlockSpec((tm, w), lambda i: (i, 0))
    out_shape = (
        jax.ShapeDtypeStruct((t, aw), BF16),
        jax.ShapeDtypeStruct((t, kvw), BF16),
        jax.ShapeDtypeStruct((t, kvw), BF16),
        jax.ShapeDtypeStruct((t, 3 * dnw), BF16),
        jax.ShapeDtypeStruct((t, dnw), BF16),
        jax.ShapeDtypeStruct((t // DN_CHUNK, n_ab, DN_CHUNK), F32),
        jax.ShapeDtypeStruct(w_gu.shape, BF16),
        jax.ShapeDtypeStruct(w_down.shape, BF16),
    )
    steps = t // tm
    slab = lambda w: pl.BlockSpec((w.shape[0] // steps, w.shape[1]), lambda i: (i, 0))
    out_specs = (
        row(aw), row(kvw), row(kvw), row(3 * dnw), row(dnw),
        pl.BlockSpec((tm // DN_CHUNK, n_ab, DN_CHUNK), lambda i: (i, 0, 0)),
        slab(w_gu), slab(w_down),
    )
    return pl.pallas_call(
        _in_proj_kernel,
        out_shape=out_shape,
        grid=(steps,),
        in_specs=[row(d), pl.BlockSpec((tm // LANES, LANES), lambda i: (i, 0)), _const_spec((1, LANES)),
                  _const_spec((1, d)), _const_spec(w_t.shape), slab(w_gu), slab(w_down)],
        out_specs=out_specs,
        scratch_shapes=[pltpu.VMEM((d, w_t.shape[0] - n_ab), BF16)],
        compiler_params=_cparams(1),
        name="in_proj",
    )(x2, pos_rows, invf_row, gain, w_t, w_gu, w_down)


def _paired_dots(reqs):
    out = [None] * len(reqs)
    todo = [i for i, r in enumerate(reqs) if r is not None]
    while todo:
        i = todo.pop(0)
        l1, r1 = reqs[i]
        j = next((t for t in todo if reqs[t][0].shape == l1.shape and reqs[t][1].shape == r1.shape), None)
        if j is None or r1.shape[1] != LANES:
            out[i] = jnp.dot(l1, r1, preferred_element_type=F32)
            continue
        todo.remove(j)
        l2, r2 = reqs[j]
        z = jnp.zeros_like(r1)
        rhs = jnp.concatenate([jnp.concatenate([r1, z], axis=1), jnp.concatenate([z, r2], axis=1)], axis=0)
        both = jnp.dot(jnp.concatenate([l1, l2], axis=1), rhs, preferred_element_type=F32)
        out[i] = both[:, :LANES]
        out[j] = both[:, LANES:]
    return out


def _lockstep(gens):
    live = list(gens)
    sends = [None] * len(live)
    while live:
        reqs, still = [], []
        for g, val in zip(live, sends):
            try:
                reqs.append(g.send(val))
                still.append(g)
            except StopIteration:
                pass
        live = still
        sends = _paired_dots(reqs)


def _win_attn_kernel(sink_ref, q_ref, kp_ref, kc_ref, kn_ref, vp_ref, vc_ref, vn_ref, o_ref):
    n = pl.program_id(1)
    ns = pl.num_programs(1)
    blk = ATTN_BLOCK
    hd = ATTN_HEAD_DIM
    nq = q_ref.shape[0] // blk
    group = ATTN_HEADS // ATTN_KV_HEADS
    lane = lax.broadcasted_iota(jnp.int32, (1, LANES), 1)
    lo = lane < hd

    def halves(x):
        swapped = pltpu.roll(x.astype(F32), hd, 1).astype(BF16)
        z = jnp.zeros_like(x)
        return ((jnp.where(lo, x, z), jnp.where(lo, z, swapped)),
                (jnp.where(lo, swapped, z), jnp.where(lo, z, x)))

    kh = [halves(kp_ref[...])] + [halves(kc_ref[i * blk:(i + 1) * blk, :]) for i in range(nq)] + [halves(kn_ref[...])]
    vh = [halves(vp_ref[...])] + [halves(vc_ref[i * blk:(i + 1) * blk, :]) for i in range(nq)] + [halves(vn_ref[...])]

    def blockdiag(parts, i, g):
        return jnp.concatenate([parts[i + w][g][e] for e in range(2) for w in range(3)], axis=0)

    qi = lax.broadcasted_iota(jnp.int32, (blk, blk), 0)
    ki = lax.broadcasted_iota(jnp.int32, (blk, blk), 1)
    ninf = jnp.float32(-jnp.inf)
    zer = jnp.zeros((blk, blk), F32)

    def band_bias(i):
        has_prev = jnp.where(n > 0, 0.0, ninf) if i == 0 else jnp.float32(0.0)
        has_next = jnp.where(n < ns - 1, 0.0, ninf) if i == nq - 1 else jnp.float32(0.0)
        return jnp.concatenate([jnp.where(ki >= qi, has_prev, ninf), zer,
                                jnp.where(ki <= qi, has_next, ninf)], axis=1)

    def chain(i, m, kbd, vbd, bias):
        rows = slice(i * blk, (i + 1) * blk)
        cols = slice(m * LANES, (m + 1) * LANES)
        s = lax.dot_general(q_ref[rows, cols], kbd, (((1,), (1,)), ((), ())), preferred_element_type=F32)
        yield
        ps, ds = [], []
        for e in range(2):
            sk = sink_ref[2 * m + e] * LOG2E
            se = s[:, e * 3 * blk:(e + 1) * 3 * blk] + bias
            mx = jnp.maximum(jnp.max(se, axis=-1, keepdims=True), sk)
            p = jnp.exp2(se - mx)
            ds.append(jnp.sum(p, axis=-1, keepdims=True) + jnp.exp2(sk - mx))
            ps.append(p.astype(BF16))
        o = jnp.dot(jnp.concatenate(ps, axis=1), vbd, preferred_element_type=F32)
        yield
        o_ref[rows, cols] = (o / jnp.where(lo, ds[0], ds[1])).astype(BF16)

    chains = []
    for i in range(nq):
        bias = band_bias(i)
        for g in range(ATTN_KV_HEADS):
            kbd = blockdiag(kh, i, g)
            vbd = blockdiag(vh, i, g)
            for m in range(g * group // 2, (g + 1) * group // 2):
                chains.append(chain(i, m, kbd, vbd, bias))
    _lockstep(chains)


def _win_attn(sink, q, k, v, batch, seq, nq):
    nb = seq // ATTN_BLOCK
    ns = nb // nq
    aw = q.shape[1]
    kvw = k.shape[1]
    cur = lambda b, n: (b * ns + n, 0)
    prev = lambda b, n: (b * nb + jnp.maximum(n * nq - 1, 0), 0)
    nxt = lambda b, n: (b * nb + jnp.minimum(n * nq + nq, nb - 1), 0)
    edge = lambda f: pl.BlockSpec((ATTN_BLOCK, kvw), f)
    own = pl.BlockSpec((nq * ATTN_BLOCK, kvw), cur)
    return pl.pallas_call(
        _win_attn_kernel,
        out_shape=jax.ShapeDtypeStruct(q.shape, BF16),
        grid=(batch, ns),
        in_specs=[pl.BlockSpec(memory_space=pltpu.SMEM),
                  pl.BlockSpec((nq * ATTN_BLOCK, aw), cur),
                  edge(prev), own, edge(nxt), edge(prev), own, edge(nxt)],
        out_specs=pl.BlockSpec((nq * ATTN_BLOCK, aw), cur),
        compiler_params=_cparams(2),
        name="win_attn",
    )(sink, q, k, k, k, v, v, v)


def _cumsum_both(x, axis):
    n = x.shape[axis]
    idx = lax.broadcasted_iota(jnp.int32, x.shape, axis)
    pre, suf = x, x
    s = 1
    while s < n:
        pre = pre + jnp.where(idx >= s, pltpu.roll(pre, s, axis), 0.0)
        suf = suf + jnp.where(idx < n - s, pltpu.roll(suf, n - s, axis), 0.0)
        s *= 2
    return pre, suf


def _unit_tri_inverse(nmat, rc_xor, lower):
    c = nmat.shape[0]
    n0 = jnp.where(rc_xor < TRI_BASE, nmat, 0.0)
    nb16 = n0.astype(BF16)
    p = yield (nb16, nb16)
    y = n0
    n_sq = (TRI_BASE - 1).bit_length() - 1
    for m in range(n_sq):
        pb = p.astype(BF16)
        if m < n_sq - 1:
            r2 = jnp.dot(pb, jnp.concatenate([y.astype(BF16), pb], axis=1), preferred_element_type=F32)
            yield
            y = y + p + r2[:, :c]
            p = r2[:, c:]
        else:
            py = yield (pb, y.astype(BF16))
            y = y + p + py
    neg_n = -nmat
    blk = TRI_BASE
    while blk < c:
        first = blk if lower else 0
        held = [slice(s, s + blk) for s in range(first, c, 2 * blk)]
        kept = [slice(s, s + blk) for s in range(blk - first, c, 2 * blk)]
        take = lambda a: jnp.concatenate([a[s] for s in held], axis=0)
        rx = take(rc_xor)
        l_rows = jnp.where((rx >= blk) & (rx < 2 * blk), take(neg_n), 0.0)
        y_rows = take(y)
        ly = yield (l_rows.astype(BF16), y.astype(BF16))
        u_rows = l_rows + ly
        ub = u_rows.astype(BF16)
        zero = jnp.zeros((blk, c), BF16)
        pieces = [ub[i * blk:(i + 1) * blk] for i in range(len(held))]
        order = [zero, None] if lower else [None, zero]
        u_full = jnp.concatenate([(pieces[i] if o is None else o) for i in range(len(held)) for o in order], axis=0)
        yu = yield (y_rows.astype(BF16), u_full)
        new_rows = y_rows - u_rows - yu
        parts = {}
        for i, s in enumerate(held):
            parts[s.start] = new_rows[i * blk:(i + 1) * blk]
        for s in kept:
            parts[s.start] = y[s]
        y = jnp.concatenate([parts[k] for k in sorted(parts)], axis=0)
        blk *= 2
    return y


def _deltanet_kernel(dq_ref, dk_ref, dv_ref, abr_ref, cw_ref, alog_ref, dtb_ref,
                     o_ref, shifts, qn, kn, vn, kt, oacc, st, pu, pw, pm2, pkte, pgl, gcum, gbeta, gcols, gsum):
    seq = dq_ref.shape[0]
    nc = seq // DN_CHUNK
    c = DN_CHUNK
    hd = DN_HEAD_DIM
    width = DN_HEADS * hd

    win = shifts.shape[2]
    pad = DN_CONV // 2
    lead = (win - c) // 2
    side_taps = [j for j in range(DN_CONV) if j != pad]

    @pl.when(pl.program_id(0) == 0)
    def _():
        tok = lax.broadcasted_iota(jnp.int32, (c, win), 0)
        src = lax.broadcasted_iota(jnp.int32, (c, win), 1)
        for v in range(shifts.shape[0]):
            shifts[v] = jnp.concatenate(
                [jnp.where(src == tok + (j - pad + v * lead), 1.0, 0.0).astype(BF16) for j in side_taps], axis=0)

    def conv_chunk(i):
        r = pl.multiple_of(i * c, c)
        start = pl.multiple_of(jnp.clip(r - lead, 0, seq - win), lead)
        sel = shifts[(r - start) // lead]

        def one(src_ref, w_off, dst, l2, scale):
            taps = jnp.dot(sel, src_ref[pl.ds(start, win), :], preferred_element_type=F32)
            yield
            for h in range(DN_HEADS):
                hs = slice(h * hd, (h + 1) * hd)
                wrow = lambda j: cw_ref[j:j + 1, w_off + h * hd:w_off + (h + 1) * hd]
                y = wrow(pad) * src_ref[pl.ds(r, c), hs].astype(F32)
                for n, j in enumerate(side_taps):
                    y = y + wrow(j) * taps[n * c:(n + 1) * c, hs]
                y = _silu(y)
                if l2:
                    y = y * lax.rsqrt(jnp.sum(y * y, axis=-1, keepdims=True) + EPS)
                    if scale != 1.0:
                        y = y * scale
                dst[i, :, hs] = y.astype(BF16)
                if dst is kn:
                    kt[i, hs, :] = y.T.astype(BF16)

        return [one(dq_ref, 0, qn, True, hd ** -0.5), one(dk_ref, width, kn, True, 1.0),
                one(dv_ref, 2 * width, vn, False, 1.0)]

    def conv_group(i, carry):
        _lockstep([g for n in range(CONV_CHUNKS) for g in conv_chunk(CONV_CHUNKS * i + n)])
        return carry

    lax.fori_loop(0, nc // CONV_CHUNKS, conv_group, 0)

    st[...] = jnp.zeros(st.shape, F32)
    row = lax.broadcasted_iota(jnp.int32, (c, c), 0)
    col = lax.broadcasted_iota(jnp.int32, (c, c), 1)
    eye = row == col
    masks = ((row >= col, row > col), (row <= col, row < col))
    nh = DN_HEADS

    n_g = N_DIRS * nh
    n_r = abr_ref.shape[1]
    ab = jnp.concatenate([abr_ref[ci] for ci in range(nc)], axis=0)
    g = -jnp.exp(jnp.tile(alog_ref[...], (nc, 1))) * _softplus(ab + jnp.tile(dtb_ref[...], (nc, 1)))
    beta = jax.nn.sigmoid(ab)
    pre, suf = _cumsum_both(g, 1)
    cum = jnp.where(lax.broadcasted_iota(jnp.int32, g.shape, 0) % n_r < nh, pre, suf)
    gtot_all = jnp.broadcast_to(jnp.sum(g, axis=1, keepdims=True), g.shape)
    for ci in range(nc):
        rows = slice(ci * n_r, (ci + 1) * n_r)
        gcum[ci] = cum[rows]
        gbeta[ci] = beta[rows]
        gsum[ci] = gtot_all[rows]
        packed = jnp.concatenate([cum[ci * n_r:ci * n_r + n_g], -beta[ci * n_r + n_g:(ci + 1) * n_r],
                                  jnp.zeros((c - 2 * n_g, c), F32)], axis=0)
        gcols[ci] = packed.T

    def prep(ci, h, d, slot):
        j = d * nh + h
        jb = N_DIRS * nh + j
        gc_c = gcols[ci, :, j:j + 1]
        gc_r = gcum[ci, j:j + 1, :]
        neg_b_c = gcols[ci, :, jb:jb + 1]
        b_r = gbeta[ci, jb:jb + 1, :]
        gtot = gsum[ci, j:j + 1, 0:1]
        incl, strict = masks[d]
        hs = slice(h * hd, (h + 1) * hd)
        ktc = kt[ci, hs, :]
        qk2 = jnp.concatenate([qn[ci, :, hs], kn[ci, :, hs]], axis=0)
        gram = yield (qk2, ktc)
        decay = jnp.exp(jnp.where(incl, gc_c - gc_r, -jnp.inf))
        nmat = jnp.where(strict, gram[c:] * neg_b_c * decay, 0.0)
        y = yield from _unit_tri_inverse(nmat, row ^ col, lower=(d == 0))
        tinv = y + jnp.where(eye, 1.0, 0.0)
        eg_r = jnp.exp(gc_r)
        z = jnp.zeros((c, hd), BF16)
        vk = jnp.concatenate([jnp.concatenate([vn[ci, :, hs], z], axis=1),
                              jnp.concatenate([z, kn[ci, :, hs]], axis=1)], axis=0)
        uw = jnp.dot(jnp.concatenate([tinv * b_r, tinv * (b_r * eg_r)], axis=1).astype(BF16), vk,
                     preferred_element_type=F32)
        yield
        pu[slot] = uw[:, :hd]
        pw[slot] = uw[:, hd:].astype(BF16)
        pm2[slot] = jnp.concatenate([jnp.where(eye, eg_r, 0.0), gram[:c] * decay], axis=1).astype(BF16)
        pkte[slot] = (ktc.astype(F32) * jnp.exp(gtot - gc_r)).astype(BF16)
        pgl[slot] = jnp.broadcast_to(jnp.exp(gtot), pgl.shape[1:])

    def scan(ci, h, d, slot):
        j = d * nh + h
        hs = slice(h * hd, (h + 1) * hd)
        wq = jnp.concatenate([pw[slot], qn[ci, :, hs]], axis=0)
        pm = yield (wq, st[j].astype(BF16))
        vb = (pu[slot] - pm[:c]).astype(BF16)
        o = jnp.dot(pm2[slot], jnp.concatenate([pm[c:].astype(BF16), vb], axis=0), preferred_element_type=F32)
        ds = yield (pkte[slot], vb)
        st[j] = st[j] * pgl[slot][0:1, :] + ds
        oacc[ci, :, hs] = oacc[ci, :, hs] + o

    def chunk_of(step_idx, d):
        return step_idx if d == 0 else nc - 1 - step_idx

    n_chain = N_DIRS * nh
    group = pm2.shape[0] // n_chain

    def preps(step0):
        return [prep(chunk_of(step0 + g, d), h, d, g * n_chain + d * nh + h)
                for g in range(group) for d in range(N_DIRS) for h in range(nh)]

    def scans(step0):
        def run(h, d):
            for g in range(group):
                yield from scan(chunk_of(step0 + g, d), h, d, g * n_chain + d * nh + h)
        return [run(h, d) for d in range(N_DIRS) for h in range(nh)]

    oacc[...] = jnp.zeros(oacc.shape, F32)
    _lockstep(preps(0))

    def group_of_steps(i, carry):
        s0 = i * group
        _lockstep(scans(s0))
        _lockstep(preps(s0 + group))
        return carry

    lax.fori_loop(0, nc // group - 1, group_of_steps, 0)
    _lockstep(scans(nc - group))

    def finish(ci, carry):
        o_ref[pl.ds(pl.multiple_of(ci * c, c), c), :] = oacc[ci].astype(BF16)
        return carry

    lax.fori_loop(0, nc, finish, 0)


def _deltanet(dqkv, abr, conv_w, alog_c, dtb_c, batch, seq):
    width = DN_HEADS * DN_HEAD_DIM
    nc = seq // DN_CHUNK
    n_ab = abr.shape[1]
    col = lambda j: pl.BlockSpec((seq, width), lambda b, j=j: (b, j))
    return pl.pallas_call(
        _deltanet_kernel,
        out_shape=jax.ShapeDtypeStruct((batch * seq, width), BF16),
        grid=(batch,),
        in_specs=[col(0), col(1), col(2),
                  pl.BlockSpec((nc, n_ab, DN_CHUNK), lambda b: (b, 0, 0)),
                  _const_spec(conv_w.shape),
                  _const_spec(alog_c.shape), _const_spec(dtb_c.shape)],
        out_specs=col(0),
        scratch_shapes=[
            pltpu.VMEM((3, (DN_CONV - 1) * DN_CHUNK, 2 * DN_CHUNK), BF16),
            pltpu.VMEM((nc, DN_CHUNK, width), BF16),
            pltpu.VMEM((nc, DN_CHUNK, width), BF16),
            pltpu.VMEM((nc, DN_CHUNK, width), BF16),
            pltpu.VMEM((nc, width, DN_CHUNK), BF16),
            pltpu.VMEM((nc, DN_CHUNK, width), F32),
            pltpu.VMEM((N_DIRS * DN_HEADS, DN_HEAD_DIM, DN_HEAD_DIM), F32),
            pltpu.VMEM((DN_GROUP * N_DIRS * DN_HEADS, DN_CHUNK, DN_HEAD_DIM), F32),
            pltpu.VMEM((DN_GROUP * N_DIRS * DN_HEADS, DN_CHUNK, DN_HEAD_DIM), BF16),
            pltpu.VMEM((DN_GROUP * N_DIRS * DN_HEADS, DN_CHUNK, 2 * DN_CHUNK), BF16),
            pltpu.VMEM((DN_GROUP * N_DIRS * DN_HEADS, DN_HEAD_DIM, DN_CHUNK), BF16),
            pltpu.VMEM((DN_GROUP * N_DIRS * DN_HEADS, SUBLANES, LANES), F32),
            pltpu.VMEM((nc, n_ab, DN_CHUNK), F32),
            pltpu.VMEM((nc, n_ab, DN_CHUNK), F32),
            pltpu.VMEM((nc, DN_CHUNK, LANES), F32),
            pltpu.VMEM((nc, n_ab, DN_CHUNK), F32),
        ],
        compiler_params=_cparams(1),
        name="deltanet",
    )(dqkv, dqkv, dqkv, abr, conv_w, alog_c, dtb_c)


def _mem_kv_kernel(mem_ref, gain_ref, w32_ref, k_ref, v_ref, w_ref):
    d = k_ref.shape[1]
    _cast_once(w32_ref, w_ref)
    mn = _rms(mem_ref[...], gain_ref[...]).astype(BF16)
    k_ref[...] = jnp.dot(mn, w_ref[:, :d], preferred_element_type=F32).astype(BF16)
    v_ref[...] = jnp.dot(mn, w_ref[:, d:], preferred_element_type=F32).astype(BF16)


def _mem_kv(mem2, gain, w_ckv, mem_len):
    rows, d = mem2.shape
    spec = pl.BlockSpec((mem_len, d), lambda b: (b, 0))
    return pl.pallas_call(
        _mem_kv_kernel,
        out_shape=(jax.ShapeDtypeStruct((rows, d), BF16),) * 2,
        grid=(rows // mem_len,),
        in_specs=[spec, _const_spec((1, d)), _const_spec(w_ckv.shape)],
        out_specs=(spec, spec),
        scratch_shapes=[pltpu.VMEM(w_ckv.shape, BF16)],
        compiler_params=_cparams(1),
        name="mem_kv",
    )(mem2, gain, w_ckv)


def _out_cross_kernel(x_ref, ao_ref, do_ref, dgate_ref, gdn_ref, wout32_ref, gpost_ref, gpre_ref, wq32_ref, km_ref, vm_ref,
                      wo32_ref, gcpost_ref, y_ref, wout_ref, wq_ref, wo_ref):
    _cast_once(wout32_ref, wout_ref)
    _cast_once(wq32_ref, wq_ref)
    _cast_once(wo32_ref, wo_ref)
    aw = ao_ref.shape[1]
    d = x_ref.shape[1]
    chd = d // CROSS_HEADS
    nt = (((1,), (1,)), ((), ()))

    def rows_chain(r0):
        rs = slice(r0, r0 + ROW_SUB)
        dn = []
        for h in range(DN_HEADS):
            hs = slice(h * DN_HEAD_DIM, (h + 1) * DN_HEAD_DIM)
            gated = _rms(do_ref[rs, hs].astype(F32), gdn_ref[...]) * _silu(dgate_ref[rs, hs].astype(F32))
            dn.append(gated.astype(BF16))
        mix = (jnp.dot(ao_ref[rs, :], wout_ref[:aw, :], preferred_element_type=F32)
               + jnp.dot(jnp.concatenate(dn, axis=1), wout_ref[aw:, :], preferred_element_type=F32))
        yield
        x1 = x_ref[rs, :] + _rms(mix, gpost_ref[...])
        q = jnp.dot(_rms(x1, gpre_ref[...]).astype(BF16), wq_ref[...], preferred_element_type=F32)
        yield
        q = (q * (chd ** -0.5)).astype(BF16)
        hcols = [slice(h * chd, (h + 1) * chd) for h in range(CROSS_HEADS)]
        scores = [lax.dot_general(q[:, hs], km_ref[:, hs], nt, preferred_element_type=F32) for hs in hcols]
        yield
        probs = [jnp.exp(s - jnp.max(s, axis=-1, keepdims=True)) for s in scores]
        outs = [jnp.dot(p.astype(BF16), vm_ref[:, hs], preferred_element_type=F32) for p, hs in zip(probs, hcols)]
        yield
        heads = [(o / jnp.sum(p, axis=-1, keepdims=True)).astype(BF16) for o, p in zip(outs, probs)]
        cproj = jnp.dot(jnp.concatenate(heads, axis=1), wo_ref[...], preferred_element_type=F32)
        yield
        y_ref[rs, :] = x1 + _rms(cproj, gcpost_ref[...])

    _lockstep([rows_chain(r0) for r0 in range(0, x_ref.shape[0], ROW_SUB)])


def _out_cross(x2, attn_o, dn_raw, dgate, g_dn, w_out, g_post, g_pre, w_cq, kmem, vmem, w_co, g_cpost, seq, mem_len, tm):
    t, d = x2.shape
    per_batch = seq // tm
    row = lambda w: pl.BlockSpec((tm, w), lambda i: (i, 0))
    memspec = pl.BlockSpec((mem_len, d), lambda i: (i // per_batch, 0))
    vec = _const_spec((1, d))
    return pl.pallas_call(
        _out_cross_kernel,
        out_shape=jax.ShapeDtypeStruct((t, d), F32),
        grid=(t // tm,),
        in_specs=[row(d), row(attn_o.shape[1]), row(dn_raw.shape[1]), row(dgate.shape[1]), _const_spec(g_dn.shape),
                  _const_spec(w_out.shape), vec, vec,
                  _const_spec(w_cq.shape), memspec, memspec, _const_spec(w_co.shape), vec],
        out_specs=row(d),
        scratch_shapes=[pltpu.VMEM(w_out.shape, BF16), pltpu.VMEM(w_cq.shape, BF16), pltpu.VMEM(w_co.shape, BF16)],
        compiler_params=_cparams(1),
        name="out_cross",
    )(x2, attn_o, dn_raw, dgate, g_dn, w_out, g_post, g_pre, w_cq, kmem, vmem, w_co, g_cpost)


def _ffn_kernel(x_ref, gpre_ref, wgu_ref, wdown_ref, gpost_ref, y_ref):
    dff = wdown_ref.shape[0]

    def rows_chain(r0):
        rs = slice(r0, r0 + ROW_SUB)
        hf = _rms(x_ref[rs, :], gpre_ref[...]).astype(BF16)
        gu = jnp.dot(hf, wgu_ref[...], preferred_element_type=F32)
        yield
        act = (_silu(gu[:, :dff]) * gu[:, dff:]).astype(BF16)
        f = jnp.dot(act, wdown_ref[...], preferred_element_type=F32)
        yield
        y_ref[rs, :] = x_ref[rs, :] + _rms(f, gpost_ref[...])

    _lockstep([rows_chain(r0) for r0 in range(0, x_ref.shape[0], ROW_SUB)])


def _ffn(x2, g_pre, w_gu, w_down, g_post, tm):
    t, d = x2.shape
    row = pl.BlockSpec((tm, d), lambda i: (i, 0))
    vec = _const_spec((1, d))
    return pl.pallas_call(
        _ffn_kernel,
        out_shape=jax.ShapeDtypeStruct((t, d), F32),
        grid=(t // tm,),
        in_specs=[row, vec, _const_spec(w_gu.shape), _const_spec(w_down.shape), vec],
        out_specs=row,
        compiler_params=_cparams(1),
        name="ffn",
    )(x2, g_pre, w_gu, w_down, g_post)


def _layer(x, mem, positions, g_mix_pre, w_in, conv_w, a_log, dt_bias, g_dn_out, attn_sink, w_out,
           g_mix_post, g_cross_pre, g_mem, w_cq, w_ckv, w_co, g_cross_post, g_ffn_pre, w_gate_up,
           w_down, g_ffn_post):
    batch, seq, d = x.shape
    mem_len = mem.shape[1]
    t = batch * seq
    x2 = x.reshape(t, d)
    vec = lambda g: g.reshape(1, -1).astype(F32)

    half = ATTN_HEAD_DIM // 2
    inv_freq = ROPE_THETA ** (-jnp.arange(half, dtype=F32) / half)
    invf_row = jnp.tile(inv_freq, LANES // half).reshape(1, LANES)

    n_ab = 2 * N_DIRS * DN_HEADS
    w_t = w_in.astype(F32).T
    q, k, v, dqkv, dgate, abr, w_gu16, w_down16 = _in_proj(
        x2, positions.reshape(t // LANES, LANES), invf_row, vec(g_mix_pre), w_t, n_ab,
        w_gate_up.astype(F32), w_down.astype(F32), tm=min(IN_PROJ_ROWS, t))

    attn_o = _win_attn(attn_sink.astype(F32), q, k, v, batch, seq, nq=min(ATTN_Q_BLOCKS, seq // ATTN_BLOCK))

    pad_c = lambda p: jnp.pad(p.reshape(-1, 1).astype(F32), ((0, n_ab - N_DIRS * DN_HEADS), (0, 0)))
    dn_raw = _deltanet(dqkv, abr, conv_w.astype(F32), pad_c(a_log), pad_c(dt_bias), batch, seq)

    kmem, vmem = _mem_kv(mem.reshape(batch * mem_len, d), vec(g_mem), w_ckv.astype(F32), mem_len)
    x3 = _out_cross(x2, attn_o, dn_raw, dgate, vec(g_dn_out), w_out.astype(F32), vec(g_mix_post), vec(g_cross_pre),
                    w_cq.astype(F32), kmem, vmem, w_co.astype(F32), vec(g_cross_post), seq, mem_len, tm=min(OUT_CROSS_ROWS, seq))
    y = _ffn(x3, vec(g_ffn_pre), w_gu16, w_down16, vec(g_ffn_post), tm=FFN_ROWS)
    return y.reshape(batch, seq, d)


def kernel(x, mem, positions, g_mix_pre, w_in, conv_w, a_log, dt_bias, g_dn_out, attn_sink, w_out,
           g_mix_post, g_cross_pre, g_mem, w_cq, w_ckv, w_co, g_cross_post, g_ffn_pre, w_gate_up,
           w_down, g_ffn_post):
    depth = w_in.shape[0]
    for l in range(depth):
        x = _layer(x, mem, positions, g_mix_pre[l], w_in[l], conv_w[l], a_log[l], dt_bias[l], g_dn_out[l],
                   attn_sink[l], w_out[l], g_mix_post[l], g_cross_pre[l], g_mem[l], w_cq[l], w_ckv[l],
                   w_co[l], g_cross_post[l], g_ffn_pre[l], w_gate_up[l], w_down[l], g_ffn_post[l])
    return x
```
